```python
import jax, jax.numpy as jnp
from jax import lax
import numpy as np

D_MODEL = 1024
BATCH = 8
SEQ = 2048
DEPTH = 2

N_HEADS = 16
HEAD_DIM = D_MODEL // N_HEADS
N_MIXERS = 2
GRID_W = 64
NA_ROWS = 8
NA_COLS = 16
DIL_GROUPS = ((128, 1), (512, 4), (2048, 16))
N_GROUPS = len(DIL_GROUPS)
BAND_BLOCK = 128
D_FF = -(-8 * D_MODEL // (3 * 256)) * 256
RMS_EPS = 1e-6
NEG_INF = -1e30
N_A_LAYERS = (DEPTH + 1) // 2
N_B_LAYERS = DEPTH // 2

kernel_name = "hybrid_natten_dilated_encoder"


def rms_norm(x, g):
    xf = x.astype(jnp.float32)
    y = xf * lax.rsqrt(jnp.mean(xf * xf, axis=-1, keepdims=True) + RMS_EPS)
    return (y * g.astype(jnp.float32)).astype(x.dtype)


def alibi_slopes(n):
    return 2.0 ** (-8.0 * jnp.arange(1, n + 1, dtype=jnp.float32) / n)


def swiglu(x, w_gate, w_up, w_down):
    return (jax.nn.silu(x @ w_gate) * (x @ w_up)) @ w_down


def neighbourhood_attention(x, w_qkv, w_o, rpb):
    b, s, _ = x.shape
    rows = s // GRID_W
    kh = min(NA_ROWS, rows)
    qkv = (x @ w_qkv).reshape(b, rows, GRID_W, 3, N_HEADS, HEAD_DIM)
    q, k, v = (jnp.transpose(qkv[:, :, :, i], (0, 3, 1, 2, 4)) for i in range(3))
    q = q * HEAD_DIM ** -0.5
    col = jnp.arange(GRID_W)
    col_start = jnp.clip(col - NA_COLS // 2, 0, GRID_W - NA_COLS)
    col_mask = (col[None, :] >= col_start[:, None]) & (col[None, :] < col_start[:, None] + NA_COLS)
    col_idx = jnp.clip(col[None, :] - col[:, None] + NA_COLS - 1, 0, 2 * NA_COLS - 2)
    rpb_cols = rpb.astype(jnp.float32)[:, :, col_idx]

    def row_block(i):
        rs = jnp.clip(i - kh // 2, 0, rows - kh)
        qi = lax.dynamic_index_in_dim(q, i, axis=2, keepdims=False)
        kr = lax.dynamic_slice_in_dim(k, rs, kh, axis=2)
        vr = lax.dynamic_slice_in_dim(v, rs, kh, axis=2)
        bias = lax.dynamic_slice_in_dim(rpb_cols, rs - i + NA_ROWS - 1, kh, axis=1)
        sc = jnp.einsum('bhqd,bhrkd->bhqrk', qi, kr).astype(jnp.float32)
        sc = sc + jnp.transpose(bias, (0, 2, 1, 3))[None]
        sc = jnp.where(col_mask[:, None, :], sc, NEG_INF)
        p = jax.nn.softmax(sc.reshape(b, N_HEADS, GRID_W, kh * GRID_W), axis=-1).reshape(sc.shape)
        return jnp.einsum('bhqrk,bhrkd->bhqd', p.astype(vr.dtype), vr)

    o = lax.map(row_block, jnp.arange(rows))
    o = jnp.transpose(o, (1, 0, 3, 2, 4)).reshape(b, s, D_MODEL)
    return o @ w_o


def banded_attention(q, k, v, radius, slope_dist):
    n, h, l, dh = q.shape
    nb = -(-l // BAND_BLOCK)
    lp = nb * BAND_BLOCK
    qb = jnp.pad(q, ((0, 0), (0, 0), (0, lp - l), (0, 0))).reshape(n, h, nb, BAND_BLOCK, dh)

    def key_blocks(t):
        tp = jnp.pad(t, ((0, 0), (0, 0), (radius, lp - l + BAND_BLOCK - radius), (0, 0)))
        tp = tp.reshape(n, h, nb + 1, BAND_BLOCK, dh)
        return jnp.concatenate([tp[:, :, :-1], tp[:, :, 1:]], axis=3)

    kb, vb = key_blocks(k), key_blocks(v)
    qi = jnp.arange(lp).reshape(nb, BAND_BLOCK)
    kj = jnp.arange(nb)[:, None] * BAND_BLOCK - radius + jnp.arange(2 * BAND_BLOCK)[None, :]
    dist = jnp.abs(qi[:, :, None] - kj[:, None, :])
    valid = (dist <= radius) & (kj[:, None, :] >= 0) & (kj[:, None, :] < l)
    sc = jnp.einsum('nhbqd,nhbkd->nhbqk', qb, kb).astype(jnp.float32)
    sc = sc - slope_dist[None, :, None, None, None] * dist.astype(jnp.float32)
    sc = jnp.where(valid, sc, NEG_INF)
    lse = jax.nn.logsumexp(sc, axis=-1)
    p = jnp.exp(sc - lse[..., None])
    o = jnp.einsum('nhbqk,nhbkd->nhbqd', p.astype(vb.dtype), vb)
    return o.reshape(n, h, lp, dh)[:, :, :l], lse.reshape(n, h, lp)[:, :, :l]


def dilated_attention(x, w_qkv, w_o):
    b, s, _ = x.shape
    qkv = (x @ w_qkv).reshape(b, s, N_GROUPS, 3, N_HEADS, HEAD_DIM)
    slopes = alibi_slopes(N_HEADS)
    outs, lses = [], []
    for g, (window, dil) in enumerate(DIL_GROUPS):
        radius = window // (2 * dil)
        l = s // dil

        def to_sub(t):
            t = jnp.transpose(t.reshape(b, l, dil, N_HEADS, HEAD_DIM), (0, 2, 3, 1, 4))
            return t.reshape(b * dil, N_HEADS, l, HEAD_DIM)

        q, k, v = (to_sub(qkv[:, :, g, i]) for i in range(3))
        o, lse = banded_attention(q * HEAD_DIM ** -0.5, k, v, radius, slopes * dil)
        o = jnp.transpose(o.reshape(b, dil, N_HEADS, l, HEAD_DIM), (0, 3, 1, 2, 4)).reshape(b, s, N_HEADS, HEAD_DIM)
        lse = jnp.transpose(lse.reshape(b, dil, N_HEADS, l), (0, 3, 1, 2)).reshape(b, s, N_HEADS)
        outs.append(o)
        lses.append(lse)
    alpha = jax.nn.softmax(jnp.stack(lses, axis=0), axis=0)
    o = jnp.einsum('gbsh,gbshd->bshd', alpha, jnp.stack(outs, axis=0).astype(jnp.float32)).astype(x.dtype)
    return o.reshape(b, s, D_MODEL) @ w_o


def setup_inputs(seed: int = 0) -> dict:
    key = jax.random.key(seed)
    ks = jax.random.split(key, 14)
    d = D_MODEL

    def w(k, shape, fan_in):
        return jax.random.normal(k, shape, jnp.float32) * fan_in ** -0.5

    def gain(k):
        return 1.0 + 0.02 * jax.random.normal(k, (DEPTH, d), jnp.float32)

    return {
        "x": jax.random.normal(ks[0], (BATCH, SEQ, d), jnp.float32),
        "norm_mix_pre": gain(ks[1]),
        "norm_mix_post": gain(ks[2]),
        "norm_ffn_pre": gain(ks[3]),
        "norm_ffn_post": gain(ks[4]),
        "na_w_qkv": w(ks[5], (N_A_LAYERS, d, 3 * d), d),
        "na_w_o": w(ks[6], (N_A_LAYERS, d, d), d),
        "na_rpb": 0.5 * jax.random.normal(ks[7], (N_A_LAYERS, N_HEADS, 2 * NA_ROWS - 1, 2 * NA_COLS - 1), jnp.float32),
        "dil_w_qkv": w(ks[8], (N_B_LAYERS, d, N_GROUPS * 3 * d), d),
        "dil_w_o": w(ks[9], (N_B_LAYERS, d, d), d),
        "ffn_w_gate": w(ks[10], (DEPTH, d, D_FF), d),
        "ffn_w_up": w(ks[11], (DEPTH, d, D_FF), d),
        "ffn_w_down": w(ks[12], (DEPTH, D_FF, d), D_FF),
    }


def reference(x, norm_mix_pre, norm_mix_post, norm_ffn_pre, norm_ffn_post, na_w_qkv, na_w_o, na_rpb,
              dil_w_qkv, dil_w_o, ffn_w_gate, ffn_w_up, ffn_w_down):
    for layer in range(DEPTH):
        j = layer // N_MIXERS
        h = rms_norm(x, norm_mix_pre[layer])
        if layer % N_MIXERS == 0:
            h = neighbourhood_attention(h, na_w_qkv[j], na_w_o[j], na_rpb[j])
        else:
            h = dilated_attention(h, dil_w_qkv[j], dil_w_o[j])
        x = x + rms_norm(h, norm_mix_post[layer])
        h = rms_norm(x, norm_ffn_pre[layer])
        x = x + rms_norm(swiglu(h, ffn_w_gate[layer], ffn_w_up[layer], ffn_w_down[layer]), norm_ffn_post[layer])
    return x
```

```python
import functools

import jax
import jax.numpy as jnp
from jax import lax
from jax.experimental import pallas as pl
from jax.experimental.pallas import tpu as pltpu

D_MODEL = 1024
BATCH = 8
SEQ = 2048
DEPTH = 2
N_HEADS = 16
HEAD_DIM = 64
GRID_W = 64
GRID_ROWS = SEQ // GRID_W
NA_ROWS = 8
NA_COLS = 16
DIL_GROUPS = ((128, 1), (512, 4), (2048, 16))
N_GROUPS = len(DIL_GROUPS)
BAND_BLOCK = 128
D_FF = 2816
RMS_EPS = 1e-6
NEG_INF = -1e30
Q_SCALE = HEAD_DIM ** -0.5

LANES = 128
N_PAIRS = D_MODEL // LANES
N_TOK = BATCH * SEQ
TM = 512
FF_SPLITS = ((0, 1536), (1536, 1280))
DIL_TN = 512
VMEM_LIMIT = 56 * 1024 * 1024

_BF16 = jnp.bfloat16
_F32 = jnp.float32


def _rms(x, g):
    ms = jnp.mean(x * x, axis=-1, keepdims=True)
    return (x * lax.rsqrt(ms + RMS_EPS)) * g


def _resident(shape):
    return pl.BlockSpec(shape, lambda *_: (0,) * len(shape), pipeline_mode=pl.Buffered(1))


def _params(*sem):
    return pltpu.CompilerParams(dimension_semantics=sem, vmem_limit_bytes=VMEM_LIMIT)


def _qkv_na_kernel(x_ref, g_ref, w_ref, o_ref):
    h = _rms(x_ref[...], g_ref[...]).astype(_BF16)
    for j in range(3):
        y = jnp.dot(h, w_ref[:, j * D_MODEL:(j + 1) * D_MODEL], preferred_element_type=_F32)
        for c in range(N_PAIRS):
            o_ref[0, j * N_PAIRS + c] = y[:, c * LANES:(c + 1) * LANES].astype(_BF16)


def _qkv_na(x2d, gain, w_bf16):
    tiles_per_seq = SEQ // TM
    return pl.pallas_call(
        _qkv_na_kernel,
        grid=(N_TOK // TM,),
        in_specs=[
            pl.BlockSpec((TM, D_MODEL), lambda i: (i, 0)),
            _resident((1, D_MODEL)),
            _resident((D_MODEL, 3 * D_MODEL)),
        ],
        out_specs=pl.BlockSpec((1, 3 * N_PAIRS, TM, LANES),
                               lambda i: (i // tiles_per_seq, 0, i % tiles_per_seq, 0)),
        out_shape=jax.ShapeDtypeStruct((BATCH, 3 * N_PAIRS, SEQ, LANES), _BF16),
        compiler_params=_params("parallel"),
        name="qkv_na",
    )(x2d, gain, w_bf16)


def _na_kernel(q_ref, k_ref, v_ref, bias_ref, o_ref):
    kh = NA_ROWS
    nk = kh * GRID_W
    lane = lax.broadcasted_iota(jnp.int32, (GRID_W, LANES), 1)
    first_head = lane < HEAD_DIM
    qcol = lax.broadcasted_iota(jnp.int32, (GRID_W, nk), 0)
    kcol = lax.broadcasted_iota(jnp.int32, (GRID_W, nk), 1) % GRID_W
    col_start = jnp.clip(qcol - NA_COLS // 2, 0, GRID_W - NA_COLS)
    col_ok = (kcol >= col_start) & (kcol < col_start + NA_COLS)

    def row(i, carry):
        rs = jnp.clip(i - kh // 2, 0, GRID_ROWS - kh)
        delta = i - rs
        q = q_ref[0, 0, pl.ds(pl.multiple_of(i * GRID_W, GRID_W), GRID_W), :]
        k = k_ref[0, 0, pl.ds(pl.multiple_of(rs * GRID_W, GRID_W), nk), :]
        v = v_ref[0, 0, pl.ds(pl.multiple_of(rs * GRID_W, GRID_W), nk), :]
        outs = []
        for hh in range(2):
            keep = first_head if hh == 0 else jnp.logical_not(first_head)
            qm = jnp.where(keep, q, jnp.zeros_like(q))
            s = lax.dot_general(qm, k, (((1,), (1,)), ((), ())), preferred_element_type=_F32)
            s = s * Q_SCALE + bias_ref[hh, delta]
            s = jnp.where(col_ok, s, NEG_INF)
            m = jnp.max(s, axis=-1, keepdims=True)
            p = jnp.exp(s - m)
            l = jnp.sum(p, axis=-1, keepdims=True)
            pv = jnp.dot(p.astype(_BF16), v, preferred_element_type=_F32)
            outs.append(pv / l)
        o = jnp.where(first_head, outs[0], outs[1])
        o_ref[0, 0, pl.ds(pl.multiple_of(i * GRID_W, GRID_W), GRID_W), :] = o.astype(_BF16)
        return carry

    lax.fori_loop(0, GRID_ROWS, row, 0)


def _na_attention(qkv, bias_tbl):
    slab = (1, 1, SEQ, LANES)
    return pl.pallas_call(
        _na_kernel,
        grid=(BATCH, N_PAIRS),
        in_specs=[
            pl.BlockSpec(slab, lambda b, p: (b, p, 0, 0)),
            pl.BlockSpec(slab, lambda b, p: (b, N_PAIRS + p, 0, 0)),
            pl.BlockSpec(slab, lambda b, p: (b, 2 * N_PAIRS + p, 0, 0)),
            pl.BlockSpec((2, NA_ROWS, GRID_W, NA_ROWS * GRID_W), lambda b, p: (p, 0, 0, 0)),
        ],
        out_specs=pl.BlockSpec(slab, lambda b, p: (b, p, 0, 0)),
        out_shape=jax.ShapeDtypeStruct((BATCH, N_PAIRS, SEQ, LANES), _BF16),
        compiler_params=_params("parallel", "arbitrary"),
        name="na_attention",
    )(qkv, qkv, qkv, bias_tbl)


def _na_bias_table(rpb):
    col = jnp.arange(GRID_W)
    col_idx = jnp.clip(col[None, :] - col[:, None] + NA_COLS - 1, 0, 2 * NA_COLS - 2)
    rpb_cols = rpb.astype(_F32)[:, :, col_idx]
    per_delta = [rpb_cols[:, NA_ROWS - 1 - d: 2 * NA_ROWS - 1 - d] for d in range(NA_ROWS)]
    tbl = jnp.stack(per_delta, axis=1)
    tbl = jnp.transpose(tbl, (0, 1, 3, 2, 4))
    return tbl.reshape(N_HEADS, NA_ROWS, GRID_W, NA_ROWS * GRID_W)


def _oproj_kernel(o_ref, x_ref, w_ref, g_ref, out_ref):
    o = jnp.concatenate([o_ref[0, c] for c in range(N_PAIRS)], axis=-1)
    y = jnp.dot(o, w_ref[...], preferred_element_type=_F32)
    out_ref[...] = x_ref[...] + _rms(y, g_ref[...])


def _oproj(o_cbm, x2d, w_bf16, gain):
    tiles_per_seq = SEQ // TM
    return pl.pallas_call(
        _oproj_kernel,
        grid=(N_TOK // TM,),
        in_specs=[
            pl.BlockSpec((1, N_PAIRS, TM, LANES), lambda i: (i // tiles_per_seq, 0, i % tiles_per_seq, 0)),
            pl.BlockSpec((TM, D_MODEL), lambda i: (i, 0)),
            _resident((D_MODEL, D_MODEL)),
            _resident((1, D_MODEL)),
        ],
        out_specs=pl.BlockSpec((TM, D_MODEL), lambda i: (i, 0)),
        out_shape=jax.ShapeDtypeStruct((N_TOK, D_MODEL), _F32),
        compiler_params=_params("parallel"),
        name="oproj",
    )(o_cbm, x2d, w_bf16, gain)


def _ffn_kernel(x_ref, gpre_ref, wg_ref, wu_ref, wd_ref, gpost_ref, out_ref):
    x = x_ref[...]
    h = _rms(x, gpre_ref[...]).astype(_BF16)
    y = None
    for start, size in FF_SPLITS:
        g = jnp.dot(h, wg_ref[:, start:start + size], preferred_element_type=_F32)
        u = jnp.dot(h, wu_ref[:, start:start + size], preferred_element_type=_F32)
        a = ((g * jax.nn.sigmoid(g)) * u).astype(_BF16)
        part = jnp.dot(a, wd_ref[start:start + size, :], preferred_element_type=_F32)
        y = part if y is None else y + part
    out_ref[...] = x + _rms(y, gpost_ref[...])


def _ffn(x2d, gpre, wg, wu, wd, gpost):
    return pl.pallas_call(
        _ffn_kernel,
        grid=(N_TOK // TM,),
        in_specs=[
            pl.BlockSpec((TM, D_MODEL), lambda i: (i, 0)),
            _resident((1, D_MODEL)),
            _resident((D_MODEL, D_FF)),
            _resident((D_MODEL, D_FF)),
            _resident((D_FF, D_MODEL)),
            _resident((1, D_MODEL)),
        ],
        out_specs=pl.BlockSpec((TM, D_MODEL), lambda i: (i, 0)),
        out_shape=jax.ShapeDtypeStruct((N_TOK, D_MODEL), _F32),
        compiler_params=_params("parallel"),
        name="ffn",
    )(x2d, gpre, wg, wu, wd, gpost)


def _qkv_dil_kernel(x_ref, g_ref, w_ref, o_ref, h_scr, acc_scr):
    j = pl.program_id(1)
    tiles_per_group = 3 * D_MODEL // DIL_TN

    @pl.when(j == 0)
    def _():
        for t in range(SEQ // TM):
            rows = slice(t * TM, (t + 1) * TM)
            h_scr[rows, :] = _rms(x_ref[0, rows, :], g_ref[...]).astype(_BF16)

    y = jnp.dot(h_scr[...], w_ref[...], preferred_element_type=_F32)
    for c in range(DIL_TN // LANES):
        acc_scr[c] = y[:, c * LANES:(c + 1) * LANES]

    for grp, (_, dil) in enumerate(DIL_GROUPS):
        @pl.when(j // tiles_per_group == grp)
        def _(dil=dil):
            sub = SEQ // dil
            for r in range(dil):
                for c in range(DIL_TN // LANES):
                    if dil == 1:
                        val = acc_scr[c]
                    else:
                        val = acc_scr[c, pl.ds(r, sub, stride=dil), :]
                    o_ref[0, c, r * sub:(r + 1) * sub, :] = val.astype(_BF16)


def _qkv_dil(x3d, gain, w_bf16):
    n_col = N_GROUPS * 3 * D_MODEL
    blocks_per_tile = DIL_TN // LANES
    return pl.pallas_call(
        _qkv_dil_kernel,
        grid=(BATCH, n_col // DIL_TN),
        in_specs=[
            pl.BlockSpec((1, SEQ, D_MODEL), lambda b, j: (b, 0, 0)),
            _resident((1, D_MODEL)),
            pl.BlockSpec((D_MODEL, DIL_TN), lambda b, j: (0, j)),
        ],
        out_specs=pl.BlockSpec((1, blocks_per_tile, SEQ, LANES), lambda b, j: (b, j, 0, 0)),
        out_shape=jax.ShapeDtypeStruct((BATCH, n_col // LANES, SEQ, LANES), _BF16),
        scratch_shapes=[pltpu.VMEM((SEQ, D_MODEL), _BF16), pltpu.VMEM((DIL_TN // LANES, SEQ, LANES), _F32)],
        compiler_params=_params("parallel", "arbitrary"),
        name="qkv_dil",
    )(x3d, gain, w_bf16)


def _dil_kernel(slopes_ref, q0, k0, v0, q1, k1, v1, q2, k2, v2, o_ref, og_scr, lg_scr):
    pair = pl.program_id(1)
    qkv_refs = ((q0, k0, v0), (q1, k1, v1), (q2, k2, v2))
    lane = lax.broadcasted_iota(jnp.int32, (BAND_BLOCK, LANES), 1)
    first_head = lane < HEAD_DIM

    for grp, (window, dil) in enumerate(DIL_GROUPS):
        q_ref, k_ref, v_ref = qkv_refs[grp]
        radius = window // (2 * dil)
        sub = SEQ // dil
        blocks_per_sub = sub // BAND_BLOCK
        width = min(2 * BAND_BLOCK, sub)
        rel = (lax.broadcasted_iota(jnp.int32, (BAND_BLOCK, width), 0)
               - lax.broadcasted_iota(jnp.int32, (BAND_BLOCK, width), 1))

        def block(t, carry, q_ref=q_ref, k_ref=k_ref, v_ref=v_ref, dil=dil, radius=radius, sub=sub,
                  blocks_per_sub=blocks_per_sub, width=width, rel=rel, grp=grp):
            res = t // blocks_per_sub
            c = t % blocks_per_sub
            kstart = jnp.clip(c * BAND_BLOCK - radius, 0, sub - width)
            q = q_ref[0, 0, pl.ds(pl.multiple_of(t * BAND_BLOCK, BAND_BLOCK), BAND_BLOCK), :]
            krow = pl.multiple_of(res * sub + kstart, radius)
            k = k_ref[0, 0, pl.ds(krow, width), :]
            v = v_ref[0, 0, pl.ds(krow, width), :]
            dist = jnp.abs(rel + (c * BAND_BLOCK - kstart))
            valid = dist <= radius
            distf = dist.astype(_F32)
            outs, lses = [], []
            for hh in range(2):
                keep = first_head if hh == 0 else jnp.logical_not(first_head)
                qm = jnp.where(keep, q, jnp.zeros_like(q))
                s = lax.dot_general(qm, k, (((1,), (1,)), ((), ())), preferred_element_type=_F32)
                s = s * Q_SCALE - (slopes_ref[2 * pair + hh] * dil) * distf
                s = jnp.where(valid, s, NEG_INF)
                m = jnp.max(s, axis=-1, keepdims=True)
                p = jnp.exp(s - m)
                l = jnp.sum(p, axis=-1, keepdims=True)
                pv = jnp.dot(p.astype(_BF16), v, preferred_element_type=_F32)
                outs.append(pv / l)
                lses.append(jnp.broadcast_to(m + jnp.log(l), (BAND_BLOCK, LANES)))
            o = jnp.where(first_head, outs[0], outs[1])
            lse = jnp.where(first_head, lses[0], lses[1])
            if dil == 1:
                rows = pl.ds(pl.multiple_of(t * BAND_BLOCK, BAND_BLOCK), BAND_BLOCK)
            else:
                rows = pl.ds(c * (BAND_BLOCK * dil) + res, BAND_BLOCK, stride=dil)
            og_scr[grp, rows, :] = o
            lg_scr[grp, rows, :] = lse
            return carry

        lax.fori_loop(0, SEQ // BAND_BLOCK, block, 0)

    chunk = 256
    for t in range(SEQ // chunk):
        rows = slice(t * chunk, (t + 1) * chunk)
        lse = [lg_scr[grp, rows, :] for grp in range(N_GROUPS)]
        mx = jnp.maximum(jnp.maximum(lse[0], lse[1]), lse[2])
        e = [jnp.exp(x - mx) for x in lse]
        den = e[0] + e[1] + e[2]
        num = e[0] * og_scr[0, rows, :] + e[1] * og_scr[1, rows, :] + e[2] * og_scr[2, rows, :]
        o_ref[0, 0, rows, :] = (num / den).astype(_BF16)


def _dil_attention(qkv, slopes):
    slab = (1, 1, SEQ, LANES)

    def spec(grp, which):
        base = (grp * 3 + which) * N_PAIRS
        return pl.BlockSpec(slab, lambda b, p: (b, base + p, 0, 0))

    return pl.pallas_call(
        _dil_kernel,
        grid=(BATCH, N_PAIRS),
        in_specs=[pl.BlockSpec(memory_space=pltpu.SMEM)]
        + [spec(grp, which) for grp in range(N_GROUPS) for which in range(3)],
        out_specs=pl.BlockSpec(slab, lambda b, p: (b, p, 0, 0)),
        out_shape=jax.ShapeDtypeStruct((BATCH, N_PAIRS, SEQ, LANES), _BF16),
        scratch_shapes=[pltpu.VMEM((N_GROUPS, SEQ, LANES), _F32), pltpu.VMEM((N_GROUPS, SEQ, LANES), _F32)],
        compiler_params=_params("parallel", "arbitrary"),
        name="dil_attention",
    )(slopes, *([qkv] * 9))


def kernel(x, norm_mix_pre, norm_mix_post, norm_ffn_pre, norm_ffn_post, na_w_qkv, na_w_o, na_rpb, dil_w_qkv, dil_w_o,
           ffn_w_gate, ffn_w_up, ffn_w_down):
    x2d = x.reshape(N_TOK, D_MODEL)
    slopes = 2.0 ** (-8.0 * jnp.arange(1, N_HEADS + 1, dtype=_F32) / N_HEADS)

    def gain(g, layer):
        return g[layer].reshape(1, D_MODEL)

    for layer in range(DEPTH):
        j = layer // 2
        if layer % 2 == 0:
            qkv = _qkv_na(x2d, gain(norm_mix_pre, layer), na_w_qkv[j].astype(_BF16))
            o = _na_attention(qkv, _na_bias_table(na_rpb[j]))
            w_o = na_w_o[j]
        else:
            qkv = _qkv_dil(x2d.reshape(BATCH, SEQ, D_MODEL), gain(norm_mix_pre, layer), dil_w_qkv[j].astype(_BF16))
            o = _dil_attention(qkv, slopes)
            w_o = dil_w_o[j]
        x2d = _oproj(o, x2d, w_o.astype(_BF16), gain(norm_mix_post, layer))
        x2d = _ffn(x2d, gain(norm_ffn_pre, layer), ffn_w_gate[layer].astype(_BF16), ffn_w_up[layer].astype(_BF16),
                   ffn_w_down[layer].astype(_BF16), gain(norm_ffn_post, layer))
    return x2d.reshape(BATCH, SEQ, D_MODEL)
```

```python
import functools

import jax
import jax.numpy as jnp
from jax import lax
from jax.experimental import pallas as pl
from jax.experimental.pallas import tpu as pltpu

D_MODEL = 1024
BATCH = 8
SEQ = 2048
DEPTH = 2
N_HEADS = 16
HEAD_DIM = 64
GRID_W = 64
GRID_ROWS = SEQ // GRID_W
NA_ROWS = 8
NA_COLS = 16
DIL_GROUPS = ((128, 1), (512, 4), (2048, 16))
N_GROUPS = len(DIL_GROUPS)
BAND_BLOCK = 128
D_FF = 2816
RMS_EPS = 1e-6
NEG_INF = -1e30
Q_SCALE = HEAD_DIM ** -0.5

LANES = 128
N_PAIRS = D_MODEL // LANES
N_TOK = BATCH * SEQ
TM = 512
FF_SPLITS = ((0, 1536), (1536, 1280))
DIL_TN = 512
NA_ROWS_PER_ITER = 4
DIL_BLOCKS_PER_ITER = 4
DIL_WINDOW_OFFSETS = (0, BAND_BLOCK // 2, BAND_BLOCK)
VMEM_LIMIT = 56 * 1024 * 1024

_BF16 = jnp.bfloat16
_F32 = jnp.float32


def _rms(x, g):
    ms = jnp.mean(x * x, axis=-1, keepdims=True)
    return (x * lax.rsqrt(ms + RMS_EPS)) * g


def _resident(shape):
    return pl.BlockSpec(shape, lambda *_: (0,) * len(shape), pipeline_mode=pl.Buffered(1))


def _params(*sem):
    return pltpu.CompilerParams(dimension_semantics=sem, vmem_limit_bytes=VMEM_LIMIT)


def _qkv_na_kernel(x_ref, g_ref, w_ref, o_ref):
    h = _rms(x_ref[...], g_ref[...]).astype(_BF16)
    for j in range(3):
        y = jnp.dot(h, w_ref[:, j * D_MODEL:(j + 1) * D_MODEL], preferred_element_type=_F32)
        for c in range(N_PAIRS):
            o_ref[0, j * N_PAIRS + c] = y[:, c * LANES:(c + 1) * LANES].astype(_BF16)


def _qkv_na(x2d, gain, w_bf16):
    tiles_per_seq = SEQ // TM
    return pl.pallas_call(
        _qkv_na_kernel,
        grid=(N_TOK // TM,),
        in_specs=[
            pl.BlockSpec((TM, D_MODEL), lambda i: (i, 0)),
            _resident((1, D_MODEL)),
            _resident((D_MODEL, 3 * D_MODEL)),
        ],
        out_specs=pl.BlockSpec((1, 3 * N_PAIRS, TM, LANES),
                               lambda i: (i // tiles_per_seq, 0, i % tiles_per_seq, 0)),
        out_shape=jax.ShapeDtypeStruct((BATCH, 3 * N_PAIRS, SEQ, LANES), _BF16),
        compiler_params=_params("parallel"),
        name="qkv_na",
    )(x2d, gain, w_bf16)


def _softmax_pv(s, v):
    m = jnp.max(s, axis=-1, keepdims=True)
    p = jnp.exp(s - m)
    l = jnp.sum(p, axis=-1, keepdims=True)
    pv = jnp.dot(p.astype(_BF16), v, preferred_element_type=_F32)
    return pv * (1.0 / l), m, l


def _na_kernel(q_ref, k_ref, v_ref, bias_ref, o_ref, cap_scr, s_scr):
    kh = NA_ROWS
    nk = kh * GRID_W
    lane = lax.broadcasted_iota(jnp.int32, (GRID_W, LANES), 1)
    first_head = lane < HEAD_DIM

    qcol = lax.broadcasted_iota(jnp.int32, (GRID_W, nk), 0)
    kcol = lax.broadcasted_iota(jnp.int32, (GRID_W, nk), 1) % GRID_W
    col_start = jnp.clip(qcol - NA_COLS // 2, 0, GRID_W - NA_COLS)
    col_ok = (kcol >= col_start) & (kcol < col_start + NA_COLS)
    cap_scr[...] = jnp.where(col_ok, jnp.finfo(_F32).max, NEG_INF).astype(_F32)

    def rows(it, carry):
        units = []
        for u in range(NA_ROWS_PER_ITER):
            i = it * NA_ROWS_PER_ITER + u
            rs = jnp.clip(i - kh // 2, 0, GRID_ROWS - kh)
            q = q_ref[0, 0, pl.ds(pl.multiple_of(i * GRID_W, GRID_W), GRID_W), :] * Q_SCALE
            k = k_ref[0, 0, pl.ds(pl.multiple_of(rs * GRID_W, GRID_W), nk), :]
            for hh in range(2):
                keep = first_head if hh == 0 else jnp.logical_not(first_head)
                qm = jnp.where(keep, q, jnp.zeros_like(q))
                s_scr[2 * u + hh] = lax.dot_general(qm, k, (((1,), (1,)), ((), ())), preferred_element_type=_F32)
            units.append((u, i, rs))
        for u, i, rs in units:
            v = v_ref[0, 0, pl.ds(pl.multiple_of(rs * GRID_W, GRID_W), nk), :]
            outs = []
            for hh in range(2):
                s = jnp.minimum(s_scr[2 * u + hh] + bias_ref[hh, i - rs], cap_scr[...])
                outs.append(_softmax_pv(s, v)[0])
            o = jnp.where(first_head, outs[0], outs[1])
            o_ref[0, 0, pl.ds(pl.multiple_of(i * GRID_W, GRID_W), GRID_W), :] = o.astype(_BF16)
        return carry

    lax.fori_loop(0, GRID_ROWS // NA_ROWS_PER_ITER, rows, 0)


def _na_attention(qkv, bias_tbl):
    slab = (1, 1, SEQ, LANES)
    return pl.pallas_call(
        _na_kernel,
        grid=(N_PAIRS, BATCH),
        in_specs=[
            pl.BlockSpec(slab, lambda p, b: (b, p, 0, 0)),
            pl.BlockSpec(slab, lambda p, b: (b, N_PAIRS + p, 0, 0)),
            pl.BlockSpec(slab, lambda p, b: (b, 2 * N_PAIRS + p, 0, 0)),
            pl.BlockSpec((2, NA_ROWS, GRID_W, NA_ROWS * GRID_W), lambda p, b: (p, 0, 0, 0)),
        ],
        out_specs=pl.BlockSpec(slab, lambda p, b: (b, p, 0, 0)),
        out_shape=jax.ShapeDtypeStruct((BATCH, N_PAIRS, SEQ, LANES), _BF16),
        scratch_shapes=[pltpu.VMEM((GRID_W, NA_ROWS * GRID_W), _F32),
                        pltpu.VMEM((2 * NA_ROWS_PER_ITER, GRID_W, NA_ROWS * GRID_W), _F32)],
        compiler_params=_params("arbitrary", "arbitrary"),
        name="na_attention",
    )(qkv, qkv, qkv, bias_tbl)


def _na_bias_table(rpb):
    col = jnp.arange(GRID_W)
    col_idx = jnp.clip(col[None, :] - col[:, None] + NA_COLS - 1, 0, 2 * NA_COLS - 2)
    rpb_cols = rpb.astype(_F32)[:, :, col_idx]
    per_delta = [rpb_cols[:, NA_ROWS - 1 - d: 2 * NA_ROWS - 1 - d] for d in range(NA_ROWS)]
    tbl = jnp.stack(per_delta, axis=1)
    tbl = jnp.transpose(tbl, (0, 1, 3, 2, 4))
    return tbl.reshape(N_HEADS, NA_ROWS, GRID_W, NA_ROWS * GRID_W)


def _oproj_kernel(o_ref, x_ref, w_ref, g_ref, out_ref):
    o = jnp.concatenate([o_ref[0, c] for c in range(N_PAIRS)], axis=-1)
    y = jnp.dot(o, w_ref[...], preferred_element_type=_F32)
    out_ref[...] = x_ref[...] + _rms(y, g_ref[...])


def _oproj(o_cbm, x2d, w_bf16, gain):
    tiles_per_seq = SEQ // TM
    return pl.pallas_call(
        _oproj_kernel,
        grid=(N_TOK // TM,),
        in_specs=[
            pl.BlockSpec((1, N_PAIRS, TM, LANES), lambda i: (i // tiles_per_seq, 0, i % tiles_per_seq, 0)),
            pl.BlockSpec((TM, D_MODEL), lambda i: (i, 0)),
            _resident((D_MODEL, D_MODEL)),
            _resident((1, D_MODEL)),
        ],
        out_specs=pl.BlockSpec((TM, D_MODEL), lambda i: (i, 0)),
        out_shape=jax.ShapeDtypeStruct((N_TOK, D_MODEL), _F32),
        compiler_params=_params("parallel"),
        name="oproj",
    )(o_cbm, x2d, w_bf16, gain)


def _ffn_kernel(x_ref, gpre_ref, wg_ref, wu_ref, wd_ref, gpost_ref, out_ref):
    x = x_ref[...]
    h = _rms(x, gpre_ref[...]).astype(_BF16)
    y = None
    for start, size in FF_SPLITS:
        g = jnp.dot(h, wg_ref[:, start:start + size], preferred_element_type=_F32)
        u = jnp.dot(h, wu_ref[:, start:start + size], preferred_element_type=_F32)
        a = ((g * jax.nn.sigmoid(g)) * u).astype(_BF16)
        part = jnp.dot(a, wd_ref[start:start + size, :], preferred_element_type=_F32)
        y = part if y is None else y + part
    out_ref[...] = x + _rms(y, gpost_ref[...])


def _ffn(x2d, gpre, wg, wu, wd, gpost):
    return pl.pallas_call(
        _ffn_kernel,
        grid=(N_TOK // TM,),
        in_specs=[
            pl.BlockSpec((TM, D_MODEL), lambda i: (i, 0)),
            _resident((1, D_MODEL)),
            _resident((D_MODEL, D_FF)),
            _resident((D_MODEL, D_FF)),
            _resident((D_FF, D_MODEL)),
            _resident((1, D_MODEL)),
        ],
        out_specs=pl.BlockSpec((TM, D_MODEL), lambda i: (i, 0)),
        out_shape=jax.ShapeDtypeStruct((N_TOK, D_MODEL), _F32),
        compiler_params=_params("parallel"),
        name="ffn",
    )(x2d, gpre, wg, wu, wd, gpost)


def _qkv_dil_kernel(x_ref, g_ref, w_ref, o_ref, h_scr, acc_scr):
    j = pl.program_id(1)
    tiles_per_group = 3 * D_MODEL // DIL_TN

    @pl.when(j == 0)
    def _():
        for t in range(SEQ // TM):
            rows = slice(t * TM, (t + 1) * TM)
            h_scr[rows, :] = _rms(x_ref[0, rows, :], g_ref[...]).astype(_BF16)

    y = jnp.dot(h_scr[...], w_ref[...], preferred_element_type=_F32)
    for c in range(DIL_TN // LANES):
        acc_scr[c] = y[:, c * LANES:(c + 1) * LANES]

    for grp, (_, dil) in enumerate(DIL_GROUPS):
        @pl.when(j // tiles_per_group == grp)
        def _(dil=dil):
            sub = SEQ // dil
            for r in range(dil):
                for c in range(DIL_TN // LANES):
                    if dil == 1:
                        val = acc_scr[c]
                    else:
                        val = acc_scr[c, pl.ds(r, sub, stride=dil), :]
                    o_ref[0, c, r * sub:(r + 1) * sub, :] = val.astype(_BF16)


def _qkv_dil(x3d, gain, w_bf16):
    n_col = N_GROUPS * 3 * D_MODEL
    blocks_per_tile = DIL_TN // LANES
    return pl.pallas_call(
        _qkv_dil_kernel,
        grid=(BATCH, n_col // DIL_TN),
        in_specs=[
            pl.BlockSpec((1, SEQ, D_MODEL), lambda b, j: (b, 0, 0)),
            _resident((1, D_MODEL)),
            pl.BlockSpec((D_MODEL, DIL_TN), lambda b, j: (0, j)),
        ],
        out_specs=pl.BlockSpec((1, blocks_per_tile, SEQ, LANES), lambda b, j: (b, j, 0, 0)),
        out_shape=jax.ShapeDtypeStruct((BATCH, n_col // LANES, SEQ, LANES), _BF16),
        scratch_shapes=[pltpu.VMEM((SEQ, D_MODEL), _BF16), pltpu.VMEM((DIL_TN // LANES, SEQ, LANES), _F32)],
        compiler_params=_params("parallel", "arbitrary"),
        name="qkv_dil",
    )(x3d, gain, w_bf16)


def _dil_kernel(slopes_ref, q0, k0, v0, q1, k1, v1, q2, k2, v2, o_ref, og_scr, lg_scr, tab_scr, s_scr):
    pair = pl.program_id(0)
    qkv_refs = ((q0, k0, v0), (q1, k1, v1), (q2, k2, v2))
    lane = lax.broadcasted_iota(jnp.int32, (BAND_BLOCK, LANES), 1)
    first_head = lane < HEAD_DIM
    max_width = 2 * BAND_BLOCK

    @pl.when(pl.program_id(1) == 0)
    def _():
        rel = (lax.broadcasted_iota(jnp.int32, (BAND_BLOCK, max_width), 0)
               - lax.broadcasted_iota(jnp.int32, (BAND_BLOCK, max_width), 1))
        for grp, (window, dil) in enumerate(DIL_GROUPS):
            radius = window // (2 * dil)
            for var, off in enumerate(DIL_WINDOW_OFFSETS):
                dist = jnp.abs(rel + off)
                distf = dist.astype(_F32)
                for hh in range(2):
                    tab_scr[grp, var, hh] = -((slopes_ref[2 * pair + hh] * dil) * distf)
                tab_scr[grp, var, 2] = jnp.where(dist <= radius, jnp.finfo(_F32).max, NEG_INF).astype(_F32)

    for grp, (window, dil) in enumerate(DIL_GROUPS):
        q_ref, k_ref, v_ref = qkv_refs[grp]
        radius = window // (2 * dil)
        sub = SEQ // dil
        blocks_per_sub = sub // BAND_BLOCK
        width = min(max_width, sub)

        def blocks(it, carry, q_ref=q_ref, k_ref=k_ref, v_ref=v_ref, dil=dil, radius=radius, sub=sub,
                   blocks_per_sub=blocks_per_sub, width=width, grp=grp):
            units = []
            for u in range(DIL_BLOCKS_PER_ITER):
                t = it * DIL_BLOCKS_PER_ITER + u
                res = t // blocks_per_sub
                c = t % blocks_per_sub
                kstart = jnp.clip(c * BAND_BLOCK - radius, 0, sub - width)
                krow = pl.multiple_of(res * sub + kstart, radius)
                q = q_ref[0, 0, pl.ds(pl.multiple_of(t * BAND_BLOCK, BAND_BLOCK), BAND_BLOCK), :] * Q_SCALE
                k = k_ref[0, 0, pl.ds(krow, width), :]
                for hh in range(2):
                    keep = first_head if hh == 0 else jnp.logical_not(first_head)
                    qm = jnp.where(keep, q, jnp.zeros_like(q))
                    s_scr[2 * u + hh, :, :width] = lax.dot_general(
                        qm, k, (((1,), (1,)), ((), ())), preferred_element_type=_F32)
                units.append((u, t, res, c, krow))
            for u, t, res, c, krow in units:
                v = v_ref[0, 0, pl.ds(krow, width), :]
                if blocks_per_sub == 1:
                    var = 0
                else:
                    var = (c > 0).astype(jnp.int32) + (c == blocks_per_sub - 1).astype(jnp.int32)
                outs, lses = [], []
                for hh in range(2):
                    s = jnp.minimum(s_scr[2 * u + hh, :, :width] + tab_scr[grp, var, hh, :, :width],
                                    tab_scr[grp, var, 2, :, :width])
                    o_h, m, l = _softmax_pv(s, v)
                    outs.append(o_h)
                    lses.append(jnp.broadcast_to(m + jnp.log(l), (BAND_BLOCK, LANES)))
                o = jnp.where(first_head, outs[0], outs[1])
                lse = jnp.where(first_head, lses[0], lses[1])
                if dil == 1:
                    rows = pl.ds(pl.multiple_of(t * BAND_BLOCK, BAND_BLOCK), BAND_BLOCK)
                else:
                    rows = pl.ds(c * (BAND_BLOCK * dil) + res, BAND_BLOCK, stride=dil)
                og_scr[grp, rows, :] = o
                lg_scr[grp, rows, :] = lse
            return carry

        lax.fori_loop(0, SEQ // (BAND_BLOCK * DIL_BLOCKS_PER_ITER), blocks, 0)

    chunk = 256
    for t in range(SEQ // chunk):
        rows = slice(t * chunk, (t + 1) * chunk)
        lse = [lg_scr[grp, rows, :] for grp in range(N_GROUPS)]
        mx = jnp.maximum(jnp.maximum(lse[0], lse[1]), lse[2])
        e = [jnp.exp(x - mx) for x in lse]
        den = e[0] + e[1] + e[2]
        num = e[0] * og_scr[0, rows, :] + e[1] * og_scr[1, rows, :] + e[2] * og_scr[2, rows, :]
        o_ref[0, 0, rows, :] = (num / den).astype(_BF16)


def _dil_attention(qkv, slopes):
    slab = (1, 1, SEQ, LANES)

    def spec(grp, which):
        base = (grp * 3 + which) * N_PAIRS
        return pl.BlockSpec(slab, lambda p, b: (b, base + p, 0, 0))

    return pl.pallas_call(
        _dil_kernel,
        grid=(N_PAIRS, BATCH),
        in_specs=[pl.BlockSpec(memory_space=pltpu.SMEM)]
        + [spec(grp, which) for grp in range(N_GROUPS) for which in range(3)],
        out_specs=pl.BlockSpec(slab, lambda p, b: (b, p, 0, 0)),
        out_shape=jax.ShapeDtypeStruct((BATCH, N_PAIRS, SEQ, LANES), _BF16),
        scratch_shapes=[
            pltpu.VMEM((N_GROUPS, SEQ, LANES), _F32),
            pltpu.VMEM((N_GROUPS, SEQ, LANES), _F32),
            pltpu.VMEM((N_GROUPS, len(DIL_WINDOW_OFFSETS), 3, BAND_BLOCK, 2 * BAND_BLOCK), _F32),
            pltpu.VMEM((2 * DIL_BLOCKS_PER_ITER, BAND_BLOCK, 2 * BAND_BLOCK), _F32),
        ],
        compiler_params=_params("arbitrary", "arbitrary"),
        name="dil_attention",
    )(slopes, *([qkv] * 9))


def kernel(x, norm_mix_pre, norm_mix_post, norm_ffn_pre, norm_ffn_post, na_w_qkv, na_w_o, na_rpb, dil_w_qkv, dil_w_o,
           ffn_w_gate, ffn_w_up, ffn_w_down):
    x2d = x.reshape(N_TOK, D_MODEL)
    slopes = 2.0 ** (-8.0 * jnp.arange(1, N_HEADS + 1, dtype=_F32) / N_HEADS)

    def gain(g, layer):
        return g[layer].reshape(1, D_MODEL)

    for layer in range(DEPTH):
        j = layer // 2
        if layer % 2 == 0:
            qkv = _qkv_na(x2d, gain(norm_mix_pre, layer), na_w_qkv[j].astype(_BF16))
            o = _na_attention(qkv, _na_bias_table(na_rpb[j]))
            w_o = na_w_o[j]
        else:
            qkv = _qkv_dil(x2d.reshape(BATCH, SEQ, D_MODEL), gain(norm_mix_pre, layer), dil_w_qkv[j].astype(_BF16))
            o = _dil_attention(qkv, slopes)
            w_o = dil_w_o[j]
        x2d = _oproj(o, x2d, w_o.astype(_BF16), gain(norm_mix_post, layer))
        x2d = _ffn(x2d, gain(norm_ffn_pre, layer), ffn_w_gate[layer].astype(_BF16), ffn_w_up[layer].astype(_BF16),
                   ffn_w_down[layer].astype(_BF16), gain(norm_ffn_post, layer))
    return x2d.reshape(BATCH, SEQ, D_MODEL)
```

```python
import functools

import jax
import jax.numpy as jnp
from jax import lax
from jax.experimental import pallas as pl
from jax.experimental.pallas import tpu as pltpu

D_MODEL = 1024
BATCH = 8
SEQ = 2048
DEPTH = 2
N_HEADS = 16
HEAD_DIM = 64
GRID_W = 64
GRID_ROWS = SEQ // GRID_W
NA_ROWS = 8
NA_COLS = 16
DIL_GROUPS = ((128, 1), (512, 4), (2048, 16))
N_GROUPS = len(DIL_GROUPS)
BAND_BLOCK = 128
D_FF = 2816
RMS_EPS = 1e-6
NEG_INF = -1e30
Q_SCALE = HEAD_DIM ** -0.5

LANES = 128
N_PAIRS = D_MODEL // LANES
N_TOK = BATCH * SEQ
TM = 512
FF_SPLITS = ((0, 1536), (1536, 1280))
DIL_TN = 512
NA_ROWS_PER_ITER = 4
DIL_BLOCKS_PER_ITER = 4
DIL_WINDOW_OFFSETS = (0, BAND_BLOCK // 2, BAND_BLOCK)
VMEM_LIMIT = 56 * 1024 * 1024

_BF16 = jnp.bfloat16
_F32 = jnp.float32


def _rms(x, g):
    ms = jnp.mean(x * x, axis=-1, keepdims=True)
    return (x * lax.rsqrt(ms + RMS_EPS)) * g


def _resident(shape):
    return pl.BlockSpec(shape, lambda *_: (0,) * len(shape), pipeline_mode=pl.Buffered(1))


def _params(*sem):
    return pltpu.CompilerParams(dimension_semantics=sem, vmem_limit_bytes=VMEM_LIMIT)


def _qkv_na_kernel(x_ref, g_ref, w_ref, o_ref):
    h = _rms(x_ref[...], g_ref[...]).astype(_BF16)
    for j in range(3):
        y = jnp.dot(h, w_ref[:, j * D_MODEL:(j + 1) * D_MODEL], preferred_element_type=_F32)
        for c in range(N_PAIRS):
            o_ref[0, j * N_PAIRS + c] = y[:, c * LANES:(c + 1) * LANES].astype(_BF16)


def _qkv_na(x2d, gain, w_bf16):
    tiles_per_seq = SEQ // TM
    return pl.pallas_call(
        _qkv_na_kernel,
        grid=(N_TOK // TM,),
        in_specs=[
            pl.BlockSpec((TM, D_MODEL), lambda i: (i, 0)),
            _resident((1, D_MODEL)),
            _resident((D_MODEL, 3 * D_MODEL)),
        ],
        out_specs=pl.BlockSpec((1, 3 * N_PAIRS, TM, LANES),
                               lambda i: (i // tiles_per_seq, 0, i % tiles_per_seq, 0)),
        out_shape=jax.ShapeDtypeStruct((BATCH, 3 * N_PAIRS, SEQ, LANES), _BF16),
        compiler_params=_params("parallel"),
        name="qkv_na",
    )(x2d, gain, w_bf16)


def _softmax_pv(s, v):
    m = jnp.max(s, axis=-1, keepdims=True)
    p = jnp.exp(s - m)
    l = jnp.sum(p, axis=-1, keepdims=True)
    pv = jnp.dot(p.astype(_BF16), v, preferred_element_type=_F32)
    return pv * (1.0 / l), m, l


def _na_kernel(q_ref, k_ref, v_ref, bias_ref, o_ref, cap_scr, s_scr):
    kh = NA_ROWS
    nk = kh * GRID_W
    lane = lax.broadcasted_iota(jnp.int32, (GRID_W, LANES), 1)
    first_head = lane < HEAD_DIM

    qcol = lax.broadcasted_iota(jnp.int32, (GRID_W, nk), 0)
    kcol = lax.broadcasted_iota(jnp.int32, (GRID_W, nk), 1) % GRID_W
    col_start = jnp.clip(qcol - NA_COLS // 2, 0, GRID_W - NA_COLS)
    col_ok = (kcol >= col_start) & (kcol < col_start + NA_COLS)
    cap_scr[...] = jnp.where(col_ok, jnp.finfo(_F32).max, NEG_INF).astype(_F32)

    def rows(it, carry):
        units = []
        for u in range(NA_ROWS_PER_ITER):
            i = it * NA_ROWS_PER_ITER + u
            rs = jnp.clip(i - kh // 2, 0, GRID_ROWS - kh)
            q = q_ref[0, 0, pl.ds(pl.multiple_of(i * GRID_W, GRID_W), GRID_W), :] * Q_SCALE
            k = k_ref[0, 0, pl.ds(pl.multiple_of(rs * GRID_W, GRID_W), nk), :]
            for hh in range(2):
                keep = first_head if hh == 0 else jnp.logical_not(first_head)
                qm = jnp.where(keep, q, jnp.zeros_like(q))
                s_scr[2 * u + hh] = lax.dot_general(qm, k, (((1,), (1,)), ((), ())), preferred_element_type=_F32)
            units.append((u, i, rs))
        for u, i, rs in units:
            v = v_ref[0, 0, pl.ds(pl.multiple_of(rs * GRID_W, GRID_W), nk), :]
            outs = []
            for hh in range(2):
                s = jnp.minimum(s_scr[2 * u + hh] + bias_ref[hh, i - rs], cap_scr[...])
                outs.append(_softmax_pv(s, v)[0])
            o = jnp.where(first_head, outs[0], outs[1])
            o_ref[0, 0, pl.ds(pl.multiple_of(i * GRID_W, GRID_W), GRID_W), :] = o.astype(_BF16)
        return carry

    lax.fori_loop(0, GRID_ROWS // NA_ROWS_PER_ITER, rows, 0)


def _na_attention(qkv, bias_tbl):
    slab = (1, 1, SEQ, LANES)
    return pl.pallas_call(
        _na_kernel,
        grid=(N_PAIRS, BATCH),
        in_specs=[
            pl.BlockSpec(slab, lambda p, b: (b, p, 0, 0)),
            pl.BlockSpec(slab, lambda p, b: (b, N_PAIRS + p, 0, 0)),
            pl.BlockSpec(slab, lambda p, b: (b, 2 * N_PAIRS + p, 0, 0)),
            pl.BlockSpec((2, NA_ROWS, GRID_W, NA_ROWS * GRID_W), lambda p, b: (p, 0, 0, 0)),
        ],
        out_specs=pl.BlockSpec(slab, lambda p, b: (b, p, 0, 0)),
        out_shape=jax.ShapeDtypeStruct((BATCH, N_PAIRS, SEQ, LANES), _BF16),
        scratch_shapes=[pltpu.VMEM((GRID_W, NA_ROWS * GRID_W), _F32),
                        pltpu.VMEM((2 * NA_ROWS_PER_ITER, GRID_W, NA_ROWS * GRID_W), _F32)],
        compiler_params=_params("arbitrary", "arbitrary"),
        name="na_attention",
    )(qkv, qkv, qkv, bias_tbl)


def _na_bias_table(rpb):
    delta = jnp.arange(NA_ROWS)[:, None, None]
    qcol = jnp.arange(GRID_W)[None, :, None]
    key = jnp.arange(NA_ROWS * GRID_W)[None, None, :]
    row_idx = NA_ROWS - 1 - delta + key // GRID_W
    col_idx = jnp.clip(key % GRID_W - qcol + NA_COLS - 1, 0, 2 * NA_COLS - 2)
    return rpb.astype(_F32)[:, row_idx, col_idx]


def _mix_ffn_kernel(o_ref, x_ref, wo_ref, gmix_ref, gpre_ref, wg_ref, wu_ref, wd_ref, gpost_ref, out_ref):
    o = jnp.concatenate([o_ref[0, c] for c in range(N_PAIRS)], axis=-1)
    y = jnp.dot(o, wo_ref[...], preferred_element_type=_F32)
    x = x_ref[...] + _rms(y, gmix_ref[...])
    h = _rms(x, gpre_ref[...]).astype(_BF16)
    y = None
    for start, size in FF_SPLITS:
        g = jnp.dot(h, wg_ref[:, start:start + size], preferred_element_type=_F32)
        u = jnp.dot(h, wu_ref[:, start:start + size], preferred_element_type=_F32)
        a = ((g * jax.nn.sigmoid(g)) * u).astype(_BF16)
        part = jnp.dot(a, wd_ref[start:start + size, :], preferred_element_type=_F32)
        y = part if y is None else y + part
    out_ref[...] = x + _rms(y, gpost_ref[...])


def _mix_ffn(o_cbm, x2d, wo, gmix, gpre, wg, wu, wd, gpost):
    tiles_per_seq = SEQ // TM
    return pl.pallas_call(
        _mix_ffn_kernel,
        grid=(N_TOK // TM,),
        in_specs=[
            pl.BlockSpec((1, N_PAIRS, TM, LANES), lambda i: (i // tiles_per_seq, 0, i % tiles_per_seq, 0)),
            pl.BlockSpec((TM, D_MODEL), lambda i: (i, 0)),
            _resident((D_MODEL, D_MODEL)),
            _resident((1, D_MODEL)),
            _resident((1, D_MODEL)),
            _resident((D_MODEL, D_FF)),
            _resident((D_MODEL, D_FF)),
            _resident((D_FF, D_MODEL)),
            _resident((1, D_MODEL)),
        ],
        out_specs=pl.BlockSpec((TM, D_MODEL), lambda i: (i, 0)),
        out_shape=jax.ShapeDtypeStruct((N_TOK, D_MODEL), _F32),
        compiler_params=_params("parallel"),
        name="mix_ffn",
    )(o_cbm, x2d, wo, gmix, gpre, wg, wu, wd, gpost)


def _qkv_dil_kernel(x_ref, g_ref, w_ref, o_ref, h32_scr, hperm_scr):
    j = pl.program_id(1)
    tiles_per_group = 3 * D_MODEL // DIL_TN

    @pl.when(j == 0)
    def _():
        for t in range(SEQ // TM):
            rows = slice(t * TM, (t + 1) * TM)
            h = _rms(x_ref[0, rows, :], g_ref[...])
            for c in range(N_PAIRS):
                h32_scr[c, rows, :] = h[:, c * LANES:(c + 1) * LANES]

    for grp, (_, dil) in enumerate(DIL_GROUPS):
        @pl.when(j == grp * tiles_per_group)
        def _(dil=dil):
            sub = SEQ // dil
            for r in range(dil):
                for c in range(N_PAIRS):
                    if dil == 1:
                        val = h32_scr[c]
                    else:
                        val = h32_scr[c, pl.ds(r, sub, stride=dil), :]
                    hperm_scr[r * sub:(r + 1) * sub, c * LANES:(c + 1) * LANES] = val.astype(_BF16)

    y = jnp.dot(hperm_scr[...], w_ref[...], preferred_element_type=_F32)
    for c in range(DIL_TN // LANES):
        o_ref[0, c] = y[:, c * LANES:(c + 1) * LANES].astype(_BF16)


def _qkv_dil(x3d, gain, w_bf16):
    n_col = N_GROUPS * 3 * D_MODEL
    blocks_per_tile = DIL_TN // LANES
    return pl.pallas_call(
        _qkv_dil_kernel,
        grid=(BATCH, n_col // DIL_TN),
        in_specs=[
            pl.BlockSpec((1, SEQ, D_MODEL), lambda b, j: (b, 0, 0)),
            _resident((1, D_MODEL)),
            pl.BlockSpec((D_MODEL, DIL_TN), lambda b, j: (0, j)),
        ],
        out_specs=pl.BlockSpec((1, blocks_per_tile, SEQ, LANES), lambda b, j: (b, j, 0, 0)),
        out_shape=jax.ShapeDtypeStruct((BATCH, n_col // LANES, SEQ, LANES), _BF16),
        scratch_shapes=[pltpu.VMEM((N_PAIRS, SEQ, LANES), _F32), pltpu.VMEM((SEQ, D_MODEL), _BF16)],
        compiler_params=_params("parallel", "arbitrary"),
        name="qkv_dil",
    )(x3d, gain, w_bf16)


def _dil_kernel(slopes_ref, q0, k0, v0, q1, k1, v1, q2, k2, v2, o_ref, og_scr, lg_scr, tab_scr, s_scr):
    pair = pl.program_id(0)
    qkv_refs = ((q0, k0, v0), (q1, k1, v1), (q2, k2, v2))
    lane = lax.broadcasted_iota(jnp.int32, (BAND_BLOCK, LANES), 1)
    first_head = lane < HEAD_DIM
    max_width = 2 * BAND_BLOCK

    @pl.when(pl.program_id(1) == 0)
    def _():
        rel = (lax.broadcasted_iota(jnp.int32, (BAND_BLOCK, max_width), 0)
               - lax.broadcasted_iota(jnp.int32, (BAND_BLOCK, max_width), 1))
        for grp, (window, dil) in enumerate(DIL_GROUPS):
            radius = window // (2 * dil)
            for var, off in enumerate(DIL_WINDOW_OFFSETS):
                dist = jnp.abs(rel + off)
                distf = dist.astype(_F32)
                for hh in range(2):
                    tab_scr[grp, var, hh] = -((slopes_ref[2 * pair + hh] * dil) * distf)
                tab_scr[grp, var, 2] = jnp.where(dist <= radius, jnp.finfo(_F32).max, NEG_INF).astype(_F32)

    for grp, (window, dil) in enumerate(DIL_GROUPS):
        q_ref, k_ref, v_ref = qkv_refs[grp]
        radius = window // (2 * dil)
        sub = SEQ // dil
        blocks_per_sub = sub // BAND_BLOCK
        width = min(max_width, sub)

        def blocks(it, carry, q_ref=q_ref, k_ref=k_ref, v_ref=v_ref, dil=dil, radius=radius, sub=sub,
                   blocks_per_sub=blocks_per_sub, width=width, grp=grp):
            units = []
            for u in range(DIL_BLOCKS_PER_ITER):
                t = it * DIL_BLOCKS_PER_ITER + u
                res = t // blocks_per_sub
                c = t % blocks_per_sub
                kstart = jnp.clip(c * BAND_BLOCK - radius, 0, sub - width)
                krow = pl.multiple_of(res * sub + kstart, radius)
                q = q_ref[0, 0, pl.ds(pl.multiple_of(t * BAND_BLOCK, BAND_BLOCK), BAND_BLOCK), :] * Q_SCALE
                k = k_ref[0, 0, pl.ds(krow, width), :]
                for hh in range(2):
                    keep = first_head if hh == 0 else jnp.logical_not(first_head)
                    qm = jnp.where(keep, q, jnp.zeros_like(q))
                    s_scr[2 * u + hh, :, :width] = lax.dot_general(
                        qm, k, (((1,), (1,)), ((), ())), preferred_element_type=_F32)
                units.append((u, t, res, c, krow))
            for u, t, res, c, krow in units:
                v = v_ref[0, 0, pl.ds(krow, width), :]
                if blocks_per_sub == 1:
                    var = 0
                else:
                    var = jnp.int32(c > 0) + jnp.int32(c == blocks_per_sub - 1)
                outs, lses = [], []
                for hh in range(2):
                    s = jnp.minimum(s_scr[2 * u + hh, :, :width] + tab_scr[grp, var, hh, :, :width],
                                    tab_scr[grp, var, 2, :, :width])
                    o_h, m, l = _softmax_pv(s, v)
                    outs.append(o_h)
                    lses.append(jnp.broadcast_to(m + jnp.log(l), (BAND_BLOCK, LANES)))
                o = jnp.where(first_head, outs[0], outs[1])
                lse = jnp.where(first_head, lses[0], lses[1])
                if dil == 1:
                    rows = pl.ds(pl.multiple_of(t * BAND_BLOCK, BAND_BLOCK), BAND_BLOCK)
                else:
                    rows = pl.ds(c * (BAND_BLOCK * dil) + res, BAND_BLOCK, stride=dil)
                og_scr[grp, rows, :] = o
                lg_scr[grp, rows, :] = lse
            return carry

        lax.fori_loop(0, SEQ // (BAND_BLOCK * DIL_BLOCKS_PER_ITER), blocks, 0)

    chunk = 256
    for t in range(SEQ // chunk):
        rows = slice(t * chunk, (t + 1) * chunk)
        lse = [lg_scr[grp, rows, :] for grp in range(N_GROUPS)]
        mx = jnp.maximum(jnp.maximum(lse[0], lse[1]), lse[2])
        e = [jnp.exp(x - mx) for x in lse]
        den = e[0] + e[1] + e[2]
        num = e[0] * og_scr[0, rows, :] + e[1] * og_scr[1, rows, :] + e[2] * og_scr[2, rows, :]
        o_ref[0, 0, rows, :] = (num / den).astype(_BF16)


def _dil_attention(qkv, slopes):
    slab = (1, 1, SEQ, LANES)

    def spec(grp, which):
        base = (grp * 3 + which) * N_PAIRS
        return pl.BlockSpec(slab, lambda p, b: (b, base + p, 0, 0))

    return pl.pallas_call(
        _dil_kernel,
        grid=(N_PAIRS, BATCH),
        in_specs=[pl.BlockSpec(memory_space=pltpu.SMEM)]
        + [spec(grp, which) for grp in range(N_GROUPS) for which in range(3)],
        out_specs=pl.BlockSpec(slab, lambda p, b: (b, p, 0, 0)),
        out_shape=jax.ShapeDtypeStruct((BATCH, N_PAIRS, SEQ, LANES), _BF16),
        scratch_shapes=[
            pltpu.VMEM((N_GROUPS, SEQ, LANES), _F32),
            pltpu.VMEM((N_GROUPS, SEQ, LANES), _F32),
            pltpu.VMEM((N_GROUPS, len(DIL_WINDOW_OFFSETS), 3, BAND_BLOCK, 2 * BAND_BLOCK), _F32),
            pltpu.VMEM((2 * DIL_BLOCKS_PER_ITER, BAND_BLOCK, 2 * BAND_BLOCK), _F32),
        ],
        compiler_params=_params("arbitrary", "arbitrary"),
        name="dil_attention",
    )(slopes, *([qkv] * 9))


def kernel(x, norm_mix_pre, norm_mix_post, norm_ffn_pre, norm_ffn_post, na_w_qkv, na_w_o, na_rpb, dil_w_qkv, dil_w_o,
           ffn_w_gate, ffn_w_up, ffn_w_down):
    x2d = x.reshape(N_TOK, D_MODEL)
    slopes = 2.0 ** (-8.0 * jnp.arange(1, N_HEADS + 1, dtype=_F32) / N_HEADS)

    def gain(g, layer):
        return g[layer].reshape(1, D_MODEL)

    for layer in range(DEPTH):
        j = layer // 2
        if layer % 2 == 0:
            qkv = _qkv_na(x2d, gain(norm_mix_pre, layer), na_w_qkv[j].astype(_BF16))
            o = _na_attention(qkv, _na_bias_table(na_rpb[j]))
            w_o = na_w_o[j]
        else:
            qkv = _qkv_dil(x2d.reshape(BATCH, SEQ, D_MODEL), gain(norm_mix_pre, layer), dil_w_qkv[j].astype(_BF16))
            o = _dil_attention(qkv, slopes)
            w_o = dil_w_o[j]
        x2d = _mix_ffn(o, x2d, w_o.astype(_BF16), gain(norm_mix_post, layer), gain(norm_ffn_pre, layer),
                       ffn_w_gate[layer].astype(_BF16), ffn_w_up[layer].astype(_BF16),
                       ffn_w_down[layer].astype(_BF16), gain(norm_ffn_post, layer))
    return x2d.reshape(BATCH, SEQ, D_MODEL)
```

```python
import functools

import jax
import jax.numpy as jnp
from jax import lax
from jax.experimental import pallas as pl
from jax.experimental.pallas import tpu as pltpu

D_MODEL = 1024
BATCH = 8
SEQ = 2048
DEPTH = 2
N_HEADS = 16
HEAD_DIM = 64
GRID_W = 64
GRID_ROWS = SEQ // GRID_W
NA_ROWS = 8
NA_COLS = 16
DIL_GROUPS = ((128, 1), (512, 4), (2048, 16))
N_GROUPS = len(DIL_GROUPS)
BAND_BLOCK = 128
D_FF = 2816
RMS_EPS = 1e-6
NEG_INF = -1e30
LOG2E = 1.4426950408889634
QK_SCALE = HEAD_DIM ** -0.5 * LOG2E

LANES = 128
N_PAIRS = D_MODEL // LANES
N_TOK = BATCH * SEQ
TM = 512
FF_SPLITS = ((0, 1536), (1536, 1280))
DIL_TN = 512
NA_ROWS_PER_ITER = 4
DIL_BLOCKS_PER_ITER = 4
DIL_WINDOW_OFFSETS = (0, BAND_BLOCK // 2, BAND_BLOCK)
VMEM_LIMIT = 56 * 1024 * 1024

_BF16 = jnp.bfloat16
_F32 = jnp.float32


def _rms(x, g):
    ms = jnp.mean(x * x, axis=-1, keepdims=True)
    return (x * lax.rsqrt(ms + RMS_EPS)) * g


def _resident(shape):
    return pl.BlockSpec(shape, lambda *_: (0,) * len(shape), pipeline_mode=pl.Buffered(1))


def _params(*sem):
    return pltpu.CompilerParams(dimension_semantics=sem, vmem_limit_bytes=VMEM_LIMIT)


def _qkv_na_kernel(x_ref, g_ref, w_ref, o_ref):
    h = _rms(x_ref[...], g_ref[...]).astype(_BF16)
    for j in range(3):
        y = jnp.dot(h, w_ref[:, j * D_MODEL:(j + 1) * D_MODEL], preferred_element_type=_F32)
        if j == 0:
            y = y * QK_SCALE
        for c in range(N_PAIRS):
            o_ref[0, j * N_PAIRS + c] = y[:, c * LANES:(c + 1) * LANES].astype(_BF16)


def _qkv_na(x2d, gain, w_bf16):
    tiles_per_seq = SEQ // TM
    return pl.pallas_call(
        _qkv_na_kernel,
        grid=(N_TOK // TM,),
        in_specs=[
            pl.BlockSpec((TM, D_MODEL), lambda i: (i, 0)),
            _resident((1, D_MODEL)),
            _resident((D_MODEL, 3 * D_MODEL)),
        ],
        out_specs=pl.BlockSpec((1, 3 * N_PAIRS, TM, LANES),
                               lambda i: (i // tiles_per_seq, 0, i % tiles_per_seq, 0)),
        out_shape=jax.ShapeDtypeStruct((BATCH, 3 * N_PAIRS, SEQ, LANES), _BF16),
        compiler_params=_params("parallel"),
        name="qkv_na",
    )(x2d, gain, w_bf16)


def _stack_heads(q, first_head):
    zero = jnp.zeros_like(q)
    return jnp.concatenate([jnp.where(first_head, q, zero), jnp.where(first_head, zero, q)], axis=0)


def _stacked_softmax_pv(s, v, first_head):
    rows = s.shape[0] // 2
    m = jnp.max(s, axis=-1, keepdims=True)
    p = jnp.exp2(s - m).astype(_BF16)
    r = jnp.dot(p, jnp.concatenate([v, jnp.ones_like(v)], axis=1), preferred_element_type=_F32)
    num = jnp.where(first_head, r[:rows, :LANES], r[rows:, :LANES])
    den = jnp.where(first_head, r[:rows, LANES:], r[rows:, LANES:])
    return num * (1.0 / den), m, den


def _na_kernel(q_ref, k_ref, v_ref, bias_ref, o_ref, cap_scr, s_scr):
    kh = NA_ROWS
    nk = kh * GRID_W
    lane = lax.broadcasted_iota(jnp.int32, (GRID_W, LANES), 1)
    first_head = lane < HEAD_DIM

    qcol = lax.broadcasted_iota(jnp.int32, (2 * GRID_W, nk), 0) % GRID_W
    kcol = lax.broadcasted_iota(jnp.int32, (2 * GRID_W, nk), 1) % GRID_W
    col_start = jnp.clip(qcol - NA_COLS // 2, 0, GRID_W - NA_COLS)
    col_ok = (kcol >= col_start) & (kcol < col_start + NA_COLS)
    cap_scr[...] = jnp.where(col_ok, jnp.finfo(_F32).max, NEG_INF).astype(_F32)

    def rows(it, carry):
        units = []
        for u in range(NA_ROWS_PER_ITER):
            i = it * NA_ROWS_PER_ITER + u
            rs = jnp.clip(i - kh // 2, 0, GRID_ROWS - kh)
            q = q_ref[0, 0, pl.ds(pl.multiple_of(i * GRID_W, GRID_W), GRID_W), :]
            k = k_ref[0, 0, pl.ds(pl.multiple_of(rs * GRID_W, GRID_W), nk), :]
            s_scr[u] = lax.dot_general(_stack_heads(q, first_head), k, (((1,), (1,)), ((), ())),
                                       preferred_element_type=_F32)
            units.append((u, i, rs))
        for u, i, rs in units:
            v = v_ref[0, 0, pl.ds(pl.multiple_of(rs * GRID_W, GRID_W), nk), :]
            s = jnp.minimum(s_scr[u] + bias_ref[0, i - rs], cap_scr[...])
            o, _, _ = _stacked_softmax_pv(s, v, first_head)
            o_ref[0, 0, pl.ds(pl.multiple_of(i * GRID_W, GRID_W), GRID_W), :] = o.astype(_BF16)
        return carry

    lax.fori_loop(0, GRID_ROWS // NA_ROWS_PER_ITER, rows, 0)


def _na_attention(qkv, bias_tbl):
    slab = (1, 1, SEQ, LANES)
    return pl.pallas_call(
        _na_kernel,
        grid=(N_PAIRS, BATCH),
        in_specs=[
            pl.BlockSpec(slab, lambda p, b: (b, p, 0, 0)),
            pl.BlockSpec(slab, lambda p, b: (b, N_PAIRS + p, 0, 0)),
            pl.BlockSpec(slab, lambda p, b: (b, 2 * N_PAIRS + p, 0, 0)),
            pl.BlockSpec((1, NA_ROWS, 2 * GRID_W, NA_ROWS * GRID_W), lambda p, b: (p, 0, 0, 0)),
        ],
        out_specs=pl.BlockSpec(slab, lambda p, b: (b, p, 0, 0)),
        out_shape=jax.ShapeDtypeStruct((BATCH, N_PAIRS, SEQ, LANES), _BF16),
        scratch_shapes=[pltpu.VMEM((2 * GRID_W, NA_ROWS * GRID_W), _F32),
                        pltpu.VMEM((NA_ROWS_PER_ITER, 2 * GRID_W, NA_ROWS * GRID_W), _F32)],
        compiler_params=_params("arbitrary", "arbitrary"),
        name="na_attention",
    )(qkv, qkv, qkv, bias_tbl)


def _na_bias_table(rpb):
    col = jnp.arange(GRID_W)
    col_idx = jnp.clip(col[None, :] - col[:, None] + NA_COLS - 1, 0, 2 * NA_COLS - 2)
    rpb_cols = (rpb.astype(_F32) * LOG2E)[:, :, col_idx]
    per_delta = [rpb_cols[:, NA_ROWS - 1 - d: 2 * NA_ROWS - 1 - d] for d in range(NA_ROWS)]
    tbl = jnp.stack(per_delta, axis=1)
    tbl = tbl.reshape(N_PAIRS, 2, NA_ROWS, NA_ROWS, GRID_W, GRID_W)
    tbl = jnp.transpose(tbl, (0, 2, 1, 4, 3, 5))
    return tbl.reshape(N_PAIRS, NA_ROWS, 2 * GRID_W, NA_ROWS * GRID_W)


def _mix_ffn_kernel(o_ref, x_ref, wo_ref, gmix_ref, gpre_ref, wg_ref, wu_ref, wd_ref, gpost_ref, out_ref):
    o = jnp.concatenate([o_ref[0, c] for c in range(N_PAIRS)], axis=-1)
    y = jnp.dot(o, wo_ref[...], preferred_element_type=_F32)
    x = x_ref[...] + _rms(y, gmix_ref[...])
    h = _rms(x, gpre_ref[...]).astype(_BF16)
    y = None
    for start, size in FF_SPLITS:
        g = jnp.dot(h, wg_ref[:, start:start + size], preferred_element_type=_F32)
        u = jnp.dot(h, wu_ref[:, start:start + size], preferred_element_type=_F32)
        a = ((g * jax.nn.sigmoid(g)) * u).astype(_BF16)
        part = jnp.dot(a, wd_ref[start:start + size, :], preferred_element_type=_F32)
        y = part if y is None else y + part
    out_ref[...] = x + _rms(y, gpost_ref[...])


def _mix_ffn(o_cbm, x2d, wo, gmix, gpre, wg, wu, wd, gpost):
    tiles_per_seq = SEQ // TM
    return pl.pallas_call(
        _mix_ffn_kernel,
        grid=(N_TOK // TM,),
        in_specs=[
            pl.BlockSpec((1, N_PAIRS, TM, LANES), lambda i: (i // tiles_per_seq, 0, i % tiles_per_seq, 0)),
            pl.BlockSpec((TM, D_MODEL), lambda i: (i, 0)),
            _resident((D_MODEL, D_MODEL)),
            _resident((1, D_MODEL)),
            _resident((1, D_MODEL)),
            _resident((D_MODEL, D_FF)),
            _resident((D_MODEL, D_FF)),
            _resident((D_FF, D_MODEL)),
            _resident((1, D_MODEL)),
        ],
        out_specs=pl.BlockSpec((TM, D_MODEL), lambda i: (i, 0)),
        out_shape=jax.ShapeDtypeStruct((N_TOK, D_MODEL), _F32),
        compiler_params=_params("parallel"),
        name="mix_ffn",
    )(o_cbm, x2d, wo, gmix, gpre, wg, wu, wd, gpost)


def _qkv_dil_kernel(x_ref, g_ref, w_ref, o_ref, h32_scr, hperm_scr):
    j = pl.program_id(1)
    tiles_per_group = 3 * D_MODEL // DIL_TN

    @pl.when(j == 0)
    def _():
        for t in range(SEQ // TM):
            rows = slice(t * TM, (t + 1) * TM)
            h = _rms(x_ref[0, rows, :], g_ref[...])
            for c in range(N_PAIRS):
                h32_scr[c, rows, :] = h[:, c * LANES:(c + 1) * LANES]

    for grp, (_, dil) in enumerate(DIL_GROUPS):
        @pl.when(j == grp * tiles_per_group)
        def _(dil=dil):
            sub = SEQ // dil
            for r in range(dil):
                for c in range(N_PAIRS):
                    if dil == 1:
                        val = h32_scr[c]
                    else:
                        val = h32_scr[c, pl.ds(r, sub, stride=dil), :]
                    hperm_scr[r * sub:(r + 1) * sub, c * LANES:(c + 1) * LANES] = val.astype(_BF16)

    y = jnp.dot(hperm_scr[...], w_ref[...], preferred_element_type=_F32)
    is_q_tile = j % tiles_per_group < D_MODEL // DIL_TN
    y = y * jnp.where(is_q_tile, QK_SCALE, 1.0)
    for c in range(DIL_TN // LANES):
        o_ref[0, c] = y[:, c * LANES:(c + 1) * LANES].astype(_BF16)


def _qkv_dil(x3d, gain, w_bf16):
    n_col = N_GROUPS * 3 * D_MODEL
    blocks_per_tile = DIL_TN // LANES
    return pl.pallas_call(
        _qkv_dil_kernel,
        grid=(BATCH, n_col // DIL_TN),
        in_specs=[
            pl.BlockSpec((1, SEQ, D_MODEL), lambda b, j: (b, 0, 0)),
            _resident((1, D_MODEL)),
            pl.BlockSpec((D_MODEL, DIL_TN), lambda b, j: (0, j)),
        ],
        out_specs=pl.BlockSpec((1, blocks_per_tile, SEQ, LANES), lambda b, j: (b, j, 0, 0)),
        out_shape=jax.ShapeDtypeStruct((BATCH, n_col // LANES, SEQ, LANES), _BF16),
        scratch_shapes=[pltpu.VMEM((N_PAIRS, SEQ, LANES), _F32), pltpu.VMEM((SEQ, D_MODEL), _BF16)],
        compiler_params=_params("parallel", "arbitrary"),
        name="qkv_dil",
    )(x3d, gain, w_bf16)


def _dil_kernel(slopes_ref, q0, k0, v0, q1, k1, v1, q2, k2, v2, o_ref, og_scr, lg_scr, tab_scr, s_scr):
    pair = pl.program_id(0)
    qkv_refs = ((q0, k0, v0), (q1, k1, v1), (q2, k2, v2))
    lane = lax.broadcasted_iota(jnp.int32, (BAND_BLOCK, LANES), 1)
    first_head = lane < HEAD_DIM
    max_width = 2 * BAND_BLOCK

    @pl.when(pl.program_id(1) == 0)
    def _():
        rel = (lax.broadcasted_iota(jnp.int32, (BAND_BLOCK, max_width), 0)
               - lax.broadcasted_iota(jnp.int32, (BAND_BLOCK, max_width), 1))
        for grp, (window, dil) in enumerate(DIL_GROUPS):
            radius = window // (2 * dil)
            for var, off in enumerate(DIL_WINDOW_OFFSETS):
                dist = jnp.abs(rel + off)
                distf = dist.astype(_F32)
                cap = jnp.where(dist <= radius, jnp.finfo(_F32).max, NEG_INF).astype(_F32)
                for hh in range(2):
                    head_rows = slice(hh * BAND_BLOCK, (hh + 1) * BAND_BLOCK)
                    tab_scr[grp, var, 0, head_rows, :] = -((slopes_ref[2 * pair + hh] * dil) * distf) * LOG2E
                    tab_scr[grp, var, 1, head_rows, :] = cap

    for grp, (window, dil) in enumerate(DIL_GROUPS):
        q_ref, k_ref, v_ref = qkv_refs[grp]
        radius = window // (2 * dil)
        sub = SEQ // dil
        blocks_per_sub = sub // BAND_BLOCK
        width = min(max_width, sub)

        def blocks(it, carry, q_ref=q_ref, k_ref=k_ref, v_ref=v_ref, dil=dil, radius=radius, sub=sub,
                   blocks_per_sub=blocks_per_sub, width=width, grp=grp):
            units = []
            for u in range(DIL_BLOCKS_PER_ITER):
                t = it * DIL_BLOCKS_PER_ITER + u
                res = t // blocks_per_sub
                c = t % blocks_per_sub
                kstart = jnp.clip(c * BAND_BLOCK - radius, 0, sub - width)
                krow = pl.multiple_of(res * sub + kstart, radius)
                q = q_ref[0, 0, pl.ds(pl.multiple_of(t * BAND_BLOCK, BAND_BLOCK), BAND_BLOCK), :]
                k = k_ref[0, 0, pl.ds(krow, width), :]
                s_scr[u, :, :width] = lax.dot_general(_stack_heads(q, first_head), k, (((1,), (1,)), ((), ())),
                                                      preferred_element_type=_F32)
                units.append((u, t, res, c, krow))
            for u, t, res, c, krow in units:
                if blocks_per_sub == 1:
                    var = 0
                else:
                    var = jnp.int32(c > 0) + jnp.int32(c == blocks_per_sub - 1)
                v = v_ref[0, 0, pl.ds(krow, width), :]
                s = jnp.minimum(s_scr[u, :, :width] + tab_scr[grp, var, 0, :, :width],
                                tab_scr[grp, var, 1, :, :width])
                o, m, den = _stacked_softmax_pv(s, v, first_head)
                lse = jnp.where(first_head, m[:BAND_BLOCK], m[BAND_BLOCK:]) + jnp.log2(den)
                if dil == 1:
                    rows = pl.ds(pl.multiple_of(t * BAND_BLOCK, BAND_BLOCK), BAND_BLOCK)
                else:
                    rows = pl.ds(c * (BAND_BLOCK * dil) + res, BAND_BLOCK, stride=dil)
                og_scr[grp, rows, :] = o
                lg_scr[grp, rows, :] = lse
            return carry

        lax.fori_loop(0, SEQ // (BAND_BLOCK * DIL_BLOCKS_PER_ITER), blocks, 0)

    chunk = 256
    for t in range(SEQ // chunk):
        rows = slice(t * chunk, (t + 1) * chunk)
        lse = [lg_scr[grp, rows, :] for grp in range(N_GROUPS)]
        mx = jnp.maximum(jnp.maximum(lse[0], lse[1]), lse[2])
        e = [jnp.exp2(x - mx) for x in lse]
        den = e[0] + e[1] + e[2]
        num = e[0] * og_scr[0, rows, :] + e[1] * og_scr[1, rows, :] + e[2] * og_scr[2, rows, :]
        o_ref[0, 0, rows, :] = (num / den).astype(_BF16)


def _dil_attention(qkv, slopes):
    slab = (1, 1, SEQ, LANES)

    def spec(grp, which):
        base = (grp * 3 + which) * N_PAIRS
        return pl.BlockSpec(slab, lambda p, b: (b, base + p, 0, 0))

    return pl.pallas_call(
        _dil_kernel,
        grid=(N_PAIRS, BATCH),
        in_specs=[pl.BlockSpec(memory_space=pltpu.SMEM)]
        + [spec(grp, which) for grp in range(N_GROUPS) for which in range(3)],
        out_specs=pl.BlockSpec(slab, lambda p, b: (b, p, 0, 0)),
        out_shape=jax.ShapeDtypeStruct((BATCH, N_PAIRS, SEQ, LANES), _BF16),
        scratch_shapes=[
            pltpu.VMEM((N_GROUPS, SEQ, LANES), _F32),
            pltpu.VMEM((N_GROUPS, SEQ, LANES), _F32),
            pltpu.VMEM((N_GROUPS, len(DIL_WINDOW_OFFSETS), 2, 2 * BAND_BLOCK, 2 * BAND_BLOCK), _F32),
            pltpu.VMEM((DIL_BLOCKS_PER_ITER, 2 * BAND_BLOCK, 2 * BAND_BLOCK), _F32),
        ],
        compiler_params=_params("arbitrary", "arbitrary"),
        name="dil_attention",
    )(slopes, *([qkv] * 9))


def kernel(x, norm_mix_pre, norm_mix_post, norm_ffn_pre, norm_ffn_post, na_w_qkv, na_w_o, na_rpb, dil_w_qkv, dil_w_o,
           ffn_w_gate, ffn_w_up, ffn_w_down):
    x2d = x.reshape(N_TOK, D_MODEL)
    slopes = 2.0 ** (-8.0 * jnp.arange(1, N_HEADS + 1, dtype=_F32) / N_HEADS)

    def gain(g, layer):
        return g[layer].reshape(1, D_MODEL)

    for layer in range(DEPTH):
        j = layer // 2
        if layer % 2 == 0:
            qkv = _qkv_na(x2d, gain(norm_mix_pre, layer), na_w_qkv[j].astype(_BF16))
            o = _na_attention(qkv, _na_bias_table(na_rpb[j]))
            w_o = na_w_o[j]
        else:
            qkv = _qkv_dil(x2d.reshape(BATCH, SEQ, D_MODEL), gain(norm_mix_pre, layer), dil_w_qkv[j].astype(_BF16))
            o = _dil_attention(qkv, slopes)
            w_o = dil_w_o[j]
        x2d = _mix_ffn(o, x2d, w_o.astype(_BF16), gain(norm_mix_post, layer), gain(norm_ffn_pre, layer),
                       ffn_w_gate[layer].astype(_BF16), ffn_w_up[layer].astype(_BF16),
                       ffn_w_down[layer].astype(_BF16), gain(norm_ffn_post, layer))
    return x2d.reshape(BATCH, SEQ, D_MODEL)
```

```python
import functools

import jax
import jax.numpy as jnp
from jax import lax
from jax.experimental import pallas as pl
from jax.experimental.pallas import tpu as pltpu

D_MODEL = 1024
BATCH = 8
SEQ = 2048
DEPTH = 2
N_HEADS = 16
HEAD_DIM = 64
GRID_W = 64
GRID_ROWS = SEQ // GRID_W
NA_ROWS = 8
NA_COLS = 16
DIL_GROUPS = ((128, 1), (512, 4), (2048, 16))
N_GROUPS = len(DIL_GROUPS)
BAND_BLOCK = 128
D_FF = 2816
RMS_EPS = 1e-6
NEG_INF = -1e30
LOG2E = 1.4426950408889634
QK_SCALE = HEAD_DIM ** -0.5 * LOG2E

LANES = 128
N_PAIRS = D_MODEL // LANES
N_TOK = BATCH * SEQ
TM = 512
FF_SPLITS = ((0, 1536), (1536, 1280))
DIL_TN = 512
NA_ROWS_PER_ITER = 8
DIL_BLOCKS_PER_ITER = 8
DIL_WINDOW_OFFSETS = (0, BAND_BLOCK // 2, BAND_BLOCK)
VMEM_LIMIT = 56 * 1024 * 1024

_BF16 = jnp.bfloat16
_F32 = jnp.float32


def _rms(x, g):
    ms = jnp.mean(x * x, axis=-1, keepdims=True)
    return (x * lax.rsqrt(ms + RMS_EPS)) * g


def _resident(shape):
    return pl.BlockSpec(shape, lambda *_: (0,) * len(shape), pipeline_mode=pl.Buffered(1))


def _params(*sem):
    return pltpu.CompilerParams(dimension_semantics=sem, vmem_limit_bytes=VMEM_LIMIT)


def _qkv_na_kernel(x_ref, g_ref, w_ref, o_ref):
    h = _rms(x_ref[...], g_ref[...]).astype(_BF16)
    for j in range(3):
        y = jnp.dot(h, w_ref[:, j * D_MODEL:(j + 1) * D_MODEL], preferred_element_type=_F32)
        if j == 0:
            y = y * QK_SCALE
        for c in range(N_PAIRS):
            o_ref[0, j * N_PAIRS + c] = y[:, c * LANES:(c + 1) * LANES].astype(_BF16)


def _qkv_na(x2d, gain, w_bf16):
    tiles_per_seq = SEQ // TM
    return pl.pallas_call(
        _qkv_na_kernel,
        grid=(N_TOK // TM,),
        in_specs=[
            pl.BlockSpec((TM, D_MODEL), lambda i: (i, 0)),
            _resident((1, D_MODEL)),
            _resident((D_MODEL, 3 * D_MODEL)),
        ],
        out_specs=pl.BlockSpec((1, 3 * N_PAIRS, TM, LANES),
                               lambda i: (i // tiles_per_seq, 0, i % tiles_per_seq, 0)),
        out_shape=jax.ShapeDtypeStruct((BATCH, 3 * N_PAIRS, SEQ, LANES), _BF16),
        compiler_params=_params("parallel"),
        name="qkv_na",
    )(x2d, gain, w_bf16)


def _stack_heads(q, first_head):
    zero = jnp.zeros_like(q)
    return jnp.concatenate([jnp.where(first_head, q, zero), jnp.where(first_head, zero, q)], axis=0)


def _masked_probs(s, bias, cap):
    s = jnp.minimum(s + bias, cap)
    m = jnp.max(s, axis=-1, keepdims=True)
    return jnp.exp2(s - m).astype(_BF16), m


def _stacked_pv(p, v, first_head):
    rows = p.shape[0] // 2
    r = jnp.dot(p, jnp.concatenate([v, jnp.ones_like(v)], axis=1), preferred_element_type=_F32)
    num = jnp.where(first_head, r[:rows, :LANES], r[rows:, :LANES])
    den = jnp.where(first_head, r[:rows, LANES:], r[rows:, LANES:])
    return num * (1.0 / den), den


def _na_kernel(q_ref, k_ref, v_ref, rpb_ref, o_ref, bias_scr, cap_scr, p_scr):
    kh = NA_ROWS
    nk = kh * GRID_W
    per_iter = NA_ROWS_PER_ITER
    n_iter = GRID_ROWS // per_iter
    lane = lax.broadcasted_iota(jnp.int32, (GRID_W, LANES), 1)
    first_head = lane < HEAD_DIM

    @pl.when(pl.program_id(1) == 0)
    def _():
        for d in range(kh):
            for hh in range(2):
                for r in range(kh):
                    bias_scr[d, hh * GRID_W:(hh + 1) * GRID_W, r * GRID_W:(r + 1) * GRID_W] = (
                        rpb_ref[hh, kh - 1 - d + r])
        qcol = lax.broadcasted_iota(jnp.int32, (2 * GRID_W, LANES), 0) % GRID_W
        kcol = lax.broadcasted_iota(jnp.int32, (2 * GRID_W, LANES), 1) % GRID_W
        col_start = jnp.clip(qcol - NA_COLS // 2, 0, GRID_W - NA_COLS)
        col_ok = (kcol >= col_start) & (kcol < col_start + NA_COLS)
        cap_scr[...] = jnp.where(col_ok, jnp.finfo(_F32).max, NEG_INF).astype(_F32)

    def window(i):
        rs = jnp.clip(i - kh // 2, 0, GRID_ROWS - kh)
        return rs, pl.ds(pl.multiple_of(rs * GRID_W, GRID_W), nk)

    def query_rows(i):
        return pl.ds(pl.multiple_of(i * GRID_W, GRID_W), GRID_W)

    def scores(slot, u, i):
        rs, win = window(i)
        s = lax.dot_general(_stack_heads(q_ref[0, 0, query_rows(i), :], first_head), k_ref[0, 0, win, :],
                            (((1,), (1,)), ((), ())), preferred_element_type=_F32)
        cap = jnp.concatenate([cap_scr[...]] * (nk // LANES), axis=1)
        p_scr[slot, u], _ = _masked_probs(s, bias_scr[i - rs], cap)

    def outputs(slot, u, i):
        _, win = window(i)
        o, _ = _stacked_pv(p_scr[slot, u], v_ref[0, 0, win, :], first_head)
        o_ref[0, 0, query_rows(i), :] = o.astype(_BF16)

    for u in range(per_iter):
        scores(0, u, u)

    def body(it, carry):
        slot = it % 2
        for u in range(per_iter):
            outputs(slot, u, it * per_iter + u)
            scores(1 - slot, u, (it + 1) * per_iter + u)
        return carry

    lax.fori_loop(0, n_iter - 1, body, 0)
    for u in range(per_iter):
        outputs((n_iter - 1) % 2, u, (n_iter - 1) * per_iter + u)


def _na_attention(qkv, rpb_cols):
    slab = (1, 1, SEQ, LANES)
    nk = NA_ROWS * GRID_W
    return pl.pallas_call(
        _na_kernel,
        grid=(N_PAIRS, BATCH),
        in_specs=[
            pl.BlockSpec(slab, lambda p, b: (b, p, 0, 0)),
            pl.BlockSpec(slab, lambda p, b: (b, N_PAIRS + p, 0, 0)),
            pl.BlockSpec(slab, lambda p, b: (b, 2 * N_PAIRS + p, 0, 0)),
            pl.BlockSpec((2, 2 * NA_ROWS - 1, GRID_W, GRID_W), lambda p, b: (p, 0, 0, 0)),
        ],
        out_specs=pl.BlockSpec(slab, lambda p, b: (b, p, 0, 0)),
        out_shape=jax.ShapeDtypeStruct((BATCH, N_PAIRS, SEQ, LANES), _BF16),
        scratch_shapes=[pltpu.VMEM((NA_ROWS, 2 * GRID_W, nk), _F32),
                        pltpu.VMEM((2 * GRID_W, LANES), _F32),
                        pltpu.VMEM((2, NA_ROWS_PER_ITER, 2 * GRID_W, nk), _BF16)],
        compiler_params=_params("arbitrary", "arbitrary"),
        name="na_attention",
    )(qkv, qkv, qkv, rpb_cols)


def _na_rpb_cols(rpb):
    col = jnp.arange(GRID_W)
    col_idx = jnp.clip(col[None, :] - col[:, None] + NA_COLS - 1, 0, 2 * NA_COLS - 2)
    return (rpb.astype(_F32) * LOG2E)[:, :, col_idx]


def _mix_ffn_kernel(o_ref, x_ref, wo_ref, gmix_ref, gpre_ref, wg_ref, wu_ref, wd_ref, gpost_ref, out_ref):
    o = jnp.concatenate([o_ref[0, c] for c in range(N_PAIRS)], axis=-1)
    y = jnp.dot(o, wo_ref[...], preferred_element_type=_F32)
    x = x_ref[...] + _rms(y, gmix_ref[...])
    h = _rms(x, gpre_ref[...]).astype(_BF16)
    y = None
    for start, size in FF_SPLITS:
        g = jnp.dot(h, wg_ref[:, start:start + size], preferred_element_type=_F32)
        u = jnp.dot(h, wu_ref[:, start:start + size], preferred_element_type=_F32)
        a = ((g * jax.nn.sigmoid(g)) * u).astype(_BF16)
        part = jnp.dot(a, wd_ref[start:start + size, :], preferred_element_type=_F32)
        y = part if y is None else y + part
    out_ref[...] = x + _rms(y, gpost_ref[...])


def _mix_ffn(o_cbm, x2d, wo, gmix, gpre, wg, wu, wd, gpost):
    tiles_per_seq = SEQ // TM
    return pl.pallas_call(
        _mix_ffn_kernel,
        grid=(N_TOK // TM,),
        in_specs=[
            pl.BlockSpec((1, N_PAIRS, TM, LANES), lambda i: (i // tiles_per_seq, 0, i % tiles_per_seq, 0)),
            pl.BlockSpec((TM, D_MODEL), lambda i: (i, 0)),
            _resident((D_MODEL, D_MODEL)),
            _resident((1, D_MODEL)),
            _resident((1, D_MODEL)),
            _resident((D_MODEL, D_FF)),
            _resident((D_MODEL, D_FF)),
            _resident((D_FF, D_MODEL)),
            _resident((1, D_MODEL)),
        ],
        out_specs=pl.BlockSpec((TM, D_MODEL), lambda i: (i, 0)),
        out_shape=jax.ShapeDtypeStruct((N_TOK, D_MODEL), _F32),
        compiler_params=_params("parallel"),
        name="mix_ffn",
    )(o_cbm, x2d, wo, gmix, gpre, wg, wu, wd, gpost)


def _qkv_dil_kernel(x_ref, g_ref, w_ref, o_ref, h32_scr, hperm_scr):
    j = pl.program_id(1)
    tiles_per_group = 3 * D_MODEL // DIL_TN

    @pl.when(j == 0)
    def _():
        for t in range(SEQ // TM):
            rows = slice(t * TM, (t + 1) * TM)
            h = _rms(x_ref[0, rows, :], g_ref[...])
            for c in range(N_PAIRS):
                h32_scr[c, rows, :] = h[:, c * LANES:(c + 1) * LANES]

    for grp, (_, dil) in enumerate(DIL_GROUPS):
        @pl.when(j == grp * tiles_per_group)
        def _(dil=dil):
            sub = SEQ // dil
            for r in range(dil):
                for c in range(N_PAIRS):
                    if dil == 1:
                        val = h32_scr[c]
                    else:
                        val = h32_scr[c, pl.ds(r, sub, stride=dil), :]
                    hperm_scr[r * sub:(r + 1) * sub, c * LANES:(c + 1) * LANES] = val.astype(_BF16)

    y = jnp.dot(hperm_scr[...], w_ref[...], preferred_element_type=_F32)
    is_q_tile = j % tiles_per_group < D_MODEL // DIL_TN
    y = y * jnp.where(is_q_tile, QK_SCALE, 1.0)
    for c in range(DIL_TN // LANES):
        o_ref[0, c] = y[:, c * LANES:(c + 1) * LANES].astype(_BF16)


def _qkv_dil(x3d, gain, w_bf16):
    n_col = N_GROUPS * 3 * D_MODEL
    blocks_per_tile = DIL_TN // LANES
    return pl.pallas_call(
        _qkv_dil_kernel,
        grid=(BATCH, n_col // DIL_TN),
        in_specs=[
            pl.BlockSpec((1, SEQ, D_MODEL), lambda b, j: (b, 0, 0)),
            _resident((1, D_MODEL)),
            pl.BlockSpec((D_MODEL, DIL_TN), lambda b, j: (0, j)),
        ],
        out_specs=pl.BlockSpec((1, blocks_per_tile, SEQ, LANES), lambda b, j: (b, j, 0, 0)),
        out_shape=jax.ShapeDtypeStruct((BATCH, n_col // LANES, SEQ, LANES), _BF16),
        scratch_shapes=[pltpu.VMEM((N_PAIRS, SEQ, LANES), _F32), pltpu.VMEM((SEQ, D_MODEL), _BF16)],
        compiler_params=_params("parallel", "arbitrary"),
        name="qkv_dil",
    )(x3d, gain, w_bf16)


def _dil_kernel(slopes_ref, q0, k0, v0, q1, k1, v1, q2, k2, v2, o_ref, og_scr, lg_scr, bias_scr, cap_scr, p_scr,
                m_scr):
    pair = pl.program_id(0)
    qkv_refs = ((q0, k0, v0), (q1, k1, v1), (q2, k2, v2))
    lane = lax.broadcasted_iota(jnp.int32, (BAND_BLOCK, LANES), 1)
    first_head = lane < HEAD_DIM
    max_width = 2 * BAND_BLOCK
    per_iter = DIL_BLOCKS_PER_ITER
    n_iter = SEQ // (BAND_BLOCK * per_iter)

    @pl.when(pl.program_id(1) == 0)
    def _():
        rel = (lax.broadcasted_iota(jnp.int32, (BAND_BLOCK, max_width), 0)
               - lax.broadcasted_iota(jnp.int32, (BAND_BLOCK, max_width), 1))
        for grp, (window, dil) in enumerate(DIL_GROUPS):
            radius = window // (2 * dil)
            for var, off in enumerate(DIL_WINDOW_OFFSETS):
                dist = jnp.abs(rel + off)
                distf = dist.astype(_F32)
                cap_scr[grp, var] = jnp.where(dist <= radius, jnp.finfo(_F32).max, NEG_INF).astype(_F32)
                for hh in range(2):
                    head_rows = slice(hh * BAND_BLOCK, (hh + 1) * BAND_BLOCK)
                    bias_scr[grp, var, head_rows, :] = -((slopes_ref[2 * pair + hh] * dil) * distf) * LOG2E

    def stage_fns(grp):
        window, dil = DIL_GROUPS[grp]
        q_ref, k_ref, v_ref = qkv_refs[grp]
        radius = window // (2 * dil)
        sub = SEQ // dil
        blocks_per_sub = sub // BAND_BLOCK
        width = min(max_width, sub)

        def place(t):
            res = t // blocks_per_sub
            c = t % blocks_per_sub
            kstart = jnp.clip(c * BAND_BLOCK - radius, 0, sub - width)
            return res, c, pl.ds(pl.multiple_of(res * sub + kstart, radius), width)

        def scores(slot, u, t):
            _, c, key_rows = place(t)
            q = q_ref[0, 0, pl.ds(pl.multiple_of(t * BAND_BLOCK, BAND_BLOCK), BAND_BLOCK), :]
            s = lax.dot_general(_stack_heads(q, first_head), k_ref[0, 0, key_rows, :], (((1,), (1,)), ((), ())),
                                preferred_element_type=_F32)
            var = 0 if blocks_per_sub == 1 else jnp.int32(c > 0) + jnp.int32(c == blocks_per_sub - 1)
            cap = cap_scr[grp, var, :, :width]
            p, m = _masked_probs(s, bias_scr[grp, var, :, :width], jnp.concatenate([cap, cap], axis=0))
            p_scr[slot, u, :, :width] = p
            m_scr[slot, u] = jnp.where(first_head, m[:BAND_BLOCK], m[BAND_BLOCK:])

        def outputs(slot, u, t):
            res, c, key_rows = place(t)
            o, den = _stacked_pv(p_scr[slot, u, :, :width], v_ref[0, 0, key_rows, :], first_head)
            if dil == 1:
                rows = pl.ds(pl.multiple_of(t * BAND_BLOCK, BAND_BLOCK), BAND_BLOCK)
            else:
                rows = pl.ds(c * (BAND_BLOCK * dil) + res, BAND_BLOCK, stride=dil)
            og_scr[grp, rows, :] = o
            lg_scr[grp, rows, :] = m_scr[slot, u] + jnp.log2(den)

        return scores, outputs

    fns = [stage_fns(grp) for grp in range(N_GROUPS)]
    for u in range(per_iter):
        fns[0][0](0, u, u)
    for grp in range(N_GROUPS):
        scores, outputs = fns[grp]

        def body(it, carry, scores=scores, outputs=outputs):
            slot = it % 2
            for u in range(per_iter):
                outputs(slot, u, it * per_iter + u)
                scores(1 - slot, u, (it + 1) * per_iter + u)
            return carry

        lax.fori_loop(0, n_iter - 1, body, 0)
        last_slot = (n_iter - 1) % 2
        for u in range(per_iter):
            outputs(last_slot, u, (n_iter - 1) * per_iter + u)
            if grp + 1 < N_GROUPS:
                fns[grp + 1][0](1 - last_slot, u, u)

    chunk = 256
    for t in range(SEQ // chunk):
        rows = slice(t * chunk, (t + 1) * chunk)
        lse = [lg_scr[grp, rows, :] for grp in range(N_GROUPS)]
        mx = jnp.maximum(jnp.maximum(lse[0], lse[1]), lse[2])
        e = [jnp.exp2(x - mx) for x in lse]
        den = e[0] + e[1] + e[2]
        num = e[0] * og_scr[0, rows, :] + e[1] * og_scr[1, rows, :] + e[2] * og_scr[2, rows, :]
        o_ref[0, 0, rows, :] = (num / den).astype(_BF16)


def _dil_attention(qkv, slopes):
    slab = (1, 1, SEQ, LANES)

    def spec(grp, which):
        base = (grp * 3 + which) * N_PAIRS
        return pl.BlockSpec(slab, lambda p, b: (b, base + p, 0, 0))

    return pl.pallas_call(
        _dil_kernel,
        grid=(N_PAIRS, BATCH),
        in_specs=[pl.BlockSpec(memory_space=pltpu.SMEM)]
        + [spec(grp, which) for grp in range(N_GROUPS) for which in range(3)],
        out_specs=pl.BlockSpec(slab, lambda p, b: (b, p, 0, 0)),
        out_shape=jax.ShapeDtypeStruct((BATCH, N_PAIRS, SEQ, LANES), _BF16),
        scratch_shapes=[
            pltpu.VMEM((N_GROUPS, SEQ, LANES), _F32),
            pltpu.VMEM((N_GROUPS, SEQ, LANES), _F32),
            pltpu.VMEM((N_GROUPS, len(DIL_WINDOW_OFFSETS), 2 * BAND_BLOCK, 2 * BAND_BLOCK), _F32),
            pltpu.VMEM((N_GROUPS, len(DIL_WINDOW_OFFSETS), BAND_BLOCK, 2 * BAND_BLOCK), _F32),
            pltpu.VMEM((2, DIL_BLOCKS_PER_ITER, 2 * BAND_BLOCK, 2 * BAND_BLOCK), _BF16),
            pltpu.VMEM((2, DIL_BLOCKS_PER_ITER, BAND_BLOCK, LANES), _F32),
        ],
        compiler_params=_params("arbitrary", "arbitrary"),
        name="dil_attention",
    )(slopes, *([qkv] * 9))


def kernel(x, norm_mix_pre, norm_mix_post, norm_ffn_pre, norm_ffn_post, na_w_qkv, na_w_o, na_rpb, dil_w_qkv, dil_w_o,
           ffn_w_gate, ffn_w_up, ffn_w_down):
    x2d = x.reshape(N_TOK, D_MODEL)
    slopes = 2.0 ** (-8.0 * jnp.arange(1, N_HEADS + 1, dtype=_F32) / N_HEADS)

    def gain(g, layer):
        return g[layer].reshape(1, D_MODEL)

    for layer in range(DEPTH):
        j = layer // 2
        if layer % 2 == 0:
            qkv = _qkv_na(x2d, gain(norm_mix_pre, layer), na_w_qkv[j].astype(_BF16))
            o = _na_attention(qkv, _na_rpb_cols(na_rpb[j]))
            w_o = na_w_o[j]
        else:
            qkv = _qkv_dil(x2d.reshape(BATCH, SEQ, D_MODEL), gain(norm_mix_pre, layer), dil_w_qkv[j].astype(_BF16))
            o = _dil_attention(qkv, slopes)
            w_o = dil_w_o[j]
        x2d = _mix_ffn(o, x2d, w_o.astype(_BF16), gain(norm_mix_post, layer), gain(norm_ffn_pre, layer),
                       ffn_w_gate[layer].astype(_BF16), ffn_w_up[layer].astype(_BF16),
                       ffn_w_down[layer].astype(_BF16), gain(norm_ffn_post, layer))
    return x2d.reshape(BATCH, SEQ, D_MODEL)
```

```python
import functools

import jax
import jax.numpy as jnp
from jax import lax
from jax.experimental import pallas as pl
from jax.experimental.pallas import tpu as pltpu

D_MODEL = 1024
BATCH = 8
SEQ = 2048
DEPTH = 2
N_HEADS = 16
HEAD_DIM = 64
GRID_W = 64
GRID_ROWS = SEQ // GRID_W
NA_ROWS = 8
NA_COLS = 16
DIL_GROUPS = ((128, 1), (512, 4), (2048, 16))
N_GROUPS = len(DIL_GROUPS)
BAND_BLOCK = 128
D_FF = 2816
RMS_EPS = 1e-6
NEG_INF = -1e30
LOG2E = 1.4426950408889634
QK_SCALE = HEAD_DIM ** -0.5 * LOG2E

LANES = 128
N_PAIRS = D_MODEL // LANES
N_TOK = BATCH * SEQ
TM = 512
FF_SPLITS =((0, 1536), (1536, 1280))
DIL_TN = 512
CAST_COLS = 512
SUBLANES = 8
MXU_DEPTH = 256
NA_ROWS_PER_ITER = 8
DIL_BLOCKS_PER_ITER = 8
DIL_WINDOW_OFFSETS = (0, BAND_BLOCK // 2, BAND_BLOCK)
VMEM_LIMIT = 56 * 1024 * 1024

_BF16 = jnp.bfloat16
_F32 = jnp.float32


def _rms(x, g):
    ms = jnp.mean(x * x, axis=-1, keepdims=True)
    return (x * lax.rsqrt(ms + RMS_EPS)) * g


def _resident(shape):
    return pl.BlockSpec(shape, lambda *_: (0,) * len(shape), pipeline_mode=pl.Buffered(1))


def _params(*sem):
    return pltpu.CompilerParams(dimension_semantics=sem, vmem_limit_bytes=VMEM_LIMIT)


def _qkv_na_kernel(x_ref, g_ref, w_ref, o_ref, wbf_scr):
    @pl.when(pl.program_id(0) == 0)
    def _():
        for c in range(3 * D_MODEL // CAST_COLS):
            cols = slice(c * CAST_COLS, (c + 1) * CAST_COLS)
            wbf_scr[:, cols] = w_ref[:, cols].astype(_BF16)

    h = _rms(x_ref[...], g_ref[...]).astype(_BF16)
    for j in range(3):
        y = jnp.dot(h, wbf_scr[:, j * D_MODEL:(j + 1) * D_MODEL], preferred_element_type=_F32)
        if j == 0:
            y = y * QK_SCALE
        for c in range(N_PAIRS):
            o_ref[0, j * N_PAIRS + c] = y[:, c * LANES:(c + 1) * LANES].astype(_BF16)


def _qkv_na(x2d, gain, w):
    tiles_per_seq = SEQ // TM
    return pl.pallas_call(
        _qkv_na_kernel,
        grid=(N_TOK // TM,),
        in_specs=[
            pl.BlockSpec((TM, D_MODEL), lambda i: (i, 0)),
            _resident((1, D_MODEL)),
            _resident((D_MODEL, 3 * D_MODEL)),
        ],
        out_specs=pl.BlockSpec((1, 3 * N_PAIRS, TM, LANES),
                               lambda i: (i // tiles_per_seq, 0, i % tiles_per_seq, 0)),
        out_shape=jax.ShapeDtypeStruct((BATCH, 3 * N_PAIRS, SEQ, LANES), _BF16),
        scratch_shapes=[pltpu.VMEM((D_MODEL, 3 * D_MODEL), _BF16)],
        compiler_params=_params("arbitrary"),
        name="qkv_na",
    )(x2d, gain, w)


def _stack_heads(q, first_head):
    zero = jnp.zeros_like(q)
    return jnp.concatenate([jnp.where(first_head, q, zero), jnp.where(first_head, zero, q)], axis=0)


def _masked_probs(s, bias, cap):
    s = jnp.minimum(s + bias, cap)
    m = jnp.max(s, axis=-1, keepdims=True)
    return jnp.exp2(s - m).astype(_BF16), m


def _stacked_pv(p, v, first_head):
    rows = p.shape[0] // 2
    r = jnp.dot(p, jnp.concatenate([v, jnp.ones_like(v)], axis=1), preferred_element_type=_F32)
    num = jnp.where(first_head, r[:rows, :LANES], r[rows:, :LANES])
    den = jnp.where(first_head, r[:rows, LANES:], r[rows:, LANES:])
    return num * (1.0 / den), den


def _na_kernel(q_ref, k_ref, v_ref, rpb_ref, o_ref, bias_scr, cap_scr, p_scr):
    kh = NA_ROWS
    nk = kh * GRID_W
    per_iter = NA_ROWS_PER_ITER
    n_iter = GRID_ROWS // per_iter
    lane = lax.broadcasted_iota(jnp.int32, (GRID_W, LANES), 1)
    first_head = lane < HEAD_DIM

    @pl.when(pl.program_id(1) == 0)
    def _():
        n_row_off, n_col_off = 2 * kh - 1, 2 * NA_COLS - 1
        col_off = jnp.clip(lax.broadcasted_iota(jnp.int32, (GRID_W, LANES), 1) % GRID_W
                           - lax.broadcasted_iota(jnp.int32, (GRID_W, LANES), 0) + NA_COLS - 1, 0, n_col_off - 1)
        for hh in range(2):
            for row_off in range(n_row_off):
                base = ((2 * pl.program_id(0) + hh) * n_row_off + row_off) * n_col_off
                tile = jnp.zeros((GRID_W, LANES), _F32)
                for c in range(n_col_off):
                    tile = jnp.where(col_off == c, rpb_ref[base + c], tile)
                for d in range(kh):
                    r = row_off - (kh - 1 - d)
                    if 0 <= r < kh:
                        half = slice((r % 2) * GRID_W, (r % 2 + 1) * GRID_W)
                        bias_scr[d, hh * GRID_W:(hh + 1) * GRID_W, r * GRID_W:(r + 1) * GRID_W] = tile[:, half]
        qcol = lax.broadcasted_iota(jnp.int32, (2 * GRID_W, LANES), 0) % GRID_W
        kcol = lax.broadcasted_iota(jnp.int32, (2 * GRID_W, LANES), 1) % GRID_W
        col_start = jnp.clip(qcol - NA_COLS // 2, 0, GRID_W - NA_COLS)
        col_ok = (kcol >= col_start) & (kcol < col_start + NA_COLS)
        cap_scr[...] = jnp.where(col_ok, jnp.finfo(_F32).max, NEG_INF).astype(_F32)

    def window(i):
        rs = jnp.clip(i - kh // 2, 0, GRID_ROWS - kh)
        return rs, pl.ds(pl.multiple_of(rs * GRID_W, GRID_W), nk)

    def query_rows(i):
        return pl.ds(pl.multiple_of(i * GRID_W, GRID_W), GRID_W)

    def scores(slot, u, i):
        rs, win = window(i)
        s = lax.dot_general(_stack_heads(q_ref[0, 0, query_rows(i), :], first_head), k_ref[0, 0, win, :],
                            (((1,), (1,)), ((), ())), preferred_element_type=_F32)
        cap = jnp.concatenate([cap_scr[...]] * (nk // LANES), axis=1)
        p_scr[slot, u], _ = _masked_probs(s, bias_scr[i - rs], cap)

    def outputs(slot, u, i):
        _, win = window(i)
        o, _ = _stacked_pv(p_scr[slot, u], v_ref[0, 0, win, :], first_head)
        o_ref[0, 0, query_rows(i), :] = o.astype(_BF16)

    for u in range(per_iter):
        scores(0, u, u)

    def body(it, carry):
        slot = it % 2
        for u in range(per_iter):
            outputs(slot, u, it * per_iter + u)
            scores(1 - slot, u, (it + 1) * per_iter + u)
        return carry

    lax.fori_loop(0, n_iter - 1, body, 0)
    for u in range(per_iter):
        outputs((n_iter - 1) % 2, u, (n_iter - 1) * per_iter + u)


def _na_attention(qkv, rpb_flat):
    slab = (1, 1, SEQ, LANES)
    nk = NA_ROWS * GRID_W
    return pl.pallas_call(
        _na_kernel,
        grid=(N_PAIRS, BATCH),
        in_specs=[
            pl.BlockSpec(slab, lambda p, b: (b, p, 0, 0)),
            pl.BlockSpec(slab, lambda p, b: (b, N_PAIRS + p, 0, 0)),
            pl.BlockSpec(slab, lambda p, b: (b, 2 * N_PAIRS + p, 0, 0)),
            pl.BlockSpec(memory_space=pltpu.SMEM),
        ],
        out_specs=pl.BlockSpec(slab, lambda p, b: (b, p, 0, 0)),
        out_shape=jax.ShapeDtypeStruct((BATCH, N_PAIRS, SEQ, LANES), _BF16),
        scratch_shapes=[pltpu.VMEM((NA_ROWS, 2 * GRID_W, nk), _F32),
                        pltpu.VMEM((2 * GRID_W, LANES), _F32),
                        pltpu.VMEM((2, NA_ROWS_PER_ITER, 2 * GRID_W, nk), _BF16)],
        compiler_params=_params("arbitrary", "arbitrary"),
        name="na_attention",
    )(qkv, qkv, qkv, rpb_flat)


def _mix_ffn_kernel(o_ref, x_ref, wo_ref, gmix_ref, gpre_ref, wg_ref, wu_ref, wd_ref, gpost_ref, out_ref):
    o = jnp.concatenate([o_ref[0, c] for c in range(N_PAIRS)], axis=-1)
    y = jnp.dot(o, wo_ref[...], preferred_element_type=_F32)
    x = x_ref[...] + _rms(y, gmix_ref[...])
    h = _rms(x, gpre_ref[...]).astype(_BF16)
    y = None
    for start, size in FF_SPLITS:
        g = jnp.dot(h, wg_ref[:, start:start + size], preferred_element_type=_F32)
        u = jnp.dot(h, wu_ref[:, start:start + size], preferred_element_type=_F32)
        a = ((g * jax.nn.sigmoid(g)) * u).astype(_BF16)
        part = jnp.dot(a, wd_ref[start:start + size, :], preferred_element_type=_F32)
        y = part if y is None else y + part
    out_ref[...] = x + _rms(y, gpost_ref[...])


def _mix_ffn(o_cbm, x2d, wo, gmix, gpre, wg, wu, wd, gpost):
    tiles_per_seq = SEQ // TM
    return pl.pallas_call(
        _mix_ffn_kernel,
        grid=(N_TOK // TM,),
        in_specs=[
            pl.BlockSpec((1, N_PAIRS, TM, LANES), lambda i: (i // tiles_per_seq, 0, i % tiles_per_seq, 0)),
            pl.BlockSpec((TM, D_MODEL), lambda i: (i, 0)),
            _resident((D_MODEL, D_MODEL)),
            _resident((1, D_MODEL)),
            _resident((1, D_MODEL)),
            _resident((D_MODEL, D_FF)),
            _resident((D_MODEL, D_FF)),
            _resident((D_FF, D_MODEL)),
            _resident((1, D_MODEL)),
        ],
        out_specs=pl.BlockSpec((TM, D_MODEL), lambda i: (i, 0)),
        out_shape=jax.ShapeDtypeStruct((N_TOK, D_MODEL), _F32),
        compiler_params=_params("parallel"),
        name="mix_ffn",
    )(o_cbm, x2d, wo, gmix, gpre, wg, wu, wd, gpost)


def _qkv_dil_kernel(x_ref, g_ref, w_ref, o_ref, h32_scr, hperm_scr):
    j = pl.program_id(1)
    tiles_per_group = 3 * D_MODEL // DIL_TN

    @pl.when(j == 0)
    def _():
        for t in range(SEQ // TM):
            rows = slice(t * TM, (t + 1) * TM)
            h = _rms(x_ref[0, rows, :], g_ref[...])
            for c in range(N_PAIRS):
                h32_scr[c, rows, :] = h[:, c * LANES:(c + 1) * LANES]

    for grp, (_, dil) in enumerate(DIL_GROUPS):
        @pl.when(j == grp * tiles_per_group)
        def _(dil=dil):
            sub = SEQ // dil
            if dil <= SUBLANES:
                for r in range(dil):
                    for c in range(N_PAIRS):
                        val = h32_scr[c] if dil == 1 else h32_scr[c, pl.ds(r, sub, stride=dil), :]
                        hperm_scr[r * sub:(r + 1) * sub, c * LANES:(c + 1) * LANES] = val.astype(_BF16)
            else:
                per = MXU_DEPTH // dil
                out_row = lax.broadcasted_iota(jnp.int32, (MXU_DEPTH, MXU_DEPTH), 0)
                in_row = lax.broadcasted_iota(jnp.int32, (MXU_DEPTH, MXU_DEPTH), 1)
                pick = (in_row == dil * (out_row % per) + out_row // per).astype(_BF16)
                for a in range(SEQ // MXU_DEPTH):
                    rows = slice(a * MXU_DEPTH, (a + 1) * MXU_DEPTH)
                    chunk = jnp.concatenate([h32_scr[c, rows, :].astype(_BF16) for c in range(N_PAIRS)], axis=1)
                    picked = jnp.dot(pick, chunk, preferred_element_type=_F32).astype(_BF16)
                    for r in range(dil):
                        hperm_scr[r * sub + a * per:r * sub + (a + 1) * per, :] = picked[r * per:(r + 1) * per]

    y = jnp.dot(hperm_scr[...], w_ref[...].astype(_BF16), preferred_element_type=_F32)
    is_q_tile = j % tiles_per_group < D_MODEL // DIL_TN
    y = y * jnp.where(is_q_tile, QK_SCALE, 1.0)
    for c in range(DIL_TN // LANES):
        o_ref[0, c] = y[:, c * LANES:(c + 1) * LANES].astype(_BF16)


def _qkv_dil(x3d, gain, w):
    n_col = N_GROUPS * 3 * D_MODEL
    blocks_per_tile = DIL_TN // LANES
    return pl.pallas_call(
        _qkv_dil_kernel,
        grid=(BATCH, n_col // DIL_TN),
        in_specs=[
            pl.BlockSpec((1, SEQ, D_MODEL), lambda b, j: (b, 0, 0)),
            _resident((1, D_MODEL)),
            pl.BlockSpec((D_MODEL, DIL_TN), lambda b, j: (0, j)),
        ],
        out_specs=pl.BlockSpec((1, blocks_per_tile, SEQ, LANES), lambda b, j: (b, j, 0, 0)),
        out_shape=jax.ShapeDtypeStruct((BATCH, n_col // LANES, SEQ, LANES), _BF16),
        scratch_shapes=[pltpu.VMEM((N_PAIRS, SEQ, LANES), _F32), pltpu.VMEM((SEQ, D_MODEL), _BF16)],
        compiler_params=_params("parallel", "arbitrary"),
        name="qkv_dil",
    )(x3d, gain, w)


def _dil_kernel(slopes_ref, q0, k0, v0, q1, k1, v1, q2, k2, v2, o_ref, og_scr, lg_scr, bias_scr, cap_scr, p_scr,
                m_scr):
    pair = pl.program_id(0)
    qkv_refs = ((q0, k0, v0), (q1, k1, v1), (q2, k2, v2))
    lane = lax.broadcasted_iota(jnp.int32, (BAND_BLOCK, LANES), 1)
    first_head = lane < HEAD_DIM
    max_width = 2 * BAND_BLOCK
    per_iter = DIL_BLOCKS_PER_ITER
    n_iter = SEQ // (BAND_BLOCK * per_iter)

    @pl.when(pl.program_id(1) == 0)
    def _():
        rel = (lax.broadcasted_iota(jnp.int32, (BAND_BLOCK, max_width), 0)
               - lax.broadcasted_iota(jnp.int32, (BAND_BLOCK, max_width), 1))
        for grp, (window, dil) in enumerate(DIL_GROUPS):
            radius = window // (2 * dil)
            for var, off in enumerate(DIL_WINDOW_OFFSETS):
                dist = jnp.abs(rel + off)
                distf = dist.astype(_F32)
                cap_scr[grp, var] = jnp.where(dist <= radius, jnp.finfo(_F32).max, NEG_INF).astype(_F32)
                for hh in range(2):
                    head_rows = slice(hh * BAND_BLOCK, (hh + 1) * BAND_BLOCK)
                    bias_scr[grp, var, head_rows, :] = -((slopes_ref[2 * pair + hh] * dil) * distf) * LOG2E

    def stage_fns(grp):
        window, dil = DIL_GROUPS[grp]
        q_ref, k_ref, v_ref = qkv_refs[grp]
        radius = window // (2 * dil)
        sub = SEQ // dil
        blocks_per_sub = sub // BAND_BLOCK
        width = min(max_width, sub)

        def place(t):
            res = t // blocks_per_sub
            c = t % blocks_per_sub
            kstart = jnp.clip(c * BAND_BLOCK - radius, 0, sub - width)
            return res, c, pl.ds(pl.multiple_of(res * sub + kstart, radius), width)

        def scores(slot, u, t):
            _, c, key_rows = place(t)
            q = q_ref[0, 0, pl.ds(pl.multiple_of(t * BAND_BLOCK, BAND_BLOCK), BAND_BLOCK), :]
            s = lax.dot_general(_stack_heads(q, first_head), k_ref[0, 0, key_rows, :], (((1,), (1,)), ((), ())),
                                preferred_element_type=_F32)
            var = 0 if blocks_per_sub == 1 else jnp.int32(c > 0) + jnp.int32(c == blocks_per_sub - 1)
            cap = cap_scr[grp, var, :, :width]
            p, m = _masked_probs(s, bias_scr[grp, var, :, :width], jnp.concatenate([cap, cap], axis=0))
            p_scr[slot, u, :, :width] = p
            m_scr[slot, u] = jnp.where(first_head, m[:BAND_BLOCK], m[BAND_BLOCK:])

        def outputs(slot, u, t):
            res, c, key_rows = place(t)
            o, den = _stacked_pv(p_scr[slot, u, :, :width], v_ref[0, 0, key_rows, :], first_head)
            if dil == 1:
                rows = pl.ds(pl.multiple_of(t * BAND_BLOCK, BAND_BLOCK), BAND_BLOCK)
            else:
                rows = pl.ds(c * (BAND_BLOCK * dil) + res, BAND_BLOCK, stride=dil)
            og_scr[grp, rows, :] = o
            lg_scr[grp, rows, :] = m_scr[slot, u] + jnp.log2(den)

        return scores, outputs

    fns = [stage_fns(grp) for grp in range(N_GROUPS)]
    for u in range(per_iter):
        fns[0][0](0, u, u)
    for grp in range(N_GROUPS):
        scores, outputs = fns[grp]

        def body(it, carry, scores=scores, outputs=outputs):
            slot = it % 2
            for u in range(per_iter):
                outputs(slot, u, it * per_iter + u)
                scores(1 - slot, u, (it + 1) * per_iter + u)
            return carry

        lax.fori_loop(0, n_iter - 1, body, 0)
        last_slot = (n_iter - 1) % 2
        for u in range(per_iter):
            outputs(last_slot, u, (n_iter - 1) * per_iter + u)
            if grp + 1 < N_GROUPS:
                fns[grp + 1][0](1 - last_slot, u, u)

    chunk = 256
    for t in range(SEQ // chunk):
        rows = slice(t * chunk, (t + 1) * chunk)
        lse = [lg_scr[grp, rows, :] for grp in range(N_GROUPS)]
        mx = jnp.maximum(jnp.maximum(lse[0], lse[1]), lse[2])
        e = [jnp.exp2(x - mx) for x in lse]
        den = e[0] + e[1] + e[2]
        num = e[0] * og_scr[0, rows, :] + e[1] * og_scr[1, rows, :] + e[2] * og_scr[2, rows, :]
        o_ref[0, 0, rows, :] = (num / den).astype(_BF16)


def _dil_attention(qkv, slopes):
    slab = (1, 1, SEQ, LANES)

    def spec(grp, which):
        base = (grp * 3 + which) * N_PAIRS
        return pl.BlockSpec(slab, lambda p, b: (b, base + p, 0, 0))

    return pl.pallas_call(
        _dil_kernel,
        grid=(N_PAIRS, BATCH),
        in_specs=[pl.BlockSpec(memory_space=pltpu.SMEM)]
        + [spec(grp, which) for grp in range(N_GROUPS) for which in range(3)],
        out_specs=pl.BlockSpec(slab, lambda p, b: (b, p, 0, 0)),
        out_shape=jax.ShapeDtypeStruct((BATCH, N_PAIRS, SEQ, LANES), _BF16),
        scratch_shapes=[
            pltpu.VMEM((N_GROUPS, SEQ, LANES), _F32),
            pltpu.VMEM((N_GROUPS, SEQ, LANES), _F32),
            pltpu.VMEM((N_GROUPS, len(DIL_WINDOW_OFFSETS), 2 * BAND_BLOCK, 2 * BAND_BLOCK), _F32),
            pltpu.VMEM((N_GROUPS, len(DIL_WINDOW_OFFSETS), BAND_BLOCK, 2 * BAND_BLOCK), _F32),
            pltpu.VMEM((2, DIL_BLOCKS_PER_ITER, 2 * BAND_BLOCK, 2 * BAND_BLOCK), _BF16),
            pltpu.VMEM((2, DIL_BLOCKS_PER_ITER, BAND_BLOCK, LANES), _F32),
        ],
        compiler_params=_params("arbitrary", "arbitrary"),
        name="dil_attention",
    )(slopes, *([qkv] * 9))


def kernel(x, norm_mix_pre, norm_mix_post, norm_ffn_pre, norm_ffn_post, na_w_qkv, na_w_o, na_rpb, dil_w_qkv, dil_w_o,
           ffn_w_gate, ffn_w_up, ffn_w_down):
    x2d = x.reshape(N_TOK, D_MODEL)
    slopes = 2.0 ** (-8.0 * jnp.arange(1, N_HEADS + 1, dtype=_F32) / N_HEADS)

    def gain(g, layer):
        return g[layer].reshape(1, D_MODEL)

    for layer in range(DEPTH):
        j = layer // 2
        if layer % 2 == 0:
            qkv = _qkv_na(x2d, gain(norm_mix_pre, layer), na_w_qkv[j])
            o = _na_attention(qkv, (na_rpb[j].astype(_F32) * LOG2E).reshape(-1))
            w_o = na_w_o[j]
        else:
            qkv = _qkv_dil(x2d.reshape(BATCH, SEQ, D_MODEL), gain(norm_mix_pre, layer), dil_w_qkv[j])
            o = _dil_attention(qkv, slopes)
            w_o = dil_w_o[j]
        x2d = _mix_ffn(o, x2d, w_o.astype(_BF16), gain(norm_mix_post, layer), gain(norm_ffn_pre, layer),
                       ffn_w_gate[layer].astype(_BF16), ffn_w_up[layer].astype(_BF16),
                       ffn_w_down[layer].astype(_BF16), gain(norm_ffn_post, layer))
    return x2d.reshape(BATCH, SEQ, D_MODEL)
```

```python
import functools

import jax
import jax.numpy as jnp
from jax import lax
from jax.experimental import pallas as pl
from jax.experimental.pallas import tpu as pltpu

D_MODEL = 1024
BATCH = 8
SEQ = 2048
DEPTH = 2
N_HEADS = 16
HEAD_DIM = 64
GRID_W = 64
GRID_ROWS = SEQ // GRID_W
NA_ROWS = 8
NA_COLS = 16
DIL_GROUPS = ((128, 1), (512, 4), (2048, 16))
N_GROUPS = len(DIL_GROUPS)
BAND_BLOCK = 128
D_FF = 2816
RMS_EPS = 1e-6
NEG_INF = -1e30
LOG2E = 1.4426950408889634
QK_SCALE = HEAD_DIM ** -0.5 * LOG2E

LANES = 128
N_PAIRS = D_MODEL // LANES
N_TOK = BATCH * SEQ
TM = 512
FF_SPLITS =((0, 1536), (1536, 1280))
DIL_TN = 1024
CAST_COLS = 512
WEIGHT_STEPS = 8
SUBLANES = 8
MXU_DEPTH = 256
NA_ROWS_PER_ITER = 8
DIL_BLOCKS_PER_ITER = 8
DIL_WINDOW_OFFSETS = (0, BAND_BLOCK // 2, BAND_BLOCK)
VMEM_LIMIT = 56 * 1024 * 1024

_BF16 = jnp.bfloat16
_F32 = jnp.float32


def _rms(x, g):
    ms = jnp.mean(x * x, axis=-1, keepdims=True)
    return (x * lax.rsqrt(ms + RMS_EPS)) * g


def _resident(shape):
    return pl.BlockSpec(shape, lambda *_: (0,) * len(shape), pipeline_mode=pl.Buffered(1))


def _params(*sem):
    return pltpu.CompilerParams(dimension_semantics=sem, vmem_limit_bytes=VMEM_LIMIT)


def _qkv_na_kernel(x_ref, g_ref, w_ref, o_ref, wbf_scr):
    @pl.when(pl.program_id(0) == 0)
    def _():
        for c in range(3 * D_MODEL // CAST_COLS):
            cols = slice(c * CAST_COLS, (c + 1) * CAST_COLS)
            wbf_scr[:, cols] = w_ref[:, cols].astype(_BF16)

    h = _rms(x_ref[...], g_ref[...]).astype(_BF16)
    for j in range(3):
        y = jnp.dot(h, wbf_scr[:, j * D_MODEL:(j + 1) * D_MODEL], preferred_element_type=_F32)
        if j == 0:
            y = y * QK_SCALE
        for c in range(N_PAIRS):
            o_ref[0, j * N_PAIRS + c] = y[:, c * LANES:(c + 1) * LANES].astype(_BF16)


def _qkv_na(x2d, gain, w):
    tiles_per_seq = SEQ // TM
    return pl.pallas_call(
        _qkv_na_kernel,
        grid=(N_TOK // TM,),
        in_specs=[
            pl.BlockSpec((TM, D_MODEL), lambda i: (i, 0)),
            _resident((1, D_MODEL)),
            _resident((D_MODEL, 3 * D_MODEL)),
        ],
        out_specs=pl.BlockSpec((1, 3 * N_PAIRS, TM, LANES),
                               lambda i: (i // tiles_per_seq, 0, i % tiles_per_seq, 0)),
        out_shape=jax.ShapeDtypeStruct((BATCH, 3 * N_PAIRS, SEQ, LANES), _BF16),
        scratch_shapes=[pltpu.VMEM((D_MODEL, 3 * D_MODEL), _BF16)],
        compiler_params=_params("arbitrary"),
        name="qkv_na",
    )(x2d, gain, w)


def _stack_heads(q, first_head):
    zero = jnp.zeros_like(q)
    return jnp.concatenate([jnp.where(first_head, q, zero), jnp.where(first_head, zero, q)], axis=0)


def _masked_probs(s, bias, cap):
    s = jnp.minimum(s + bias, cap)
    m = jnp.max(s, axis=-1, keepdims=True)
    return jnp.exp2(s - m).astype(_BF16), m


def _stacked_pv(p, v, first_head):
    rows = p.shape[0] // 2
    r = jnp.dot(p, jnp.concatenate([v, jnp.ones_like(v)], axis=1), preferred_element_type=_F32)
    num = jnp.where(first_head, r[:rows, :LANES], r[rows:, :LANES])
    den = jnp.where(first_head, r[:rows, LANES:], r[rows:, LANES:])
    return num * (1.0 / den), den


def _na_kernel(q_ref, k_ref, v_ref, rpb_ref, o_ref, bias_scr, cap_scr, p_scr):
    kh = NA_ROWS
    nk = kh * GRID_W
    per_iter = NA_ROWS_PER_ITER
    n_iter = GRID_ROWS // per_iter
    lane = lax.broadcasted_iota(jnp.int32, (GRID_W, LANES), 1)
    first_head = lane < HEAD_DIM

    @pl.when(pl.program_id(1) == 0)
    def _():
        n_row_off, n_col_off = 2 * kh - 1, 2 * NA_COLS - 1
        col_off = jnp.clip(lax.broadcasted_iota(jnp.int32, (GRID_W, LANES), 1) % GRID_W
                           - lax.broadcasted_iota(jnp.int32, (GRID_W, LANES), 0) + NA_COLS - 1, 0, n_col_off - 1)
        for hh in range(2):
            for row_off in range(n_row_off):
                base = ((2 * pl.program_id(0) + hh) * n_row_off + row_off) * n_col_off
                tile = jnp.zeros((GRID_W, LANES), _F32)
                for c in range(n_col_off):
                    tile = jnp.where(col_off == c, rpb_ref[base + c], tile)
                for d in range(kh):
                    r = row_off - (kh - 1 - d)
                    if 0 <= r < kh:
                        half = slice((r % 2) * GRID_W, (r % 2 + 1) * GRID_W)
                        bias_scr[d, hh * GRID_W:(hh + 1) * GRID_W, r * GRID_W:(r + 1) * GRID_W] = tile[:, half]
        qcol = lax.broadcasted_iota(jnp.int32, (2 * GRID_W, LANES), 0) % GRID_W
        kcol = lax.broadcasted_iota(jnp.int32, (2 * GRID_W, LANES), 1) % GRID_W
        col_start = jnp.clip(qcol - NA_COLS // 2, 0, GRID_W - NA_COLS)
        col_ok = (kcol >= col_start) & (kcol < col_start + NA_COLS)
        cap_scr[...] = jnp.where(col_ok, jnp.finfo(_F32).max, NEG_INF).astype(_F32)

    def window(i):
        rs = jnp.clip(i - kh // 2, 0, GRID_ROWS - kh)
        return rs, pl.ds(pl.multiple_of(rs * GRID_W, GRID_W), nk)

    def query_rows(i):
        return pl.ds(pl.multiple_of(i * GRID_W, GRID_W), GRID_W)

    def scores(slot, u, i):
        rs, win = window(i)
        s = lax.dot_general(_stack_heads(q_ref[0, 0, query_rows(i), :], first_head), k_ref[0, 0, win, :],
                            (((1,), (1,)), ((), ())), preferred_element_type=_F32)
        cap = jnp.concatenate([cap_scr[...]] * (nk // LANES), axis=1)
        p_scr[slot, u], _ = _masked_probs(s, bias_scr[i - rs], cap)

    def outputs(slot, u, i):
        _, win = window(i)
        o, _ = _stacked_pv(p_scr[slot, u], v_ref[0, 0, win, :], first_head)
        o_ref[0, 0, query_rows(i), :] = o.astype(_BF16)

    for u in range(per_iter):
        scores(0, u, u)

    def body(it, carry):
        slot = it % 2
        for u in range(per_iter):
            outputs(slot, u, it * per_iter + u)
            scores(1 - slot, u, (it + 1) * per_iter + u)
        return carry

    lax.fori_loop(0, n_iter - 1, body, 0)
    for u in range(per_iter):
        outputs((n_iter - 1) % 2, u, (n_iter - 1) * per_iter + u)


def _na_attention(qkv, rpb_flat):
    slab = (1, 1, SEQ, LANES)
    nk = NA_ROWS * GRID_W
    return pl.pallas_call(
        _na_kernel,
        grid=(N_PAIRS, BATCH),
        in_specs=[
            pl.BlockSpec(slab, lambda p, b: (b, p, 0, 0)),
            pl.BlockSpec(slab, lambda p, b: (b, N_PAIRS + p, 0, 0)),
            pl.BlockSpec(slab, lambda p, b: (b, 2 * N_PAIRS + p, 0, 0)),
            pl.BlockSpec(memory_space=pltpu.SMEM),
        ],
        out_specs=pl.BlockSpec(slab, lambda p, b: (b, p, 0, 0)),
        out_shape=jax.ShapeDtypeStruct((BATCH, N_PAIRS, SEQ, LANES), _BF16),
        scratch_shapes=[pltpu.VMEM((NA_ROWS, 2 * GRID_W, nk), _F32),
                        pltpu.VMEM((2 * GRID_W, LANES), _F32),
                        pltpu.VMEM((2, NA_ROWS_PER_ITER, 2 * GRID_W, nk), _BF16)],
        compiler_params=_params("arbitrary", "arbitrary"),
        name="na_attention",
    )(qkv, qkv, qkv, rpb_flat)


def _mix_ffn_kernel(o_ref, x_ref, wo_ref, gmix_ref, gpre_ref, wg_ref, wu_ref, wd_ref, gpost_ref, out_ref,
                    wo_scr, wg_scr, wu_scr, wd_scr):
    i = pl.program_id(0)

    @pl.when(i < WEIGHT_STEPS)
    def _():
        for src, dst in ((wo_ref, wo_scr), (wg_ref, wg_scr), (wu_ref, wu_scr), (wd_ref, wd_scr)):
            n = src.shape[0]
            dst[pl.ds(pl.multiple_of(i * n, n), n), :] = src[...].astype(_BF16)

    @pl.when(i >= WEIGHT_STEPS)
    def _():
        o = jnp.concatenate([o_ref[0, c] for c in range(N_PAIRS)], axis=-1)
        y = jnp.dot(o, wo_scr[...], preferred_element_type=_F32)
        x = x_ref[...] + _rms(y, gmix_ref[...])
        h = _rms(x, gpre_ref[...]).astype(_BF16)
        y = None
        for start, size in FF_SPLITS:
            g = jnp.dot(h, wg_scr[:, start:start + size], preferred_element_type=_F32)
            u = jnp.dot(h, wu_scr[:, start:start + size], preferred_element_type=_F32)
            a = ((g * jax.nn.sigmoid(g)) * u).astype(_BF16)
            part = jnp.dot(a, wd_scr[start:start + size, :], preferred_element_type=_F32)
            y = part if y is None else y + part
        out_ref[...] = x + _rms(y, gpost_ref[...])


def _mix_ffn(o_cbm, x2d, wo, gmix, gpre, wg, wu, wd, gpost):
    tiles_per_seq = SEQ // TM

    def tile(i):
        return jnp.maximum(i - WEIGHT_STEPS, 0)

    def weight_chunk(rows, cols):
        return pl.BlockSpec((rows // WEIGHT_STEPS, cols), lambda i: (jnp.minimum(i, WEIGHT_STEPS - 1), 0))

    return pl.pallas_call(
        _mix_ffn_kernel,
        grid=(WEIGHT_STEPS + N_TOK // TM,),
        in_specs=[
            pl.BlockSpec((1, N_PAIRS, TM, LANES), lambda i: (tile(i) // tiles_per_seq, 0, tile(i) % tiles_per_seq, 0)),
            pl.BlockSpec((TM, D_MODEL), lambda i: (tile(i), 0)),
            weight_chunk(D_MODEL, D_MODEL),
            _resident((1, D_MODEL)),
            _resident((1, D_MODEL)),
            weight_chunk(D_MODEL, D_FF),
            weight_chunk(D_MODEL, D_FF),
            weight_chunk(D_FF, D_MODEL),
            _resident((1, D_MODEL)),
        ],
        out_specs=pl.BlockSpec((TM, D_MODEL), lambda i: (tile(i), 0)),
        out_shape=jax.ShapeDtypeStruct((N_TOK, D_MODEL), _F32),
        scratch_shapes=[pltpu.VMEM((D_MODEL, D_MODEL), _BF16), pltpu.VMEM((D_MODEL, D_FF), _BF16),
                        pltpu.VMEM((D_MODEL, D_FF), _BF16), pltpu.VMEM((D_FF, D_MODEL), _BF16)],
        compiler_params=_params("arbitrary"),
        name="mix_ffn",
    )(o_cbm, x2d, wo, gmix, gpre, wg, wu, wd, gpost)


def _qkv_dil_kernel(x_ref, g_ref, w_ref, o_ref, h32_scr, hperm_scr):
    j = pl.program_id(1)
    tiles_per_group = 3 * D_MODEL // DIL_TN

    @pl.when(j == 0)
    def _():
        for t in range(SEQ // TM):
            rows = slice(t * TM, (t + 1) * TM)
            h = _rms(x_ref[0, rows, :], g_ref[...])
            for c in range(N_PAIRS):
                h32_scr[c, rows, :] = h[:, c * LANES:(c + 1) * LANES]

    for grp, (_, dil) in enumerate(DIL_GROUPS):
        @pl.when(j == grp * tiles_per_group)
        def _(dil=dil):
            sub = SEQ // dil
            if dil <= SUBLANES:
                for r in range(dil):
                    for c in range(N_PAIRS):
                        val = h32_scr[c] if dil == 1 else h32_scr[c, pl.ds(r, sub, stride=dil), :]
                        hperm_scr[r * sub:(r + 1) * sub, c * LANES:(c + 1) * LANES] = val.astype(_BF16)
            else:
                per = MXU_DEPTH // dil
                out_row = lax.broadcasted_iota(jnp.int32, (MXU_DEPTH, MXU_DEPTH), 0)
                in_row = lax.broadcasted_iota(jnp.int32, (MXU_DEPTH, MXU_DEPTH), 1)
                pick = (in_row == dil * (out_row % per) + out_row // per).astype(_BF16)
                for a in range(SEQ // MXU_DEPTH):
                    rows = slice(a * MXU_DEPTH, (a + 1) * MXU_DEPTH)
                    chunk = jnp.concatenate([h32_scr[c, rows, :].astype(_BF16) for c in range(N_PAIRS)], axis=1)
                    picked = jnp.dot(pick, chunk, preferred_element_type=_F32).astype(_BF16)
                    for r in range(dil):
                        hperm_scr[r * sub + a * per:r * sub + (a + 1) * per, :] = picked[r * per:(r + 1) * per]

    y = jnp.dot(hperm_scr[...], w_ref[...].astype(_BF16), preferred_element_type=_F32)
    is_q_tile = j % tiles_per_group < D_MODEL // DIL_TN
    y = y * jnp.where(is_q_tile, QK_SCALE, 1.0)
    for c in range(DIL_TN // LANES):
        o_ref[0, c] = y[:, c * LANES:(c + 1) * LANES].astype(_BF16)


def _qkv_dil(x3d, gain, w):
    n_col = N_GROUPS * 3 * D_MODEL
    blocks_per_tile = DIL_TN // LANES
    return pl.pallas_call(
        _qkv_dil_kernel,
        grid=(BATCH, n_col // DIL_TN),
        in_specs=[
            pl.BlockSpec((1, SEQ, D_MODEL), lambda b, j: (b, 0, 0)),
            _resident((1, D_MODEL)),
            pl.BlockSpec((D_MODEL, DIL_TN), lambda b, j: (0, j)),
        ],
        out_specs=pl.BlockSpec((1, blocks_per_tile, SEQ, LANES), lambda b, j: (b, j, 0, 0)),
        out_shape=jax.ShapeDtypeStruct((BATCH, n_col // LANES, SEQ, LANES), _BF16),
        scratch_shapes=[pltpu.VMEM((N_PAIRS, SEQ, LANES), _F32), pltpu.VMEM((SEQ, D_MODEL), _BF16)],
        compiler_params=_params("parallel", "arbitrary"),
        name="qkv_dil",
    )(x3d, gain, w)


def _dil_kernel(slopes_ref, q0, k0, v0, q1, k1, v1, q2, k2, v2, o_ref, og_scr, lg_scr, bias_scr, cap_scr, p_scr,
                m_scr):
    pair = pl.program_id(0)
    qkv_refs = ((q0, k0, v0), (q1, k1, v1), (q2, k2, v2))
    lane = lax.broadcasted_iota(jnp.int32, (BAND_BLOCK, LANES), 1)
    first_head = lane < HEAD_DIM
    max_width = 2 * BAND_BLOCK
    per_iter = DIL_BLOCKS_PER_ITER
    n_iter = SEQ // (BAND_BLOCK * per_iter)

    @pl.when(pl.program_id(1) == 0)
    def _():
        rel = (lax.broadcasted_iota(jnp.int32, (BAND_BLOCK, max_width), 0)
               - lax.broadcasted_iota(jnp.int32, (BAND_BLOCK, max_width), 1))
        for grp, (window, dil) in enumerate(DIL_GROUPS):
            radius = window // (2 * dil)
            for var, off in enumerate(DIL_WINDOW_OFFSETS):
                dist = jnp.abs(rel + off)
                distf = dist.astype(_F32)
                cap_scr[grp, var] = jnp.where(dist <= radius, jnp.finfo(_F32).max, NEG_INF).astype(_F32)
                for hh in range(2):
                    head_rows = slice(hh * BAND_BLOCK, (hh + 1) * BAND_BLOCK)
                    bias_scr[grp, var, head_rows, :] = -((slopes_ref[2 * pair + hh] * dil) * distf) * LOG2E

    def stage_fns(grp):
        window, dil = DIL_GROUPS[grp]
        q_ref, k_ref, v_ref = qkv_refs[grp]
        radius = window // (2 * dil)
        sub = SEQ // dil
        blocks_per_sub = sub // BAND_BLOCK
        width = min(max_width, sub)

        def place(t):
            res = t // blocks_per_sub
            c = t % blocks_per_sub
            kstart = jnp.clip(c * BAND_BLOCK - radius, 0, sub - width)
            return res, c, pl.ds(pl.multiple_of(res * sub + kstart, radius), width)

        def scores(slot, u, t):
            _, c, key_rows = place(t)
            q = q_ref[0, 0, pl.ds(pl.multiple_of(t * BAND_BLOCK, BAND_BLOCK), BAND_BLOCK), :]
            s = lax.dot_general(_stack_heads(q, first_head), k_ref[0, 0, key_rows, :], (((1,), (1,)), ((), ())),
                                preferred_element_type=_F32)
            var = 0 if blocks_per_sub == 1 else jnp.int32(c > 0) + jnp.int32(c == blocks_per_sub - 1)
            cap = cap_scr[grp, var, :, :width]
            p, m = _masked_probs(s, bias_scr[grp, var, :, :width], jnp.concatenate([cap, cap], axis=0))
            p_scr[slot, u, :, :width] = p
            m_scr[slot, u] = jnp.where(first_head, m[:BAND_BLOCK], m[BAND_BLOCK:])

        def outputs(slot, u, t):
            res, c, key_rows = place(t)
            o, den = _stacked_pv(p_scr[slot, u, :, :width], v_ref[0, 0, key_rows, :], first_head)
            if dil == 1:
                rows = pl.ds(pl.multiple_of(t * BAND_BLOCK, BAND_BLOCK), BAND_BLOCK)
            else:
                rows = pl.ds(c * (BAND_BLOCK * dil) + res, BAND_BLOCK, stride=dil)
            og_scr[grp, rows, :] = o
            lg_scr[grp, rows, :] = m_scr[slot, u] + jnp.log2(den)

        return scores, outputs

    fns = [stage_fns(grp) for grp in range(N_GROUPS)]
    for u in range(per_iter):
        fns[0][0](0, u, u)
    for grp in range(N_GROUPS):
        scores, outputs = fns[grp]

        def body(it, carry, scores=scores, outputs=outputs):
            slot = it % 2
            for u in range(per_iter):
                outputs(slot, u, it * per_iter + u)
                scores(1 - slot, u, (it + 1) * per_iter + u)
            return carry

        lax.fori_loop(0, n_iter - 1, body, 0)
        last_slot = (n_iter - 1) % 2
        for u in range(per_iter):
            outputs(last_slot, u, (n_iter - 1) * per_iter + u)
            if grp + 1 < N_GROUPS:
                fns[grp + 1][0](1 - last_slot, u, u)

    chunk = 256
    for t in range(SEQ // chunk):
        rows = slice(t * chunk, (t + 1) * chunk)
        lse = [lg_scr[grp, rows, :] for grp in range(N_GROUPS)]
        mx = jnp.maximum(jnp.maximum(lse[0], lse[1]), lse[2])
        e = [jnp.exp2(x - mx) for x in lse]
        den = e[0] + e[1] + e[2]
        num = e[0] * og_scr[0, rows, :] + e[1] * og_scr[1, rows, :] + e[2] * og_scr[2, rows, :]
        o_ref[0, 0, rows, :] = (num / den).astype(_BF16)


def _dil_attention(qkv, slopes):
    slab = (1, 1, SEQ, LANES)

    def spec(grp, which):
        base = (grp * 3 + which) * N_PAIRS
        return pl.BlockSpec(slab, lambda p, b: (b, base + p, 0, 0))

    return pl.pallas_call(
        _dil_kernel,
        grid=(N_PAIRS, BATCH),
        in_specs=[pl.BlockSpec(memory_space=pltpu.SMEM)]
        + [spec(grp, which) for grp in range(N_GROUPS) for which in range(3)],
        out_specs=pl.BlockSpec(slab, lambda p, b: (b, p, 0, 0)),
        out_shape=jax.ShapeDtypeStruct((BATCH, N_PAIRS, SEQ, LANES), _BF16),
        scratch_shapes=[
            pltpu.VMEM((N_GROUPS, SEQ, LANES), _F32),
            pltpu.VMEM((N_GROUPS, SEQ, LANES), _F32),
            pltpu.VMEM((N_GROUPS, len(DIL_WINDOW_OFFSETS), 2 * BAND_BLOCK, 2 * BAND_BLOCK), _F32),
            pltpu.VMEM((N_GROUPS, len(DIL_WINDOW_OFFSETS), BAND_BLOCK, 2 * BAND_BLOCK), _F32),
            pltpu.VMEM((2, DIL_BLOCKS_PER_ITER, 2 * BAND_BLOCK, 2 * BAND_BLOCK), _BF16),
            pltpu.VMEM((2, DIL_BLOCKS_PER_ITER, BAND_BLOCK, LANES), _F32),
        ],
        compiler_params=_params("arbitrary", "arbitrary"),
        name="dil_attention",
    )(slopes, *([qkv] * 9))


def kernel(x, norm_mix_pre, norm_mix_post, norm_ffn_pre, norm_ffn_post, na_w_qkv, na_w_o, na_rpb, dil_w_qkv, dil_w_o,
           ffn_w_gate, ffn_w_up, ffn_w_down):
    x2d = x.reshape(N_TOK, D_MODEL)
    slopes = 2.0 ** (-8.0 * jnp.arange(1, N_HEADS + 1, dtype=_F32) / N_HEADS)

    def gain(g, layer):
        return g[layer].reshape(1, D_MODEL)

    for layer in range(DEPTH):
        j = layer // 2
        if layer % 2 == 0:
            qkv = _qkv_na(x2d, gain(norm_mix_pre, layer), na_w_qkv[j])
            o = _na_attention(qkv, (na_rpb[j].astype(_F32) * LOG2E).reshape(-1))
            w_o = na_w_o[j]
        else:
            qkv = _qkv_dil(x2d.reshape(BATCH, SEQ, D_MODEL), gain(norm_mix_pre, layer), dil_w_qkv[j])
            o = _dil_attention(qkv, slopes)
            w_o = dil_w_o[j]
        x2d = _mix_ffn(o, x2d, w_o, gain(norm_mix_post, layer), gain(norm_ffn_pre, layer),
                       ffn_w_gate[layer], ffn_w_up[layer], ffn_w_down[layer], gain(norm_ffn_post, layer))
    return x2d.reshape(BATCH, SEQ, D_MODEL)
```

```python
import functools

import jax
import jax.numpy as jnp
from jax import lax
from jax.experimental import pallas as pl
from jax.experimental.pallas import tpu as pltpu

D_MODEL = 1024
BATCH = 8
SEQ = 2048
DEPTH = 2
N_HEADS = 16
HEAD_DIM = 64
GRID_W = 64
GRID_ROWS = SEQ // GRID_W
NA_ROWS = 8
NA_COLS = 16
DIL_GROUPS = ((128, 1), (512, 4), (2048, 16))
N_GROUPS = len(DIL_GROUPS)
BAND_BLOCK = 128
D_FF = 2816
RMS_EPS = 1e-6
NEG_INF = -1e30
LOG2E = 1.4426950408889634
QK_SCALE = HEAD_DIM ** -0.5 * LOG2E

LANES = 128
N_PAIRS = D_MODEL // LANES
N_TOK = BATCH * SEQ
TM = 512
FF_SPLITS =((0, 1536), (1536, 1280))
DIL_TN = 1024
CAST_COLS = 512
WEIGHT_STEPS = 8
SUBLANES = 8
MXU_DEPTH = 256
NA_ROWS_PER_ITER = 16
DIL_BLOCKS_PER_ITER = 8
DIL_WINDOW_OFFSETS = (0, BAND_BLOCK // 2, BAND_BLOCK)
VMEM_LIMIT = 56 * 1024 * 1024

_BF16 = jnp.bfloat16
_F32 = jnp.float32


def _rms(x, g):
    ms = jnp.mean(x * x, axis=-1, keepdims=True)
    return (x * lax.rsqrt(ms + RMS_EPS)) * g


def _resident(shape):
    return pl.BlockSpec(shape, lambda *_: (0,) * len(shape), pipeline_mode=pl.Buffered(1))


def _params(*sem):
    return pltpu.CompilerParams(dimension_semantics=sem, vmem_limit_bytes=VMEM_LIMIT)


def _qkv_na_kernel(x_ref, g_ref, w_ref, o_ref, wbf_scr):
    @pl.when(pl.program_id(0) == 0)
    def _():
        for c in range(3 * D_MODEL // CAST_COLS):
            cols = slice(c * CAST_COLS, (c + 1) * CAST_COLS)
            wbf_scr[:, cols] = w_ref[:, cols].astype(_BF16)

    h = _rms(x_ref[...], g_ref[...]).astype(_BF16)
    for j in range(3):
        y = jnp.dot(h, wbf_scr[:, j * D_MODEL:(j + 1) * D_MODEL], preferred_element_type=_F32)
        if j == 0:
            y = y * QK_SCALE
        for c in range(N_PAIRS):
            o_ref[0, j * N_PAIRS + c] = y[:, c * LANES:(c + 1) * LANES].astype(_BF16)


def _qkv_na(x2d, gain, w):
    tiles_per_seq = SEQ // TM
    return pl.pallas_call(
        _qkv_na_kernel,
        grid=(N_TOK // TM,),
        in_specs=[
            pl.BlockSpec((TM, D_MODEL), lambda i: (i, 0)),
            _resident((1, D_MODEL)),
            _resident((D_MODEL, 3 * D_MODEL)),
        ],
        out_specs=pl.BlockSpec((1, 3 * N_PAIRS, TM, LANES),
                               lambda i: (i // tiles_per_seq, 0, i % tiles_per_seq, 0)),
        out_shape=jax.ShapeDtypeStruct((BATCH, 3 * N_PAIRS, SEQ, LANES), _BF16),
        scratch_shapes=[pltpu.VMEM((D_MODEL, 3 * D_MODEL), _BF16)],
        compiler_params=_params("arbitrary"),
        name="qkv_na",
    )(x2d, gain, w)


def _stack_heads(q, first_head):
    zero = jnp.zeros_like(q)
    return jnp.concatenate([jnp.where(first_head, q, zero), jnp.where(first_head, zero, q)], axis=0)


def _masked_probs(s, bias, cap):
    s = jnp.minimum(s + bias, cap)
    m = jnp.max(s, axis=-1, keepdims=True)
    return jnp.exp2(s - m).astype(_BF16), m


def _stacked_pv(p, v, first_head):
    rows = p.shape[0] // 2
    r = jnp.dot(p, jnp.concatenate([v, jnp.ones_like(v)], axis=1), preferred_element_type=_F32)
    num = jnp.where(first_head, r[:rows, :LANES], r[rows:, :LANES])
    den = jnp.where(first_head, r[:rows, LANES:], r[rows:, LANES:])
    return num * (1.0 / den), den


def _na_kernel(q_ref, k_ref, v_ref, rpb_ref, o_ref, bias_scr, cap_scr, p_scr):
    kh = NA_ROWS
    nk = kh * GRID_W
    per_iter = NA_ROWS_PER_ITER
    n_iter = GRID_ROWS // per_iter
    lane = lax.broadcasted_iota(jnp.int32, (GRID_W, LANES), 1)
    first_head = lane < HEAD_DIM

    @pl.when(pl.program_id(1) == 0)
    def _():
        n_row_off, n_col_off = 2 * kh - 1, 2 * NA_COLS - 1
        col_off = jnp.clip(lax.broadcasted_iota(jnp.int32, (GRID_W, LANES), 1) % GRID_W
                           - lax.broadcasted_iota(jnp.int32, (GRID_W, LANES), 0) + NA_COLS - 1, 0, n_col_off - 1)
        for hh in range(2):
            for row_off in range(n_row_off):
                base = ((2 * pl.program_id(0) + hh) * n_row_off + row_off) * n_col_off
                tile = jnp.zeros((GRID_W, LANES), _F32)
                for c in range(n_col_off):
                    tile = jnp.where(col_off == c, rpb_ref[base + c], tile)
                for d in range(kh):
                    r = row_off - (kh - 1 - d)
                    if 0 <= r < kh:
                        half = slice((r % 2) * GRID_W, (r % 2 + 1) * GRID_W)
                        bias_scr[d, hh * GRID_W:(hh + 1) * GRID_W, r * GRID_W:(r + 1) * GRID_W] = tile[:, half]
        qcol = lax.broadcasted_iota(jnp.int32, (2 * GRID_W, LANES), 0) % GRID_W
        kcol = lax.broadcasted_iota(jnp.int32, (2 * GRID_W, LANES), 1) % GRID_W
        col_start = jnp.clip(qcol - NA_COLS // 2, 0, GRID_W - NA_COLS)
        col_ok = (kcol >= col_start) & (kcol < col_start + NA_COLS)
        cap_scr[...] = jnp.where(col_ok, jnp.finfo(_F32).max, NEG_INF).astype(_F32)

    def window(i):
        rs = jnp.clip(i - kh // 2, 0, GRID_ROWS - kh)
        return rs, pl.ds(pl.multiple_of(rs * GRID_W, GRID_W), nk)

    def query_rows(i):
        return pl.ds(pl.multiple_of(i * GRID_W, GRID_W), GRID_W)

    def scores(slot, u, i):
        rs, win = window(i)
        s = lax.dot_general(_stack_heads(q_ref[0, 0, query_rows(i), :], first_head), k_ref[0, 0, win, :],
                            (((1,), (1,)), ((), ())), preferred_element_type=_F32)
        cap = jnp.concatenate([cap_scr[...]] * (nk // LANES), axis=1)
        p_scr[slot, u], _ = _masked_probs(s, bias_scr[i - rs], cap)

    def outputs(slot, u, i):
        _, win = window(i)
        o, _ = _stacked_pv(p_scr[slot, u], v_ref[0, 0, win, :], first_head)
        o_ref[0, 0, query_rows(i), :] = o.astype(_BF16)

    for u in range(per_iter):
        scores(0, u, u)

    def body(it, carry):
        slot = it % 2
        for u in range(per_iter):
            outputs(slot, u, it * per_iter + u)
            scores(1 - slot, u, (it + 1) * per_iter + u)
        return carry

    lax.fori_loop(0, n_iter - 1, body, 0)
    for u in range(per_iter):
        outputs((n_iter - 1) % 2, u, (n_iter - 1) * per_iter + u)


def _na_attention(qkv, rpb_flat):
    slab = (1, 1, SEQ, LANES)
    nk = NA_ROWS * GRID_W
    return pl.pallas_call(
        _na_kernel,
        grid=(N_PAIRS, BATCH),
        in_specs=[
            pl.BlockSpec(slab, lambda p, b: (b, p, 0, 0)),
            pl.BlockSpec(slab, lambda p, b: (b, N_PAIRS + p, 0, 0)),
            pl.BlockSpec(slab, lambda p, b: (b, 2 * N_PAIRS + p, 0, 0)),
            pl.BlockSpec(memory_space=pltpu.SMEM),
        ],
        out_specs=pl.BlockSpec(slab, lambda p, b: (b, p, 0, 0)),
        out_shape=jax.ShapeDtypeStruct((BATCH, N_PAIRS, SEQ, LANES), _BF16),
        scratch_shapes=[pltpu.VMEM((NA_ROWS, 2 * GRID_W, nk), _F32),
                        pltpu.VMEM((2 * GRID_W, LANES), _F32),
                        pltpu.VMEM((2, NA_ROWS_PER_ITER, 2 * GRID_W, nk), _BF16)],
        compiler_params=_params("arbitrary", "arbitrary"),
        name="na_attention",
    )(qkv, qkv, qkv, rpb_flat)


def _mix_ffn_kernel(o_ref, x_ref, wo_ref, gmix_ref, gpre_ref, wg_ref, wu_ref, wd_ref, gpost_ref, out_ref,
                    wo_scr, wg_scr, wu_scr, wd_scr):
    i = pl.program_id(0)

    @pl.when(i < WEIGHT_STEPS)
    def _():
        for src, dst in ((wo_ref, wo_scr), (wg_ref, wg_scr), (wu_ref, wu_scr), (wd_ref, wd_scr)):
            n = src.shape[0]
            dst[pl.ds(pl.multiple_of(i * n, n), n), :] = src[...].astype(_BF16)

    @pl.when(i >= WEIGHT_STEPS)
    def _():
        o = jnp.concatenate([o_ref[0, c] for c in range(N_PAIRS)], axis=-1)
        y = jnp.dot(o, wo_scr[...], preferred_element_type=_F32)
        x = x_ref[...] + _rms(y, gmix_ref[...])
        h = _rms(x, gpre_ref[...]).astype(_BF16)
        y = None
        for start, size in FF_SPLITS:
            g = jnp.dot(h, wg_scr[:, start:start + size], preferred_element_type=_F32)
            u = jnp.dot(h, wu_scr[:, start:start + size], preferred_element_type=_F32)
            a = ((g * jax.nn.sigmoid(g)) * u).astype(_BF16)
            part = jnp.dot(a, wd_scr[start:start + size, :], preferred_element_type=_F32)
            y = part if y is None else y + part
        out_ref[...] = x + _rms(y, gpost_ref[...])


def _mix_ffn(layer, o_cbm, x2d, wo, gmix, gpre, wg, wu, wd, gpost):
    tiles_per_seq = SEQ // TM

    def tile(i):
        return jnp.maximum(i - WEIGHT_STEPS, 0)

    def chunk(i):
        return jnp.minimum(i, WEIGHT_STEPS - 1)

    def weight_chunk(rows, cols):
        return pl.BlockSpec((rows // WEIGHT_STEPS, cols), lambda i: (chunk(i), 0))

    def layer_weight_chunk(rows, cols):
        return pl.BlockSpec((None, rows // WEIGHT_STEPS, cols), lambda i: (layer, chunk(i), 0))

    return pl.pallas_call(
        _mix_ffn_kernel,
        grid=(WEIGHT_STEPS + N_TOK // TM,),
        in_specs=[
            pl.BlockSpec((1, N_PAIRS, TM, LANES), lambda i: (tile(i) // tiles_per_seq, 0, tile(i) % tiles_per_seq, 0)),
            pl.BlockSpec((TM, D_MODEL), lambda i: (tile(i), 0)),
            weight_chunk(D_MODEL, D_MODEL),
            _resident((1, D_MODEL)),
            _resident((1, D_MODEL)),
            layer_weight_chunk(D_MODEL, D_FF),
            layer_weight_chunk(D_MODEL, D_FF),
            layer_weight_chunk(D_FF, D_MODEL),
            _resident((1, D_MODEL)),
        ],
        out_specs=pl.BlockSpec((TM, D_MODEL), lambda i: (tile(i), 0)),
        out_shape=jax.ShapeDtypeStruct((N_TOK, D_MODEL), _F32),
        scratch_shapes=[pltpu.VMEM((D_MODEL, D_MODEL), _BF16), pltpu.VMEM((D_MODEL, D_FF), _BF16),
                        pltpu.VMEM((D_MODEL, D_FF), _BF16), pltpu.VMEM((D_FF, D_MODEL), _BF16)],
        compiler_params=_params("arbitrary"),
        name="mix_ffn",
    )(o_cbm, x2d, wo, gmix, gpre, wg, wu, wd, gpost)


def _qkv_dil_kernel(x_ref, g_ref, w_ref, o_ref, h32_scr, hperm_scr):
    j = pl.program_id(1)
    tiles_per_group = 3 * D_MODEL // DIL_TN

    @pl.when(j == 0)
    def _():
        for t in range(SEQ // TM):
            rows = slice(t * TM, (t + 1) * TM)
            h = _rms(x_ref[0, rows, :], g_ref[...])
            for c in range(N_PAIRS):
                h32_scr[c, rows, :] = h[:, c * LANES:(c + 1) * LANES]

    for grp, (_, dil) in enumerate(DIL_GROUPS):
        @pl.when(j == grp * tiles_per_group)
        def _(dil=dil):
            sub = SEQ // dil
            if dil <= SUBLANES:
                for r in range(dil):
                    for c in range(N_PAIRS):
                        val = h32_scr[c] if dil == 1 else h32_scr[c, pl.ds(r, sub, stride=dil), :]
                        hperm_scr[r * sub:(r + 1) * sub, c * LANES:(c + 1) * LANES] = val.astype(_BF16)
            else:
                per = MXU_DEPTH // dil
                out_row = lax.broadcasted_iota(jnp.int32, (MXU_DEPTH, MXU_DEPTH), 0)
                in_row = lax.broadcasted_iota(jnp.int32, (MXU_DEPTH, MXU_DEPTH), 1)
                pick = (in_row == dil * (out_row % per) + out_row // per).astype(_BF16)
                for a in range(SEQ // MXU_DEPTH):
                    rows = slice(a * MXU_DEPTH, (a + 1) * MXU_DEPTH)
                    chunk = jnp.concatenate([h32_scr[c, rows, :].astype(_BF16) for c in range(N_PAIRS)], axis=1)
                    picked = jnp.dot(pick, chunk, preferred_element_type=_F32).astype(_BF16)
                    for r in range(dil):
                        hperm_scr[r * sub + a * per:r * sub + (a + 1) * per, :] = picked[r * per:(r + 1) * per]

    y = jnp.dot(hperm_scr[...], w_ref[...].astype(_BF16), preferred_element_type=_F32)
    is_q_tile = j % tiles_per_group < D_MODEL // DIL_TN
    y = y * jnp.where(is_q_tile, QK_SCALE, 1.0)
    for c in range(DIL_TN // LANES):
        o_ref[0, c] = y[:, c * LANES:(c + 1) * LANES].astype(_BF16)


def _qkv_dil(x3d, gain, w):
    n_col = N_GROUPS * 3 * D_MODEL
    blocks_per_tile = DIL_TN // LANES
    return pl.pallas_call(
        _qkv_dil_kernel,
        grid=(BATCH, n_col // DIL_TN),
        in_specs=[
            pl.BlockSpec((1, SEQ, D_MODEL), lambda b, j: (b, 0, 0)),
            _resident((1, D_MODEL)),
            pl.BlockSpec((D_MODEL, DIL_TN), lambda b, j: (0, j)),
        ],
        out_specs=pl.BlockSpec((1, blocks_per_tile, SEQ, LANES), lambda b, j: (b, j, 0, 0)),
        out_shape=jax.ShapeDtypeStruct((BATCH, n_col // LANES, SEQ, LANES), _BF16),
        scratch_shapes=[pltpu.VMEM((N_PAIRS, SEQ, LANES), _F32), pltpu.VMEM((SEQ, D_MODEL), _BF16)],
        compiler_params=_params("parallel", "arbitrary"),
        name="qkv_dil",
    )(x3d, gain, w)


def _dil_kernel(slopes_ref, q0, k0, v0, q1, k1, v1, q2, k2, v2, o_ref, og_scr, lg_scr, bias_scr, cap_scr, p_scr,
                m_scr):
    pair = pl.program_id(0)
    qkv_refs = ((q0, k0, v0), (q1, k1, v1), (q2, k2, v2))
    lane = lax.broadcasted_iota(jnp.int32, (BAND_BLOCK, LANES), 1)
    first_head = lane < HEAD_DIM
    max_width = 2 * BAND_BLOCK
    per_iter = DIL_BLOCKS_PER_ITER
    n_iter = SEQ // (BAND_BLOCK * per_iter)

    @pl.when(pl.program_id(1) == 0)
    def _():
        rel = (lax.broadcasted_iota(jnp.int32, (BAND_BLOCK, max_width), 0)
               - lax.broadcasted_iota(jnp.int32, (BAND_BLOCK, max_width), 1))
        for grp, (window, dil) in enumerate(DIL_GROUPS):
            radius = window // (2 * dil)
            for var, off in enumerate(DIL_WINDOW_OFFSETS):
                dist = jnp.abs(rel + off)
                distf = dist.astype(_F32)
                cap_scr[grp, var] = jnp.where(dist <= radius, jnp.finfo(_F32).max, NEG_INF).astype(_F32)
                for hh in range(2):
                    head_rows = slice(hh * BAND_BLOCK, (hh + 1) * BAND_BLOCK)
                    bias_scr[grp, var, head_rows, :] = -((slopes_ref[2 * pair + hh] * dil) * distf) * LOG2E

    def stage_fns(grp):
        window, dil = DIL_GROUPS[grp]
        q_ref, k_ref, v_ref = qkv_refs[grp]
        radius = window // (2 * dil)
        sub = SEQ // dil
        blocks_per_sub = sub // BAND_BLOCK
        width = min(max_width, sub)

        def place(t):
            res = t // blocks_per_sub
            c = t % blocks_per_sub
            kstart = jnp.clip(c * BAND_BLOCK - radius, 0, sub - width)
            return res, c, pl.ds(pl.multiple_of(res * sub + kstart, radius), width)

        def scores(slot, u, t):
            _, c, key_rows = place(t)
            q = q_ref[0, 0, pl.ds(pl.multiple_of(t * BAND_BLOCK, BAND_BLOCK), BAND_BLOCK), :]
            s = lax.dot_general(_stack_heads(q, first_head), k_ref[0, 0, key_rows, :], (((1,), (1,)), ((), ())),
                                preferred_element_type=_F32)
            var = 0 if blocks_per_sub == 1 else jnp.int32(c > 0) + jnp.int32(c == blocks_per_sub - 1)
            cap = cap_scr[grp, var, :, :width]
            p, m = _masked_probs(s, bias_scr[grp, var, :, :width], jnp.concatenate([cap, cap], axis=0))
            p_scr[slot, u, :, :width] = p
            m_scr[slot, u] = jnp.where(first_head, m[:BAND_BLOCK], m[BAND_BLOCK:])

        def outputs(slot, u, t):
            res, c, key_rows = place(t)
            o, den = _stacked_pv(p_scr[slot, u, :, :width], v_ref[0, 0, key_rows, :], first_head)
            if dil == 1:
                rows = pl.ds(pl.multiple_of(t * BAND_BLOCK, BAND_BLOCK), BAND_BLOCK)
            else:
                rows = pl.ds(c * (BAND_BLOCK * dil) + res, BAND_BLOCK, stride=dil)
            og_scr[grp, rows, :] = o
            lg_scr[grp, rows, :] = m_scr[slot, u] + jnp.log2(den)

        return scores, outputs

    fns = [stage_fns(grp) for grp in range(N_GROUPS)]
    for u in range(per_iter):
        fns[0][0](0, u, u)
    for grp in range(N_GROUPS):
        scores, outputs = fns[grp]

        def body(it, carry, scores=scores, outputs=outputs):
            slot = it % 2
            for u in range(per_iter):
                outputs(slot, u, it * per_iter + u)
                scores(1 - slot, u, (it + 1) * per_iter + u)
            return carry

        lax.fori_loop(0, n_iter - 1, body, 0)
        last_slot = (n_iter - 1) % 2
        for u in range(per_iter):
            outputs(last_slot, u, (n_iter - 1) * per_iter + u)
            if grp + 1 < N_GROUPS:
                fns[grp + 1][0](1 - last_slot, u, u)

    chunk = 256
    for t in range(SEQ // chunk):
        rows = slice(t * chunk, (t + 1) * chunk)
        lse = [lg_scr[grp, rows, :] for grp in range(N_GROUPS)]
        mx = jnp.maximum(jnp.maximum(lse[0], lse[1]), lse[2])
        e = [jnp.exp2(x - mx) for x in lse]
        den = e[0] + e[1] + e[2]
        num = e[0] * og_scr[0, rows, :] + e[1] * og_scr[1, rows, :] + e[2] * og_scr[2, rows, :]
        o_ref[0, 0, rows, :] = (num / den).astype(_BF16)


def _dil_attention(qkv, slopes):
    slab = (1, 1, SEQ, LANES)

    def spec(grp, which):
        base = (grp * 3 + which) * N_PAIRS
        return pl.BlockSpec(slab, lambda p, b: (b, base + p, 0, 0))

    return pl.pallas_call(
        _dil_kernel,
        grid=(N_PAIRS, BATCH),
        in_specs=[pl.BlockSpec(memory_space=pltpu.SMEM)]
        + [spec(grp, which) for grp in range(N_GROUPS) for which in range(3)],
        out_specs=pl.BlockSpec(slab, lambda p, b: (b, p, 0, 0)),
        out_shape=jax.ShapeDtypeStruct((BATCH, N_PAIRS, SEQ, LANES), _BF16),
        scratch_shapes=[
            pltpu.VMEM((N_GROUPS, SEQ, LANES), _F32),
            pltpu.VMEM((N_GROUPS, SEQ, LANES), _F32),
            pltpu.VMEM((N_GROUPS, len(DIL_WINDOW_OFFSETS), 2 * BAND_BLOCK, 2 * BAND_BLOCK), _F32),
            pltpu.VMEM((N_GROUPS, len(DIL_WINDOW_OFFSETS), BAND_BLOCK, 2 * BAND_BLOCK), _F32),
            pltpu.VMEM((2, DIL_BLOCKS_PER_ITER, 2 * BAND_BLOCK, 2 * BAND_BLOCK), _BF16),
            pltpu.VMEM((2, DIL_BLOCKS_PER_ITER, BAND_BLOCK, LANES), _F32),
        ],
        compiler_params=_params("arbitrary", "arbitrary"),
        name="dil_attention",
    )(slopes, *([qkv] * 9))


def kernel(x, norm_mix_pre, norm_mix_post, norm_ffn_pre, norm_ffn_post, na_w_qkv, na_w_o, na_rpb, dil_w_qkv, dil_w_o,
           ffn_w_gate, ffn_w_up, ffn_w_down):
    x2d = x.reshape(N_TOK, D_MODEL)
    slopes = 2.0 ** (-8.0 * jnp.arange(1, N_HEADS + 1, dtype=_F32) / N_HEADS)

    def gain(g, layer):
        return g[layer].reshape(1, D_MODEL)

    for layer in range(DEPTH):
        j = layer // 2
        if layer % 2 == 0:
            qkv = _qkv_na(x2d, gain(norm_mix_pre, layer), na_w_qkv[j])
            o = _na_attention(qkv, (na_rpb[j].astype(_F32) * LOG2E).reshape(-1))
            w_o = na_w_o[j]
        else:
            qkv = _qkv_dil(x2d.reshape(BATCH, SEQ, D_MODEL), gain(norm_mix_pre, layer), dil_w_qkv[j])
            o = _dil_attention(qkv, slopes)
            w_o = dil_w_o[j]
        x2d = _mix_ffn(layer, o, x2d, w_o, gain(norm_mix_post, layer), gain(norm_ffn_pre, layer),
                       ffn_w_gate, ffn_w_up, ffn_w_down, gain(norm_ffn_post, layer))
    return x2d.reshape(BATCH, SEQ, D_MODEL)
```

```python
import functools

import jax
import jax.numpy as jnp
from jax import lax
from jax.experimental import pallas as pl
from jax.experimental.pallas import tpu as pltpu

D_MODEL = 1024
BATCH = 8
SEQ = 2048
DEPTH = 2
N_HEADS = 16
HEAD_DIM = 64
GRID_W = 64
GRID_ROWS = SEQ // GRID_W
NA_ROWS = 8
NA_COLS = 16
DIL_GROUPS = ((128, 1), (512, 4), (2048, 16))
N_GROUPS = len(DIL_GROUPS)
BAND_BLOCK = 128
D_FF = 2816
RMS_EPS = 1e-6
NEG_INF = -1e30
LOG2E = 1.4426950408889634
QK_SCALE = HEAD_DIM ** -0.5 * LOG2E

LANES = 128
N_PAIRS = D_MODEL // LANES
N_TOK = BATCH * SEQ
TM = 512
FF_SPLITS =((0, 1536), (1536, 1280))
DIL_TN = 1024
CAST_COLS = 512
WEIGHT_STEPS = 8
SUBLANES = 8
MXU_DEPTH = 256
NA_ROWS_PER_ITER = 16
DIL_BLOCKS_PER_ITER = 8
DIL_WINDOW_OFFSETS = (0, BAND_BLOCK // 2, BAND_BLOCK)
VMEM_LIMIT = 56 * 1024 * 1024

_BF16 = jnp.bfloat16
_F32 = jnp.float32


def _rms(x, g):
    ms = jnp.mean(x * x, axis=-1, keepdims=True)
    return (x * lax.rsqrt(ms + RMS_EPS)) * g


def _resident(shape):
    return pl.BlockSpec(shape, lambda *_: (0,) * len(shape), pipeline_mode=pl.Buffered(1))


def _params(*sem):
    return pltpu.CompilerParams(dimension_semantics=sem, vmem_limit_bytes=VMEM_LIMIT)


def _qkv_na_kernel(x_ref, g_ref, w_ref, o_ref, wbf_scr):
    @pl.when(pl.program_id(0) == 0)
    def _():
        for c in range(3 * D_MODEL // CAST_COLS):
            cols = slice(c * CAST_COLS, (c + 1) * CAST_COLS)
            wbf_scr[:, cols] = w_ref[:, cols].astype(_BF16)

    h = _rms(x_ref[...], g_ref[...]).astype(_BF16)
    for j in range(3):
        y = jnp.dot(h, wbf_scr[:, j * D_MODEL:(j + 1) * D_MODEL], preferred_element_type=_F32)
        if j == 0:
            y = y * QK_SCALE
        for c in range(N_PAIRS):
            o_ref[0, j * N_PAIRS + c] = y[:, c * LANES:(c + 1) * LANES].astype(_BF16)


def _qkv_na(x2d, gain, w):
    tiles_per_seq = SEQ // TM
    return pl.pallas_call(
        _qkv_na_kernel,
        grid=(N_TOK // TM,),
        in_specs=[
            pl.BlockSpec((TM, D_MODEL), lambda i: (i, 0)),
            _resident((1, D_MODEL)),
            _resident((D_MODEL, 3 * D_MODEL)),
        ],
        out_specs=pl.BlockSpec((1, 3 * N_PAIRS, TM, LANES),
                               lambda i: (i // tiles_per_seq, 0, i % tiles_per_seq, 0)),
        out_shape=jax.ShapeDtypeStruct((BATCH, 3 * N_PAIRS, SEQ, LANES), _BF16),
        scratch_shapes=[pltpu.VMEM((D_MODEL, 3 * D_MODEL), _BF16)],
        compiler_params=_params("arbitrary"),
        name="qkv_na",
    )(x2d, gain, w)


def _stack_heads(q, first_head):
    zero = jnp.zeros_like(q)
    return jnp.concatenate([jnp.where(first_head, q, zero), jnp.where(first_head, zero, q)], axis=0)


def _masked_probs(s, bias, cap):
    s = jnp.minimum(s + bias, cap)
    m = jnp.max(s, axis=-1, keepdims=True)
    return jnp.exp2((s - m).astype(_BF16)), m


def _stacked_pv(p, v, first_head):
    rows = p.shape[0] // 2
    r = jnp.dot(p, jnp.concatenate([v, jnp.ones_like(v)], axis=1), preferred_element_type=_F32)
    num = jnp.where(first_head, r[:rows, :LANES], r[rows:, :LANES])
    den = jnp.where(first_head, r[:rows, LANES:], r[rows:, LANES:])
    return num * (1.0 / den), den


def _na_kernel(q_ref, k_ref, v_ref, rpb_ref, o_ref, bias_scr, cap_scr, p_scr):
    kh = NA_ROWS
    nk = kh * GRID_W
    per_iter = NA_ROWS_PER_ITER
    n_iter = GRID_ROWS // per_iter
    lane = lax.broadcasted_iota(jnp.int32, (GRID_W, LANES), 1)
    first_head = lane < HEAD_DIM

    @pl.when(pl.program_id(1) == 0)
    def _():
        n_row_off, n_col_off = 2 * kh - 1, 2 * NA_COLS - 1
        col_off = jnp.clip(lax.broadcasted_iota(jnp.int32, (GRID_W, LANES), 1) % GRID_W
                           - lax.broadcasted_iota(jnp.int32, (GRID_W, LANES), 0) + NA_COLS - 1, 0, n_col_off - 1)
        for hh in range(2):
            for row_off in range(n_row_off):
                base = ((2 * pl.program_id(0) + hh) * n_row_off + row_off) * n_col_off
                tile = jnp.zeros((GRID_W, LANES), _F32)
                for c in range(n_col_off):
                    tile = jnp.where(col_off == c, rpb_ref[base + c], tile)
                for d in range(kh):
                    r = row_off - (kh - 1 - d)
                    if 0 <= r < kh:
                        half = slice((r % 2) * GRID_W, (r % 2 + 1) * GRID_W)
                        bias_scr[d, hh * GRID_W:(hh + 1) * GRID_W, r * GRID_W:(r + 1) * GRID_W] = tile[:, half]
        qcol = lax.broadcasted_iota(jnp.int32, (2 * GRID_W, LANES), 0) % GRID_W
        kcol = lax.broadcasted_iota(jnp.int32, (2 * GRID_W, LANES), 1) % GRID_W
        col_start = jnp.clip(qcol - NA_COLS // 2, 0, GRID_W - NA_COLS)
        col_ok = (kcol >= col_start) & (kcol < col_start + NA_COLS)
        cap_scr[...] = jnp.where(col_ok, jnp.finfo(_F32).max, NEG_INF).astype(_F32)

    def window(i):
        rs = jnp.clip(i - kh // 2, 0, GRID_ROWS - kh)
        return rs, pl.ds(pl.multiple_of(rs * GRID_W, GRID_W), nk)

    def query_rows(i):
        return pl.ds(pl.multiple_of(i * GRID_W, GRID_W), GRID_W)

    def scores(slot, u, i):
        rs, win = window(i)
        s = lax.dot_general(_stack_heads(q_ref[0, 0, query_rows(i), :], first_head), k_ref[0, 0, win, :],
                            (((1,), (1,)), ((), ())), preferred_element_type=_F32)
        cap = jnp.concatenate([cap_scr[...]] * (nk // LANES), axis=1)
        p_scr[slot, u], _ = _masked_probs(s, bias_scr[i - rs], cap)

    def outputs(slot, u, i):
        _, win = window(i)
        o, _ = _stacked_pv(p_scr[slot, u], v_ref[0, 0, win, :], first_head)
        o_ref[0, 0, query_rows(i), :] = o.astype(_BF16)

    for u in range(per_iter):
        scores(0, u, u)

    def body(it, carry):
        slot = it % 2
        for u in range(per_iter):
            outputs(slot, u, it * per_iter + u)
            scores(1 - slot, u, (it + 1) * per_iter + u)
        return carry

    lax.fori_loop(0, n_iter - 1, body, 0)
    for u in range(per_iter):
        outputs((n_iter - 1) % 2, u, (n_iter - 1) * per_iter + u)


def _na_attention(qkv, rpb_flat):
    slab = (1, 1, SEQ, LANES)
    nk = NA_ROWS * GRID_W
    return pl.pallas_call(
        _na_kernel,
        grid=(N_PAIRS, BATCH),
        in_specs=[
            pl.BlockSpec(slab, lambda p, b: (b, p, 0, 0)),
            pl.BlockSpec(slab, lambda p, b: (b, N_PAIRS + p, 0, 0)),
            pl.BlockSpec(slab, lambda p, b: (b, 2 * N_PAIRS + p, 0, 0)),
            pl.BlockSpec(memory_space=pltpu.SMEM),
        ],
        out_specs=pl.BlockSpec(slab, lambda p, b: (b, p, 0, 0)),
        out_shape=jax.ShapeDtypeStruct((BATCH, N_PAIRS, SEQ, LANES), _BF16),
        scratch_shapes=[pltpu.VMEM((NA_ROWS, 2 * GRID_W, nk), _F32),
                        pltpu.VMEM((2 * GRID_W, LANES), _F32),
                        pltpu.VMEM((2, NA_ROWS_PER_ITER, 2 * GRID_W, nk), _BF16)],
        compiler_params=_params("arbitrary", "arbitrary"),
        name="na_attention",
    )(qkv, qkv, qkv, rpb_flat)


def _mix_ffn_kernel(o_ref, x_ref, wo_ref, gmix_ref, gpre_ref, wg_ref, wu_ref, wd_ref, gpost_ref, out_ref,
                    wo_scr, wg_scr, wu_scr, wd_scr):
    i = pl.program_id(0)

    @pl.when(i < WEIGHT_STEPS)
    def _():
        for src, dst in ((wo_ref, wo_scr), (wg_ref, wg_scr), (wu_ref, wu_scr), (wd_ref, wd_scr)):
            n = src.shape[0]
            dst[pl.ds(pl.multiple_of(i * n, n), n), :] = src[...].astype(_BF16)

    @pl.when(i >= WEIGHT_STEPS)
    def _():
        o = jnp.concatenate([o_ref[0, c] for c in range(N_PAIRS)], axis=-1)
        y = jnp.dot(o, wo_scr[...], preferred_element_type=_F32)
        x = x_ref[...] + _rms(y, gmix_ref[...])
        h = _rms(x, gpre_ref[...]).astype(_BF16)
        y = None
        for start, size in FF_SPLITS:
            g = jnp.dot(h, wg_scr[:, start:start + size], preferred_element_type=_F32)
            u = jnp.dot(h, wu_scr[:, start:start + size], preferred_element_type=_F32)
            a = ((g * jax.nn.sigmoid(g)) * u).astype(_BF16)
            part = jnp.dot(a, wd_scr[start:start + size, :], preferred_element_type=_F32)
            y = part if y is None else y + part
        out_ref[...] = x + _rms(y, gpost_ref[...])


def _mix_ffn(layer, o_cbm, x2d, wo, gmix, gpre, wg, wu, wd, gpost):
    tiles_per_seq = SEQ // TM

    def tile(i):
        return jnp.maximum(i - WEIGHT_STEPS, 0)

    def chunk(i):
        return jnp.minimum(i, WEIGHT_STEPS - 1)

    def weight_chunk(rows, cols):
        return pl.BlockSpec((rows // WEIGHT_STEPS, cols), lambda i: (chunk(i), 0))

    def layer_weight_chunk(rows, cols):
        return pl.BlockSpec((None, rows // WEIGHT_STEPS, cols), lambda i: (layer, chunk(i), 0))

    return pl.pallas_call(
        _mix_ffn_kernel,
        grid=(WEIGHT_STEPS + N_TOK // TM,),
        in_specs=[
            pl.BlockSpec((1, N_PAIRS, TM, LANES), lambda i: (tile(i) // tiles_per_seq, 0, tile(i) % tiles_per_seq, 0)),
            pl.BlockSpec((TM, D_MODEL), lambda i: (tile(i), 0)),
            weight_chunk(D_MODEL, D_MODEL),
            _resident((1, D_MODEL)),
            _resident((1, D_MODEL)),
            layer_weight_chunk(D_MODEL, D_FF),
            layer_weight_chunk(D_MODEL, D_FF),
            layer_weight_chunk(D_FF, D_MODEL),
            _resident((1, D_MODEL)),
        ],
        out_specs=pl.BlockSpec((TM, D_MODEL), lambda i: (tile(i), 0)),
        out_shape=jax.ShapeDtypeStruct((N_TOK, D_MODEL), _F32),
        scratch_shapes=[pltpu.VMEM((D_MODEL, D_MODEL), _BF16), pltpu.VMEM((D_MODEL, D_FF), _BF16),
                        pltpu.VMEM((D_MODEL, D_FF), _BF16), pltpu.VMEM((D_FF, D_MODEL), _BF16)],
        compiler_params=_params("arbitrary"),
        name="mix_ffn",
    )(o_cbm, x2d, wo, gmix, gpre, wg, wu, wd, gpost)


def _qkv_dil_kernel(x_ref, g_ref, w_ref, o_ref, h32_scr, hperm_scr):
    j = pl.program_id(1)
    tiles_per_group = 3 * D_MODEL // DIL_TN

    @pl.when(j == 0)
    def _():
        for t in range(SEQ // TM):
            rows = slice(t * TM, (t + 1) * TM)
            h = _rms(x_ref[0, rows, :], g_ref[...])
            for c in range(N_PAIRS):
                h32_scr[c, rows, :] = h[:, c * LANES:(c + 1) * LANES]

    for grp, (_, dil) in enumerate(DIL_GROUPS):
        @pl.when(j == grp * tiles_per_group)
        def _(dil=dil):
            sub = SEQ // dil
            if dil <= SUBLANES:
                for r in range(dil):
                    for c in range(N_PAIRS):
                        val = h32_scr[c] if dil == 1 else h32_scr[c, pl.ds(r, sub, stride=dil), :]
                        hperm_scr[r * sub:(r + 1) * sub, c * LANES:(c + 1) * LANES] = val.astype(_BF16)
            else:
                per = MXU_DEPTH // dil
                out_row = lax.broadcasted_iota(jnp.int32, (MXU_DEPTH, MXU_DEPTH), 0)
                in_row = lax.broadcasted_iota(jnp.int32, (MXU_DEPTH, MXU_DEPTH), 1)
                pick = (in_row == dil * (out_row % per) + out_row // per).astype(_BF16)
                for a in range(SEQ // MXU_DEPTH):
                    rows = slice(a * MXU_DEPTH, (a + 1) * MXU_DEPTH)
                    chunk = jnp.concatenate([h32_scr[c, rows, :].astype(_BF16) for c in range(N_PAIRS)], axis=1)
                    picked = jnp.dot(pick, chunk, preferred_element_type=_F32).astype(_BF16)
                    for r in range(dil):
                        hperm_scr[r * sub + a * per:r * sub + (a + 1) * per, :] = picked[r * per:(r + 1) * per]

    y = jnp.dot(hperm_scr[...], w_ref[...].astype(_BF16), preferred_element_type=_F32)
    is_q_tile = j % tiles_per_group < D_MODEL // DIL_TN
    y = y * jnp.where(is_q_tile, QK_SCALE, 1.0)
    for c in range(DIL_TN // LANES):
        o_ref[0, c] = y[:, c * LANES:(c + 1) * LANES].astype(_BF16)


def _qkv_dil(x3d, gain, w):
    n_col = N_GROUPS * 3 * D_MODEL
    blocks_per_tile = DIL_TN // LANES
    return pl.pallas_call(
        _qkv_dil_kernel,
        grid=(BATCH, n_col // DIL_TN),
        in_specs=[
            pl.BlockSpec((1, SEQ, D_MODEL), lambda b, j: (b, 0, 0)),
            _resident((1, D_MODEL)),
            pl.BlockSpec((D_MODEL, DIL_TN), lambda b, j: (0, j)),
        ],
        out_specs=pl.BlockSpec((1, blocks_per_tile, SEQ, LANES), lambda b, j: (b, j, 0, 0)),
        out_shape=jax.ShapeDtypeStruct((BATCH, n_col // LANES, SEQ, LANES), _BF16),
        scratch_shapes=[pltpu.VMEM((N_PAIRS, SEQ, LANES), _F32), pltpu.VMEM((SEQ, D_MODEL), _BF16)],
        compiler_params=_params("parallel", "arbitrary"),
        name="qkv_dil",
    )(x3d, gain, w)


def _dil_kernel(slopes_ref, q0, k0, v0, q1, k1, v1, q2, k2, v2, o_ref, og_scr, lg_scr, bias_scr, cap_scr, p_scr,
                m_scr):
    pair = pl.program_id(0)
    qkv_refs = ((q0, k0, v0), (q1, k1, v1), (q2, k2, v2))
    lane = lax.broadcasted_iota(jnp.int32, (BAND_BLOCK, LANES), 1)
    first_head = lane < HEAD_DIM
    max_width = 2 * BAND_BLOCK
    per_iter = DIL_BLOCKS_PER_ITER
    n_iter = SEQ // (BAND_BLOCK * per_iter)

    @pl.when(pl.program_id(1) == 0)
    def _():
        rel = (lax.broadcasted_iota(jnp.int32, (BAND_BLOCK, max_width), 0)
               - lax.broadcasted_iota(jnp.int32, (BAND_BLOCK, max_width), 1))
        for grp, (window, dil) in enumerate(DIL_GROUPS):
            radius = window // (2 * dil)
            for var, off in enumerate(DIL_WINDOW_OFFSETS):
                dist = jnp.abs(rel + off)
                distf = dist.astype(_F32)
                cap_scr[grp, var] = jnp.where(dist <= radius, jnp.finfo(_F32).max, NEG_INF).astype(_F32)
                for hh in range(2):
                    head_rows = slice(hh * BAND_BLOCK, (hh + 1) * BAND_BLOCK)
                    bias_scr[grp, var, head_rows, :] = -((slopes_ref[2 * pair + hh] * dil) * distf) * LOG2E

    def stage_fns(grp):
        window, dil = DIL_GROUPS[grp]
        q_ref, k_ref, v_ref = qkv_refs[grp]
        radius = window // (2 * dil)
        sub = SEQ // dil
        blocks_per_sub = sub // BAND_BLOCK
        width = min(max_width, sub)

        def place(t):
            res = t // blocks_per_sub
            c = t % blocks_per_sub
            kstart = jnp.clip(c * BAND_BLOCK - radius, 0, sub - width)
            return res, c, pl.ds(pl.multiple_of(res * sub + kstart, radius), width)

        def scores(slot, u, t):
            _, c, key_rows = place(t)
            q = q_ref[0, 0, pl.ds(pl.multiple_of(t * BAND_BLOCK, BAND_BLOCK), BAND_BLOCK), :]
            s = lax.dot_general(_stack_heads(q, first_head), k_ref[0, 0, key_rows, :], (((1,), (1,)), ((), ())),
                                preferred_element_type=_F32)
            var = 0 if blocks_per_sub == 1 else jnp.int32(c > 0) + jnp.int32(c == blocks_per_sub - 1)
            cap = cap_scr[grp, var, :, :width]
            p, m = _masked_probs(s, bias_scr[grp, var, :, :width], jnp.concatenate([cap, cap], axis=0))
            p_scr[slot, u, :, :width] = p
            m_scr[slot, u] = jnp.where(first_head, m[:BAND_BLOCK], m[BAND_BLOCK:])

        def outputs(slot, u, t):
            res, c, key_rows = place(t)
            o, den = _stacked_pv(p_scr[slot, u, :, :width], v_ref[0, 0, key_rows, :], first_head)
            if dil == 1:
                rows = pl.ds(pl.multiple_of(t * BAND_BLOCK, BAND_BLOCK), BAND_BLOCK)
            else:
                rows = pl.ds(c * (BAND_BLOCK * dil) + res, BAND_BLOCK, stride=dil)
            og_scr[grp, rows, :] = o
            lg_scr[grp, rows, :] = m_scr[slot, u] + jnp.log2(den)

        return scores, outputs

    fns = [stage_fns(grp) for grp in range(N_GROUPS)]
    for u in range(per_iter):
        fns[0][0](0, u, u)
    for grp in range(N_GROUPS):
        scores, outputs = fns[grp]

        def body(it, carry, scores=scores, outputs=outputs):
            slot = it % 2
            for u in range(per_iter):
                outputs(slot, u, it * per_iter + u)
                scores(1 - slot, u, (it + 1) * per_iter + u)
            return carry

        lax.fori_loop(0, n_iter - 1, body, 0)
        last_slot = (n_iter - 1) % 2
        for u in range(per_iter):
            outputs(last_slot, u, (n_iter - 1) * per_iter + u)
            if grp + 1 < N_GROUPS:
                fns[grp + 1][0](1 - last_slot, u, u)

    chunk = 256
    for t in range(SEQ // chunk):
        rows = slice(t * chunk, (t + 1) * chunk)
        lse = [lg_scr[grp, rows, :] for grp in range(N_GROUPS)]
        mx = jnp.maximum(jnp.maximum(lse[0], lse[1]), lse[2])
        e = [jnp.exp2(x - mx) for x in lse]
        den = e[0] + e[1] + e[2]
        num = e[0] * og_scr[0, rows, :] + e[1] * og_scr[1, rows, :] + e[2] * og_scr[2, rows, :]
        o_ref[0, 0, rows, :] = (num / den).astype(_BF16)


def _dil_attention(qkv, slopes):
    slab = (1, 1, SEQ, LANES)

    def spec(grp, which):
        base = (grp * 3 + which) * N_PAIRS
        return pl.BlockSpec(slab, lambda p, b: (b, base + p, 0, 0))

    return pl.pallas_call(
        _dil_kernel,
        grid=(N_PAIRS, BATCH),
        in_specs=[pl.BlockSpec(memory_space=pltpu.SMEM)]
        + [spec(grp, which) for grp in range(N_GROUPS) for which in range(3)],
        out_specs=pl.BlockSpec(slab, lambda p, b: (b, p, 0, 0)),
        out_shape=jax.ShapeDtypeStruct((BATCH, N_PAIRS, SEQ, LANES), _BF16),
        scratch_shapes=[
            pltpu.VMEM((N_GROUPS, SEQ, LANES), _F32),
            pltpu.VMEM((N_GROUPS, SEQ, LANES), _F32),
            pltpu.VMEM((N_GROUPS, len(DIL_WINDOW_OFFSETS), 2 * BAND_BLOCK, 2 * BAND_BLOCK), _F32),
            pltpu.VMEM((N_GROUPS, len(DIL_WINDOW_OFFSETS), BAND_BLOCK, 2 * BAND_BLOCK), _F32),
            pltpu.VMEM((2, DIL_BLOCKS_PER_ITER, 2 * BAND_BLOCK, 2 * BAND_BLOCK), _BF16),
            pltpu.VMEM((2, DIL_BLOCKS_PER_ITER, BAND_BLOCK, LANES), _F32),
        ],
        compiler_params=_params("arbitrary", "arbitrary"),
        name="dil_attention",
    )(slopes, *([qkv] * 9))


def kernel(x, norm_mix_pre, norm_mix_post, norm_ffn_pre, norm_ffn_post, na_w_qkv, na_w_o, na_rpb, dil_w_qkv, dil_w_o,
           ffn_w_gate, ffn_w_up, ffn_w_down):
    x2d = x.reshape(N_TOK, D_MODEL)
    slopes = 2.0 ** (-8.0 * jnp.arange(1, N_HEADS + 1, dtype=_F32) / N_HEADS)

    def gain(g, layer):
        return g[layer].reshape(1, D_MODEL)

    for layer in range(DEPTH):
        j = layer // 2
        if layer % 2 == 0:
            qkv = _qkv_na(x2d, gain(norm_mix_pre, layer), na_w_qkv[j])
            o = _na_attention(qkv, (na_rpb[j].astype(_F32) * LOG2E).reshape(-1))
            w_o = na_w_o[j]
        else:
            qkv = _qkv_dil(x2d.reshape(BATCH, SEQ, D_MODEL), gain(norm_mix_pre, layer), dil_w_qkv[j])
            o = _dil_attention(qkv, slopes)
            w_o = dil_w_o[j]
        x2d = _mix_ffn(layer, o, x2d, w_o, gain(norm_mix_post, layer), gain(norm_ffn_pre, layer),
                       ffn_w_gate, ffn_w_up, ffn_w_down, gain(norm_ffn_post, layer))
    return x2d.reshape(BATCH, SEQ, D_MODEL)
```

```python
import functools

import jax
import jax.numpy as jnp
from jax import lax
from jax.experimental import pallas as pl
from jax.experimental.pallas import tpu as pltpu

D_MODEL = 1024
BATCH = 8
SEQ = 2048
DEPTH = 2
N_HEADS = 16
HEAD_DIM = 64
GRID_W = 64
GRID_ROWS = SEQ // GRID_W
NA_ROWS = 8
NA_COLS = 16
DIL_GROUPS = ((128, 1), (512, 4), (2048, 16))
N_GROUPS = len(DIL_GROUPS)
BAND_BLOCK = 128
D_FF = 2816
RMS_EPS = 1e-6
NEG_INF = -1e30
LOG2E = 1.4426950408889634
QK_SCALE = HEAD_DIM ** -0.5 * LOG2E

LANES = 128
N_PAIRS = D_MODEL // LANES
N_TOK = BATCH * SEQ
TM = 512
FF_SPLITS =((0, 1536), (1536, 1280))
DIL_TN = 1024
CAST_COLS = 512
WEIGHT_STEPS = 8
SUBLANES = 8
MXU_DEPTH = 256
NA_ROWS_PER_ITER = 16
DIL_BLOCKS_PER_ITER = 8
DIL_WINDOW_OFFSETS = (0, BAND_BLOCK // 2, BAND_BLOCK)
VMEM_LIMIT = 56 * 1024 * 1024

_BF16 = jnp.bfloat16
_F32 = jnp.float32


def _rms(x, g):
    ms = jnp.mean(x * x, axis=-1, keepdims=True)
    return (x * lax.rsqrt(ms + RMS_EPS)) * g


def _resident(shape):
    return pl.BlockSpec(shape, lambda *_: (0,) * len(shape), pipeline_mode=pl.Buffered(1))


def _params(*sem):
    return pltpu.CompilerParams(dimension_semantics=sem, vmem_limit_bytes=VMEM_LIMIT)


def _qkv_na_kernel(x_ref, g_ref, w_ref, o_ref, wbf_scr):
    @pl.when(pl.program_id(0) == 0)
    def _():
        for c in range(3 * D_MODEL // CAST_COLS):
            cols = slice(c * CAST_COLS, (c + 1) * CAST_COLS)
            wbf_scr[:, cols] = w_ref[:, cols].astype(_BF16)

    h = _rms(x_ref[...], g_ref[...]).astype(_BF16)
    for j in range(3):
        y = jnp.dot(h, wbf_scr[:, j * D_MODEL:(j + 1) * D_MODEL], preferred_element_type=_F32)
        if j == 0:
            y = y * QK_SCALE
        for c in range(N_PAIRS):
            o_ref[0, j * N_PAIRS + c] = y[:, c * LANES:(c + 1) * LANES].astype(_BF16)


def _qkv_na(x2d, gain, w):
    tiles_per_seq = SEQ // TM
    return pl.pallas_call(
        _qkv_na_kernel,
        grid=(N_TOK // TM,),
        in_specs=[
            pl.BlockSpec((TM, D_MODEL), lambda i: (i, 0)),
            _resident((1, D_MODEL)),
            _resident((D_MODEL, 3 * D_MODEL)),
        ],
        out_specs=pl.BlockSpec((1, 3 * N_PAIRS, TM, LANES),
                               lambda i: (i // tiles_per_seq, 0, i % tiles_per_seq, 0)),
        out_shape=jax.ShapeDtypeStruct((BATCH, 3 * N_PAIRS, SEQ, LANES), _BF16),
        scratch_shapes=[pltpu.VMEM((D_MODEL, 3 * D_MODEL), _BF16)],
        compiler_params=_params("arbitrary"),
        name="qkv_na",
    )(x2d, gain, w)


def _stack_heads(q, first_head):
    zero = jnp.zeros_like(q)
    return jnp.concatenate([jnp.where(first_head, q, zero), jnp.where(first_head, zero, q)], axis=0)


def _masked_probs(s, bias, cap):
    s = jnp.minimum(s + bias, cap)
    m = jnp.max(s, axis=-1, keepdims=True)
    return jnp.exp2(s - m).astype(_BF16), m


def _stacked_pv(p, v, first_head):
    rows = p.shape[0] // 2
    r = jnp.dot(p, jnp.concatenate([v, jnp.ones_like(v)], axis=1), preferred_element_type=_F32)
    num = jnp.where(first_head, r[:rows, :LANES], r[rows:, :LANES])
    den = jnp.where(first_head, r[:rows, LANES:], r[rows:, LANES:])
    return num * (1.0 / den), den


def _na_kernel(q_ref, k_ref, v_ref, rpb_ref, o_ref, bias_scr, cap_scr, p_scr):
    kh = NA_ROWS
    nk = kh * GRID_W
    per_iter = NA_ROWS_PER_ITER
    n_iter = GRID_ROWS // per_iter
    lane = lax.broadcasted_iota(jnp.int32, (GRID_W, LANES), 1)
    first_head = lane < HEAD_DIM

    @pl.when(pl.program_id(1) == 0)
    def _():
        n_row_off, n_col_off = 2 * kh - 1, 2 * NA_COLS - 1
        col_off = jnp.clip(lax.broadcasted_iota(jnp.int32, (GRID_W, LANES), 1) % GRID_W
                           - lax.broadcasted_iota(jnp.int32, (GRID_W, LANES), 0) + NA_COLS - 1, 0, n_col_off - 1)
        for hh in range(2):
            for row_off in range(n_row_off):
                base = ((2 * pl.program_id(0) + hh) * n_row_off + row_off) * n_col_off
                tile = jnp.zeros((GRID_W, LANES), _F32)
                for c in range(n_col_off):
                    tile = jnp.where(col_off == c, rpb_ref[base + c], tile)
                for d in range(kh):
                    r = row_off - (kh - 1 - d)
                    if 0 <= r < kh:
                        half = slice((r % 2) * GRID_W, (r % 2 + 1) * GRID_W)
                        bias_scr[d, hh * GRID_W:(hh + 1) * GRID_W, r * GRID_W:(r + 1) * GRID_W] = tile[:, half]
        qcol = lax.broadcasted_iota(jnp.int32, (2 * GRID_W, LANES), 0) % GRID_W
        kcol = lax.broadcasted_iota(jnp.int32, (2 * GRID_W, LANES), 1) % GRID_W
        col_start = jnp.clip(qcol - NA_COLS // 2, 0, GRID_W - NA_COLS)
        col_ok = (kcol >= col_start) & (kcol < col_start + NA_COLS)
        cap_scr[...] = jnp.where(col_ok, jnp.finfo(_F32).max, NEG_INF).astype(_F32)

    def window(i):
        rs = jnp.clip(i - kh // 2, 0, GRID_ROWS - kh)
        return rs, pl.ds(pl.multiple_of(rs * GRID_W, GRID_W), nk)

    def query_rows(i):
        return pl.ds(pl.multiple_of(i * GRID_W, GRID_W), GRID_W)

    def scores(slot, u, i):
        rs, win = window(i)
        s = lax.dot_general(_stack_heads(q_ref[0, 0, query_rows(i), :], first_head), k_ref[0, 0, win, :],
                            (((1,), (1,)), ((), ())), preferred_element_type=_F32)
        cap = jnp.concatenate([cap_scr[...]] * (nk // LANES), axis=1)
        p_scr[slot, u], _ = _masked_probs(s, bias_scr[i - rs], cap)

    def outputs(slot, u, i):
        _, win = window(i)
        o, _ = _stacked_pv(p_scr[slot, u], v_ref[0, 0, win, :], first_head)
        o_ref[0, 0, query_rows(i), :] = o.astype(_BF16)

    for u in range(per_iter):
        scores(0, u, u)

    def body(it, carry):
        slot = it % 2
        for u in range(per_iter):
            outputs(slot, u, it * per_iter + u)
            scores(1 - slot, u, (it + 1) * per_iter + u)
        return carry

    lax.fori_loop(0, n_iter - 1, body, 0)
    for u in range(per_iter):
        outputs((n_iter - 1) % 2, u, (n_iter - 1) * per_iter + u)


def _na_attention(qkv, rpb_flat):
    slab = (1, 1, SEQ, LANES)
    nk = NA_ROWS * GRID_W
    return pl.pallas_call(
        _na_kernel,
        grid=(N_PAIRS, BATCH),
        in_specs=[
            pl.BlockSpec(slab, lambda p, b: (b, p, 0, 0)),
            pl.BlockSpec(slab, lambda p, b: (b, N_PAIRS + p, 0, 0)),
            pl.BlockSpec(slab, lambda p, b: (b, 2 * N_PAIRS + p, 0, 0)),
            pl.BlockSpec(memory_space=pltpu.SMEM),
        ],
        out_specs=pl.BlockSpec(slab, lambda p, b: (b, p, 0, 0)),
        out_shape=jax.ShapeDtypeStruct((BATCH, N_PAIRS, SEQ, LANES), _BF16),
        scratch_shapes=[pltpu.VMEM((NA_ROWS, 2 * GRID_W, nk), _F32),
                        pltpu.VMEM((2 * GRID_W, LANES), _F32),
                        pltpu.VMEM((2, NA_ROWS_PER_ITER, 2 * GRID_W, nk), _BF16)],
        compiler_params=_params("arbitrary", "arbitrary"),
        name="na_attention",
    )(qkv, qkv, qkv, rpb_flat)


def _mix_ffn_kernel(o_ref, x_ref, wo_ref, gmix_ref, gpre_ref, wg_ref, wu_ref, wd_ref, gpost_ref, out_ref,
                    wo_scr, wg_scr, wu_scr, wd_scr, x_scr, h_scr):
    i = pl.program_id(0)

    @pl.when(i < WEIGHT_STEPS)
    def _():
        for src, dst in ((wo_ref, wo_scr), (wg_ref, wg_scr), (wu_ref, wu_scr), (wd_ref, wd_scr)):
            n = src.shape[0]
            dst[pl.ds(pl.multiple_of(i * n, n), n), :] = src[...].astype(_BF16)

    stage = i - WEIGHT_STEPS
    n_tiles = N_TOK // TM

    def mix(slot):
        o = jnp.concatenate([o_ref[0, c] for c in range(N_PAIRS)], axis=-1)
        y = jnp.dot(o, wo_scr[...], preferred_element_type=_F32)
        x = x_ref[...] + _rms(y, gmix_ref[...])
        x_scr[slot] = x
        h_scr[slot] = _rms(x, gpre_ref[...]).astype(_BF16)

    def channel_mixer(slot, mix_next=None):
        h = h_scr[slot]
        y = None
        for n, (start, size) in enumerate(FF_SPLITS):
            g = jnp.dot(h, wg_scr[:, start:start + size], preferred_element_type=_F32)
            u = jnp.dot(h, wu_scr[:, start:start + size], preferred_element_type=_F32)
            if n == 0 and mix_next is not None:
                mix_next()
            a = ((g * jax.nn.sigmoid(g)) * u).astype(_BF16)
            part = jnp.dot(a, wd_scr[start:start + size, :], preferred_element_type=_F32)
            y = part if y is None else y + part
        out_ref[...] = x_scr[slot] + _rms(y, gpost_ref[...])

    @pl.when(stage == 0)
    def _():
        mix(0)

    @pl.when((stage > 0) & (stage < n_tiles))
    def _():
        slot = stage % 2
        channel_mixer(1 - slot, functools.partial(mix, slot))

    @pl.when(stage == n_tiles)
    def _():
        channel_mixer((n_tiles - 1) % 2)


def _mix_ffn(layer, o_cbm, x2d, wo, gmix, gpre, wg, wu, wd, gpost):
    tiles_per_seq = SEQ // TM

    n_tiles = N_TOK // TM

    def tile(i):
        return jnp.clip(i - WEIGHT_STEPS, 0, n_tiles - 1)

    def out_tile(i):
        return jnp.clip(i - WEIGHT_STEPS - 1, 0, n_tiles - 1)

    def chunk(i):
        return jnp.minimum(i, WEIGHT_STEPS - 1)

    def weight_chunk(rows, cols):
        return pl.BlockSpec((rows // WEIGHT_STEPS, cols), lambda i: (chunk(i), 0))

    def layer_weight_chunk(rows, cols):
        return pl.BlockSpec((None, rows // WEIGHT_STEPS, cols), lambda i: (layer, chunk(i), 0))

    return pl.pallas_call(
        _mix_ffn_kernel,
        grid=(WEIGHT_STEPS + n_tiles + 1,),
        in_specs=[
            pl.BlockSpec((1, N_PAIRS, TM, LANES), lambda i: (tile(i) // tiles_per_seq, 0, tile(i) % tiles_per_seq, 0)),
            pl.BlockSpec((TM, D_MODEL), lambda i: (tile(i), 0)),
            weight_chunk(D_MODEL, D_MODEL),
            _resident((1, D_MODEL)),
            _resident((1, D_MODEL)),
            layer_weight_chunk(D_MODEL, D_FF),
            layer_weight_chunk(D_MODEL, D_FF),
            layer_weight_chunk(D_FF, D_MODEL),
            _resident((1, D_MODEL)),
        ],
        out_specs=pl.BlockSpec((TM, D_MODEL), lambda i: (out_tile(i), 0)),
        out_shape=jax.ShapeDtypeStruct((N_TOK, D_MODEL), _F32),
        scratch_shapes=[pltpu.VMEM((D_MODEL, D_MODEL), _BF16), pltpu.VMEM((D_MODEL, D_FF), _BF16),
                        pltpu.VMEM((D_MODEL, D_FF), _BF16), pltpu.VMEM((D_FF, D_MODEL), _BF16),
                        pltpu.VMEM((2, TM, D_MODEL), _F32), pltpu.VMEM((2, TM, D_MODEL), _BF16)],
        compiler_params=_params("arbitrary"),
        name="mix_ffn",
    )(o_cbm, x2d, wo, gmix, gpre, wg, wu, wd, gpost)


def _qkv_dil_kernel(x_ref, g_ref, w_ref, o_ref, h32_scr, hperm_scr):
    j = pl.program_id(1)
    tiles_per_group = 3 * D_MODEL // DIL_TN

    @pl.when(j == 0)
    def _():
        for t in range(SEQ // TM):
            rows = slice(t * TM, (t + 1) * TM)
            h = _rms(x_ref[0, rows, :], g_ref[...])
            for c in range(N_PAIRS):
                h32_scr[c, rows, :] = h[:, c * LANES:(c + 1) * LANES]

    for grp, (_, dil) in enumerate(DIL_GROUPS):
        @pl.when(j == grp * tiles_per_group)
        def _(dil=dil):
            sub = SEQ // dil
            if dil <= SUBLANES:
                for r in range(dil):
                    for c in range(N_PAIRS):
                        val = h32_scr[c] if dil == 1 else h32_scr[c, pl.ds(r, sub, stride=dil), :]
                        hperm_scr[r * sub:(r + 1) * sub, c * LANES:(c + 1) * LANES] = val.astype(_BF16)
            else:
                per = MXU_DEPTH // dil
                out_row = lax.broadcasted_iota(jnp.int32, (MXU_DEPTH, MXU_DEPTH), 0)
                in_row = lax.broadcasted_iota(jnp.int32, (MXU_DEPTH, MXU_DEPTH), 1)
                pick = (in_row == dil * (out_row % per) + out_row // per).astype(_BF16)
                for a in range(SEQ // MXU_DEPTH):
                    rows = slice(a * MXU_DEPTH, (a + 1) * MXU_DEPTH)
                    chunk = jnp.concatenate([h32_scr[c, rows, :].astype(_BF16) for c in range(N_PAIRS)], axis=1)
                    picked = jnp.dot(pick, chunk, preferred_element_type=_F32).astype(_BF16)
                    for r in range(dil):
                        hperm_scr[r * sub + a * per:r * sub + (a + 1) * per, :] = picked[r * per:(r + 1) * per]

    y = jnp.dot(hperm_scr[...], w_ref[...].astype(_BF16), preferred_element_type=_F32)
    is_q_tile = j % tiles_per_group < D_MODEL // DIL_TN
    y = y * jnp.where(is_q_tile, QK_SCALE, 1.0)
    for c in range(DIL_TN // LANES):
        o_ref[0, c] = y[:, c * LANES:(c + 1) * LANES].astype(_BF16)


def _qkv_dil(x3d, gain, w):
    n_col = N_GROUPS * 3 * D_MODEL
    blocks_per_tile = DIL_TN // LANES
    return pl.pallas_call(
        _qkv_dil_kernel,
        grid=(BATCH, n_col // DIL_TN),
        in_specs=[
            pl.BlockSpec((1, SEQ, D_MODEL), lambda b, j: (b, 0, 0)),
            _resident((1, D_MODEL)),
            pl.BlockSpec((D_MODEL, DIL_TN), lambda b, j: (0, j)),
        ],
        out_specs=pl.BlockSpec((1, blocks_per_tile, SEQ, LANES), lambda b, j: (b, j, 0, 0)),
        out_shape=jax.ShapeDtypeStruct((BATCH, n_col // LANES, SEQ, LANES), _BF16),
        scratch_shapes=[pltpu.VMEM((N_PAIRS, SEQ, LANES), _F32), pltpu.VMEM((SEQ, D_MODEL), _BF16)],
        compiler_params=_params("parallel", "arbitrary"),
        name="qkv_dil",
    )(x3d, gain, w)


def _dil_kernel(slopes_ref, q0, k0, v0, q1, k1, v1, q2, k2, v2, o_ref, og_scr, lg_scr, bias_scr, cap_scr, p_scr,
                m_scr):
    pair = pl.program_id(0)
    qkv_refs = ((q0, k0, v0), (q1, k1, v1), (q2, k2, v2))
    lane = lax.broadcasted_iota(jnp.int32, (BAND_BLOCK, LANES), 1)
    first_head = lane < HEAD_DIM
    max_width = 2 * BAND_BLOCK
    per_iter = DIL_BLOCKS_PER_ITER
    n_iter = SEQ // (BAND_BLOCK * per_iter)

    @pl.when(pl.program_id(1) == 0)
    def _():
        rel = (lax.broadcasted_iota(jnp.int32, (BAND_BLOCK, max_width), 0)
               - lax.broadcasted_iota(jnp.int32, (BAND_BLOCK, max_width), 1))
        for grp, (window, dil) in enumerate(DIL_GROUPS):
            radius = window // (2 * dil)
            for var, off in enumerate(DIL_WINDOW_OFFSETS):
                dist = jnp.abs(rel + off)
                distf = dist.astype(_F32)
                cap_scr[grp, var] = jnp.where(dist <= radius, jnp.finfo(_F32).max, NEG_INF).astype(_F32)
                for hh in range(2):
                    head_rows = slice(hh * BAND_BLOCK, (hh + 1) * BAND_BLOCK)
                    bias_scr[grp, var, head_rows, :] = -((slopes_ref[2 * pair + hh] * dil) * distf) * LOG2E

    def stage_fns(grp):
        window, dil = DIL_GROUPS[grp]
        q_ref, k_ref, v_ref = qkv_refs[grp]
        radius = window // (2 * dil)
        sub = SEQ // dil
        blocks_per_sub = sub // BAND_BLOCK
        width = min(max_width, sub)

        def place(t):
            res = t // blocks_per_sub
            c = t % blocks_per_sub
            kstart = jnp.clip(c * BAND_BLOCK - radius, 0, sub - width)
            return res, c, pl.ds(pl.multiple_of(res * sub + kstart, radius), width)

        def scores(slot, u, t):
            _, c, key_rows = place(t)
            q = q_ref[0, 0, pl.ds(pl.multiple_of(t * BAND_BLOCK, BAND_BLOCK), BAND_BLOCK), :]
            s = lax.dot_general(_stack_heads(q, first_head), k_ref[0, 0, key_rows, :], (((1,), (1,)), ((), ())),
                                preferred_element_type=_F32)
            var = 0 if blocks_per_sub == 1 else jnp.int32(c > 0) + jnp.int32(c == blocks_per_sub - 1)
            cap = cap_scr[grp, var, :, :width]
            p, m = _masked_probs(s, bias_scr[grp, var, :, :width], jnp.concatenate([cap, cap], axis=0))
            p_scr[slot, u, :, :width] = p
            m_scr[slot, u] = jnp.where(first_head, m[:BAND_BLOCK], m[BAND_BLOCK:])

        def outputs(slot, u, t):
            res, c, key_rows = place(t)
            o, den = _stacked_pv(p_scr[slot, u, :, :width], v_ref[0, 0, key_rows, :], first_head)
            if dil == 1:
                rows = pl.ds(pl.multiple_of(t * BAND_BLOCK, BAND_BLOCK), BAND_BLOCK)
            else:
                rows = pl.ds(c * (BAND_BLOCK * dil) + res, BAND_BLOCK, stride=dil)
            og_scr[grp, rows, :] = o
            lg_scr[grp, rows, :] = m_scr[slot, u] + jnp.log2(den)

        return scores, outputs

    fns = [stage_fns(grp) for grp in range(N_GROUPS)]
    for u in range(per_iter):
        fns[0][0](0, u, u)
    for grp in range(N_GROUPS):
        scores, outputs = fns[grp]
        first_slot = (grp * n_iter) % 2

        def body(it, carry, scores=scores, outputs=outputs, first_slot=first_slot):
            slot = (first_slot + it) % 2
            for u in range(per_iter):
                outputs(slot, u, it * per_iter + u)
                scores(1 - slot, u, (it + 1) * per_iter + u)
            return carry

        if n_iter > 1:
            lax.fori_loop(0, n_iter - 1, body, 0)
        last_slot = (first_slot + n_iter - 1) % 2
        for u in range(per_iter):
            outputs(last_slot, u, (n_iter - 1) * per_iter + u)
            if grp + 1 < N_GROUPS:
                fns[grp + 1][0](1 - last_slot, u, u)

    chunk = 256
    for t in range(SEQ // chunk):
        rows = slice(t * chunk, (t + 1) * chunk)
        lse = [lg_scr[grp, rows, :] for grp in range(N_GROUPS)]
        mx = jnp.maximum(jnp.maximum(lse[0], lse[1]), lse[2])
        e = [jnp.exp2(x - mx) for x in lse]
        den = e[0] + e[1] + e[2]
        num = e[0] * og_scr[0, rows, :] + e[1] * og_scr[1, rows, :] + e[2] * og_scr[2, rows, :]
        o_ref[0, 0, rows, :] = (num / den).astype(_BF16)


def _dil_attention(qkv, slopes):
    slab = (1, 1, SEQ, LANES)

    def spec(grp, which):
        base = (grp * 3 + which) * N_PAIRS
        return pl.BlockSpec(slab, lambda p, b: (b, base + p, 0, 0))

    return pl.pallas_call(
        _dil_kernel,
        grid=(N_PAIRS, BATCH),
        in_specs=[pl.BlockSpec(memory_space=pltpu.SMEM)]
        + [spec(grp, which) for grp in range(N_GROUPS) for which in range(3)],
        out_specs=pl.BlockSpec(slab, lambda p, b: (b, p, 0, 0)),
        out_shape=jax.ShapeDtypeStruct((BATCH, N_PAIRS, SEQ, LANES), _BF16),
        scratch_shapes=[
            pltpu.VMEM((N_GROUPS, SEQ, LANES), _F32),
            pltpu.VMEM((N_GROUPS, SEQ, LANES), _F32),
            pltpu.VMEM((N_GROUPS, len(DIL_WINDOW_OFFSETS), 2 * BAND_BLOCK, 2 * BAND_BLOCK), _F32),
            pltpu.VMEM((N_GROUPS, len(DIL_WINDOW_OFFSETS), BAND_BLOCK, 2 * BAND_BLOCK), _F32),
            pltpu.VMEM((2, DIL_BLOCKS_PER_ITER, 2 * BAND_BLOCK, 2 * BAND_BLOCK), _BF16),
            pltpu.VMEM((2, DIL_BLOCKS_PER_ITER, BAND_BLOCK, LANES), _F32),
        ],
        compiler_params=_params("arbitrary", "arbitrary"),
        name="dil_attention",
    )(slopes, *([qkv] * 9))


def kernel(x, norm_mix_pre, norm_mix_post, norm_ffn_pre, norm_ffn_post, na_w_qkv, na_w_o, na_rpb, dil_w_qkv, dil_w_o,
           ffn_w_gate, ffn_w_up, ffn_w_down):
    x2d = x.reshape(N_TOK, D_MODEL)
    slopes = 2.0 ** (-8.0 * jnp.arange(1, N_HEADS + 1, dtype=_F32) / N_HEADS)

    def gain(g, layer):
        return g[layer].reshape(1, D_MODEL)

    for layer in range(DEPTH):
        j = layer // 2
        if layer % 2 == 0:
            qkv = _qkv_na(x2d, gain(norm_mix_pre, layer), na_w_qkv[j])
            o = _na_attention(qkv, (na_rpb[j].astype(_F32) * LOG2E).reshape(-1))
            w_o = na_w_o[j]
        else:
            qkv = _qkv_dil(x2d.reshape(BATCH, SEQ, D_MODEL), gain(norm_mix_pre, layer), dil_w_qkv[j])
            o = _dil_attention(qkv, slopes)
            w_o = dil_w_o[j]
        x2d = _mix_ffn(layer, o, x2d, w_o, gain(norm_mix_post, layer), gain(norm_ffn_pre, layer),
                       ffn_w_gate, ffn_w_up, ffn_w_down, gain(norm_ffn_post, layer))
    return x2d.reshape(BATCH, SEQ, D_MODEL)
```

```python
import jax
import jax.numpy as jnp
from jax import lax
from jax.experimental import pallas as pl
from jax.experimental.pallas import tpu as pltpu

D_MODEL = 1024
BATCH = 8
SEQ = 2048
DEPTH = 2
N_HEADS = 16
HEAD_DIM = 64
GRID_W = 64
GRID_ROWS = SEQ // GRID_W
NA_ROWS = 8
NA_COLS = 16
DIL_GROUPS = ((128, 1), (512, 4), (2048, 16))
N_GROUPS = len(DIL_GROUPS)
BAND_BLOCK = 128
D_FF = 2816
RMS_EPS = 1e-6
NEG_INF = -1e30
LOG2E = 1.4426950408889634
QK_SCALE = HEAD_DIM ** -0.5 * LOG2E

LANES = 128
N_PAIRS = D_MODEL // LANES
N_TOK = BATCH * SEQ
TM = 512
FF_SPLITS = ((0, 1536), (1536, 1280))
DIL_TN = 1024
CAST_COLS = 512
WEIGHT_STEPS = 8
SUBLANES = 8
MXU_DEPTH = 256
NA_ROWS_PER_ITER = 16
DIL_BLOCKS_PER_ITER = 8
DIL_WINDOW_OFFSETS = (0, BAND_BLOCK // 2, BAND_BLOCK)
VMEM_LIMIT = 56 * 1024 * 1024

_BF16 = jnp.bfloat16
_F32 = jnp.float32


def _rms(x, g):
    ms = jnp.mean(x * x, axis=-1, keepdims=True)
    return (x * lax.rsqrt(ms + RMS_EPS)) * g


def _resident(shape):
    return pl.BlockSpec(shape, lambda *_: (0,) * len(shape), pipeline_mode=pl.Buffered(1))


def _params(*sem):
    return pltpu.CompilerParams(dimension_semantics=sem, vmem_limit_bytes=VMEM_LIMIT)


def _qkv_na_kernel(x_ref, g_ref, w_ref, o_ref, wbf_scr):
    @pl.when(pl.program_id(0) == 0)
    def _():
        for c in range(3 * D_MODEL // CAST_COLS):
            cols = slice(c * CAST_COLS, (c + 1) * CAST_COLS)
            wbf_scr[:, cols] = w_ref[:, cols].astype(_BF16)

    h = _rms(x_ref[...], g_ref[...]).astype(_BF16)
    for j in range(3):
        y = jnp.dot(h, wbf_scr[:, j * D_MODEL:(j + 1) * D_MODEL], preferred_element_type=_F32)
        if j == 0:
            y = y * QK_SCALE
        for c in range(N_PAIRS):
            o_ref[0, j * N_PAIRS + c] = y[:, c * LANES:(c + 1) * LANES].astype(_BF16)


def _qkv_na(x2d, gain, w):
    tiles_per_seq = SEQ // TM
    return pl.pallas_call(
        _qkv_na_kernel,
        grid=(N_TOK // TM,),
        in_specs=[
            pl.BlockSpec((TM, D_MODEL), lambda i: (i, 0)),
            _resident((1, D_MODEL)),
            _resident((D_MODEL, 3 * D_MODEL)),
        ],
        out_specs=pl.BlockSpec((1, 3 * N_PAIRS, TM, LANES),
                               lambda i: (i // tiles_per_seq, 0, i % tiles_per_seq, 0)),
        out_shape=jax.ShapeDtypeStruct((BATCH, 3 * N_PAIRS, SEQ, LANES), _BF16),
        scratch_shapes=[pltpu.VMEM((D_MODEL, 3 * D_MODEL), _BF16)],
        compiler_params=_params("arbitrary"),
        name="qkv_na",
    )(x2d, gain, w)


def _stack_heads(q, first_head):
    zero = jnp.zeros_like(q)
    return jnp.concatenate([jnp.where(first_head, q, zero), jnp.where(first_head, zero, q)], axis=0)


def _masked_probs(s, bias, cap):
    s = jnp.minimum(s + bias, cap)
    m = jnp.max(s, axis=-1, keepdims=True)
    return jnp.exp2(s - m).astype(_BF16), m


def _stacked_pv(p, v, first_head):
    rows = p.shape[0] // 2
    r = jnp.dot(p, jnp.concatenate([v, jnp.ones_like(v)], axis=1), preferred_element_type=_F32)
    num = jnp.where(first_head, r[:rows, :LANES], r[rows:, :LANES])
    den = jnp.where(first_head, r[:rows, LANES:], r[rows:, LANES:])
    return num * (1.0 / den), den


def _na_kernel(q_ref, k_ref, v_ref, rpb_ref, o_ref, bias_scr, cap_scr, p_scr):
    kh = NA_ROWS
    nk = kh * GRID_W
    per_iter = NA_ROWS_PER_ITER
    n_iter = GRID_ROWS // per_iter
    lane = lax.broadcasted_iota(jnp.int32, (GRID_W, LANES), 1)
    first_head = lane < HEAD_DIM

    @pl.when(pl.program_id(1) == 0)
    def _():
        n_row_off, n_col_off = 2 * kh - 1, 2 * NA_COLS - 1
        col_off = jnp.clip(lax.broadcasted_iota(jnp.int32, (GRID_W, LANES), 1) % GRID_W
                           - lax.broadcasted_iota(jnp.int32, (GRID_W, LANES), 0) + NA_COLS - 1, 0, n_col_off - 1)
        for hh in range(2):
            for row_off in range(n_row_off):
                base = ((2 * pl.program_id(0) + hh) * n_row_off + row_off) * n_col_off
                tile = jnp.zeros((GRID_W, LANES), _F32)
                for c in range(n_col_off):
                    tile = jnp.where(col_off == c, rpb_ref[base + c], tile)
                for d in range(kh):
                    r = row_off - (kh - 1 - d)
                    if 0 <= r < kh:
                        half = slice((r % 2) * GRID_W, (r % 2 + 1) * GRID_W)
                        bias_scr[d, hh * GRID_W:(hh + 1) * GRID_W, r * GRID_W:(r + 1) * GRID_W] = tile[:, half]
        qcol = lax.broadcasted_iota(jnp.int32, (2 * GRID_W, LANES), 0) % GRID_W
        kcol = lax.broadcasted_iota(jnp.int32, (2 * GRID_W, LANES), 1) % GRID_W
        col_start = jnp.clip(qcol - NA_COLS // 2, 0, GRID_W - NA_COLS)
        col_ok = (kcol >= col_start) & (kcol < col_start + NA_COLS)
        cap_scr[...] = jnp.where(col_ok, jnp.finfo(_F32).max, NEG_INF).astype(_F32)

    def window(i):
        rs = jnp.clip(i - kh // 2, 0, GRID_ROWS - kh)
        return rs, pl.ds(pl.multiple_of(rs * GRID_W, GRID_W), nk)

    def query_rows(i):
        return pl.ds(pl.multiple_of(i * GRID_W, GRID_W), GRID_W)

    def scores(slot, u, i):
        rs, win = window(i)
        s = lax.dot_general(_stack_heads(q_ref[0, 0, query_rows(i), :], first_head), k_ref[0, 0, win, :],
                            (((1,), (1,)), ((), ())), preferred_element_type=_F32)
        cap = jnp.concatenate([cap_scr[...]] * (nk // LANES), axis=1)
        p_scr[slot, u], _ = _masked_probs(s, bias_scr[i - rs], cap)

    def outputs(slot, u, i):
        _, win = window(i)
        o, _ = _stacked_pv(p_scr[slot, u], v_ref[0, 0, win, :], first_head)
        o_ref[0, 0, query_rows(i), :] = o.astype(_BF16)

    for u in range(per_iter):
        scores(0, u, u)

    def body(it, carry):
        slot = it % 2
        for u in range(per_iter):
            outputs(slot, u, it * per_iter + u)
            scores(1 - slot, u, (it + 1) * per_iter + u)
        return carry

    lax.fori_loop(0, n_iter - 1, body, 0)
    for u in range(per_iter):
        outputs((n_iter - 1) % 2, u, (n_iter - 1) * per_iter + u)


def _na_attention(qkv, rpb_flat):
    slab = (1, 1, SEQ, LANES)
    nk = NA_ROWS * GRID_W
    return pl.pallas_call(
        _na_kernel,
        grid=(N_PAIRS, BATCH),
        in_specs=[
            pl.BlockSpec(slab, lambda p, b: (b, p, 0, 0)),
            pl.BlockSpec(slab, lambda p, b: (b, N_PAIRS + p, 0, 0)),
            pl.BlockSpec(slab, lambda p, b: (b, 2 * N_PAIRS + p, 0, 0)),
            pl.BlockSpec(memory_space=pltpu.SMEM),
        ],
        out_specs=pl.BlockSpec(slab, lambda p, b: (b, p, 0, 0)),
        out_shape=jax.ShapeDtypeStruct((BATCH, N_PAIRS, SEQ, LANES), _BF16),
        scratch_shapes=[pltpu.VMEM((NA_ROWS, 2 * GRID_W, nk), _F32),
                        pltpu.VMEM((2 * GRID_W, LANES), _F32),
                        pltpu.VMEM((2, NA_ROWS_PER_ITER, 2 * GRID_W, nk), _BF16)],
        compiler_params=_params("arbitrary", "arbitrary"),
        name="na_attention",
    )(qkv, qkv, qkv, rpb_flat)


def _mix_ffn_kernel(o_ref, x_ref, wo_ref, gmix_ref, gpre_ref, wg_ref, wu_ref, wd_ref, gpost_ref, out_ref,
                    wo_scr, wg_scr, wu_scr, wd_scr):
    i = pl.program_id(0)

    @pl.when(i < WEIGHT_STEPS)
    def _():
        for src, dst in ((wo_ref, wo_scr), (wg_ref, wg_scr), (wu_ref, wu_scr), (wd_ref, wd_scr)):
            n = src.shape[0]
            dst[pl.ds(pl.multiple_of(i * n, n), n), :] = src[...].astype(_BF16)

    @pl.when(i >= WEIGHT_STEPS)
    def _():
        o = jnp.concatenate([o_ref[0, c] for c in range(N_PAIRS)], axis=-1)
        y = jnp.dot(o, wo_scr[...], preferred_element_type=_F32)
        x = x_ref[...] + _rms(y, gmix_ref[...])
        h = _rms(x, gpre_ref[...]).astype(_BF16)
        y = None
        for start, size in FF_SPLITS:
            g = jnp.dot(h, wg_scr[:, start:start + size], preferred_element_type=_F32)
            u = jnp.dot(h, wu_scr[:, start:start + size], preferred_element_type=_F32)
            a = ((g * jax.nn.sigmoid(g)) * u).astype(_BF16)
            part = jnp.dot(a, wd_scr[start:start + size, :], preferred_element_type=_F32)
            y = part if y is None else y + part
        out_ref[...] = x + _rms(y, gpost_ref[...])


def _mix_ffn(layer, o_cbm, x2d, wo, gmix, gpre, wg, wu, wd, gpost):
    tiles_per_seq = SEQ // TM

    def tile(i):
        return jnp.maximum(i - WEIGHT_STEPS, 0)

    def chunk(i):
        return jnp.minimum(i, WEIGHT_STEPS - 1)

    def weight_chunk(rows, cols):
        return pl.BlockSpec((rows // WEIGHT_STEPS, cols), lambda i: (chunk(i), 0))

    def layer_weight_chunk(rows, cols):
        return pl.BlockSpec((None, rows // WEIGHT_STEPS, cols), lambda i: (layer, chunk(i), 0))

    return pl.pallas_call(
        _mix_ffn_kernel,
        grid=(WEIGHT_STEPS + N_TOK // TM,),
        in_specs=[
            pl.BlockSpec((1, N_PAIRS, TM, LANES), lambda i: (tile(i) // tiles_per_seq, 0, tile(i) % tiles_per_seq, 0)),
            pl.BlockSpec((TM, D_MODEL), lambda i: (tile(i), 0)),
            weight_chunk(D_MODEL, D_MODEL),
            _resident((1, D_MODEL)),
            _resident((1, D_MODEL)),
            layer_weight_chunk(D_MODEL, D_FF),
            layer_weight_chunk(D_MODEL, D_FF),
            layer_weight_chunk(D_FF, D_MODEL),
            _resident((1, D_MODEL)),
        ],
        out_specs=pl.BlockSpec((TM, D_MODEL), lambda i: (tile(i), 0)),
        out_shape=jax.ShapeDtypeStruct((N_TOK, D_MODEL), _F32),
        scratch_shapes=[pltpu.VMEM((D_MODEL, D_MODEL), _BF16), pltpu.VMEM((D_MODEL, D_FF), _BF16),
                        pltpu.VMEM((D_MODEL, D_FF), _BF16), pltpu.VMEM((D_FF, D_MODEL), _BF16)],
        compiler_params=_params("arbitrary"),
        name="mix_ffn",
    )(o_cbm, x2d, wo, gmix, gpre, wg, wu, wd, gpost)


def _qkv_dil_kernel(x_ref, g_ref, w_ref, o_ref, h32_scr, hperm_scr):
    j = pl.program_id(1)
    tiles_per_group = 3 * D_MODEL // DIL_TN

    @pl.when(j == 0)
    def _():
        for t in range(SEQ // TM):
            rows = slice(t * TM, (t + 1) * TM)
            h = _rms(x_ref[0, rows, :], g_ref[...])
            for c in range(N_PAIRS):
                h32_scr[c, rows, :] = h[:, c * LANES:(c + 1) * LANES]

    for grp, (_, dil) in enumerate(DIL_GROUPS):
        @pl.when(j == grp * tiles_per_group)
        def _(dil=dil):
            sub = SEQ // dil
            if dil <= SUBLANES:
                for r in range(dil):
                    for c in range(N_PAIRS):
                        val = h32_scr[c] if dil == 1 else h32_scr[c, pl.ds(r, sub, stride=dil), :]
                        hperm_scr[r * sub:(r + 1) * sub, c * LANES:(c + 1) * LANES] = val.astype(_BF16)
            else:
                per = MXU_DEPTH // dil
                out_row = lax.broadcasted_iota(jnp.int32, (MXU_DEPTH, MXU_DEPTH), 0)
                in_row = lax.broadcasted_iota(jnp.int32, (MXU_DEPTH, MXU_DEPTH), 1)
                pick = (in_row == dil * (out_row % per) + out_row // per).astype(_BF16)
                for a in range(SEQ // MXU_DEPTH):
                    rows = slice(a * MXU_DEPTH, (a + 1) * MXU_DEPTH)
                    chunk = jnp.concatenate([h32_scr[c, rows, :].astype(_BF16) for c in range(N_PAIRS)], axis=1)
                    picked = jnp.dot(pick, chunk, preferred_element_type=_F32).astype(_BF16)
                    for r in range(dil):
                        hperm_scr[r * sub + a * per:r * sub + (a + 1) * per, :] = picked[r * per:(r + 1) * per]

    y = jnp.dot(hperm_scr[...], w_ref[...].astype(_BF16), preferred_element_type=_F32)
    is_q_tile = j % tiles_per_group < D_MODEL // DIL_TN
    y = y * jnp.where(is_q_tile, QK_SCALE, 1.0)
    for c in range(DIL_TN // LANES):
        o_ref[0, c] = y[:, c * LANES:(c + 1) * LANES].astype(_BF16)


def _qkv_dil(x3d, gain, w):
    n_col = N_GROUPS * 3 * D_MODEL
    blocks_per_tile = DIL_TN // LANES
    return pl.pallas_call(
        _qkv_dil_kernel,
        grid=(BATCH, n_col // DIL_TN),
        in_specs=[
            pl.BlockSpec((1, SEQ, D_MODEL), lambda b, j: (b, 0, 0)),
            _resident((1, D_MODEL)),
            pl.BlockSpec((D_MODEL, DIL_TN), lambda b, j: (0, j)),
        ],
        out_specs=pl.BlockSpec((1, blocks_per_tile, SEQ, LANES), lambda b, j: (b, j, 0, 0)),
        out_shape=jax.ShapeDtypeStruct((BATCH, n_col // LANES, SEQ, LANES), _BF16),
        scratch_shapes=[pltpu.VMEM((N_PAIRS, SEQ, LANES), _F32), pltpu.VMEM((SEQ, D_MODEL), _BF16)],
        compiler_params=_params("parallel", "arbitrary"),
        name="qkv_dil",
    )(x3d, gain, w)


def _dil_kernel(slopes_ref, q0, k0, v0, q1, k1, v1, q2, k2, v2, o_ref, og_scr, lg_scr, bias_scr, cap_scr, p_scr,
                m_scr):
    pair = pl.program_id(0)
    qkv_refs = ((q0, k0, v0), (q1, k1, v1), (q2, k2, v2))
    lane = lax.broadcasted_iota(jnp.int32, (BAND_BLOCK, LANES), 1)
    first_head = lane < HEAD_DIM
    max_width = 2 * BAND_BLOCK
    per_iter = DIL_BLOCKS_PER_ITER
    n_iter = SEQ // (BAND_BLOCK * per_iter)

    @pl.when(pl.program_id(1) == 0)
    def _():
        rel = (lax.broadcasted_iota(jnp.int32, (BAND_BLOCK, max_width), 0)
               - lax.broadcasted_iota(jnp.int32, (BAND_BLOCK, max_width), 1))
        for grp, (window, dil) in enumerate(DIL_GROUPS):
            radius = window // (2 * dil)
            for var, off in enumerate(DIL_WINDOW_OFFSETS):
                dist = jnp.abs(rel + off)
                distf = dist.astype(_F32)
                cap_scr[grp, var] = jnp.where(dist <= radius, jnp.finfo(_F32).max, NEG_INF).astype(_F32)
                for hh in range(2):
                    head_rows = slice(hh * BAND_BLOCK, (hh + 1) * BAND_BLOCK)
                    bias_scr[grp, var, head_rows, :] = -((slopes_ref[2 * pair + hh] * dil) * distf) * LOG2E

    def stage_fns(grp):
        window, dil = DIL_GROUPS[grp]
        q_ref, k_ref, v_ref = qkv_refs[grp]
        radius = window // (2 * dil)
        sub = SEQ // dil
        blocks_per_sub = sub // BAND_BLOCK
        width = min(max_width, sub)

        def place(t):
            res = t // blocks_per_sub
            c = t % blocks_per_sub
            kstart = jnp.clip(c * BAND_BLOCK - radius, 0, sub - width)
            return res, c, pl.ds(pl.multiple_of(res * sub + kstart, radius), width)

        def scores(slot, u, t):
            _, c, key_rows = place(t)
            q = q_ref[0, 0, pl.ds(pl.multiple_of(t * BAND_BLOCK, BAND_BLOCK), BAND_BLOCK), :]
            s = lax.dot_general(_stack_heads(q, first_head), k_ref[0, 0, key_rows, :], (((1,), (1,)), ((), ())),
                                preferred_element_type=_F32)
            var = 0 if blocks_per_sub == 1 else jnp.int32(c > 0) + jnp.int32(c == blocks_per_sub - 1)
            cap = cap_scr[grp, var, :, :width]
            p, m = _masked_probs(s, bias_scr[grp, var, :, :width], jnp.concatenate([cap, cap], axis=0))
            p_scr[slot, u, :, :width] = p
            m_scr[slot, u] = jnp.where(first_head, m[:BAND_BLOCK], m[BAND_BLOCK:])

        def outputs(slot, u, t):
            res, c, key_rows = place(t)
            o, den = _stacked_pv(p_scr[slot, u, :, :width], v_ref[0, 0, key_rows, :], first_head)
            if dil == 1:
                rows = pl.ds(pl.multiple_of(t * BAND_BLOCK, BAND_BLOCK), BAND_BLOCK)
            else:
                rows = pl.ds(c * (BAND_BLOCK * dil) + res, BAND_BLOCK, stride=dil)
            og_scr[grp, rows, :] = o
            lg_scr[grp, rows, :] = m_scr[slot, u] + jnp.log2(den)

        return scores, outputs

    fns = [stage_fns(grp) for grp in range(N_GROUPS)]
    for u in range(per_iter):
        fns[0][0](0, u, u)
    for grp in range(N_GROUPS):
        scores, outputs = fns[grp]
        first_slot = (grp * n_iter) % 2

        def body(it, carry, scores=scores, outputs=outputs, first_slot=first_slot):
            slot = (first_slot + it) % 2
            for u in range(per_iter):
                outputs(slot, u, it * per_iter + u)
                scores(1 - slot, u, (it + 1) * per_iter + u)
            return carry

        if n_iter > 1:
            lax.fori_loop(0, n_iter - 1, body, 0)
        last_slot = (first_slot + n_iter - 1) % 2
        for u in range(per_iter):
            outputs(last_slot, u, (n_iter - 1) * per_iter + u)
            if grp + 1 < N_GROUPS:
                fns[grp + 1][0](1 - last_slot, u, u)

    chunk = 256
    for t in range(SEQ // chunk):
        rows = slice(t * chunk, (t + 1) * chunk)
        lse = [lg_scr[grp, rows, :] for grp in range(N_GROUPS)]
        mx = jnp.maximum(jnp.maximum(lse[0], lse[1]), lse[2])
        e = [jnp.exp2(x - mx) for x in lse]
        den = e[0] + e[1] + e[2]
        num = e[0] * og_scr[0, rows, :] + e[1] * og_scr[1, rows, :] + e[2] * og_scr[2, rows, :]
        o_ref[0, 0, rows, :] = (num / den).astype(_BF16)


def _dil_attention(qkv, slopes):
    slab = (1, 1, SEQ, LANES)

    def spec(grp, which):
        base = (grp * 3 + which) * N_PAIRS
        return pl.BlockSpec(slab, lambda p, b: (b, base + p, 0, 0))

    return pl.pallas_call(
        _dil_kernel,
        grid=(N_PAIRS, BATCH),
        in_specs=[pl.BlockSpec(memory_space=pltpu.SMEM)]
        + [spec(grp, which) for grp in range(N_GROUPS) for which in range(3)],
        out_specs=pl.BlockSpec(slab, lambda p, b: (b, p, 0, 0)),
        out_shape=jax.ShapeDtypeStruct((BATCH, N_PAIRS, SEQ, LANES), _BF16),
        scratch_shapes=[
            pltpu.VMEM((N_GROUPS, SEQ, LANES), _F32),
            pltpu.VMEM((N_GROUPS, SEQ, LANES), _F32),
            pltpu.VMEM((N_GROUPS, len(DIL_WINDOW_OFFSETS), 2 * BAND_BLOCK, 2 * BAND_BLOCK), _F32),
            pltpu.VMEM((N_GROUPS, len(DIL_WINDOW_OFFSETS), BAND_BLOCK, 2 * BAND_BLOCK), _F32),
            pltpu.VMEM((2, DIL_BLOCKS_PER_ITER, 2 * BAND_BLOCK, 2 * BAND_BLOCK), _BF16),
            pltpu.VMEM((2, DIL_BLOCKS_PER_ITER, BAND_BLOCK, LANES), _F32),
        ],
        compiler_params=_params("arbitrary", "arbitrary"),
        name="dil_attention",
    )(slopes, *([qkv] * 9))


def kernel(x, norm_mix_pre, norm_mix_post, norm_ffn_pre, norm_ffn_post, na_w_qkv, na_w_o, na_rpb, dil_w_qkv, dil_w_o,
           ffn_w_gate, ffn_w_up, ffn_w_down):
    x2d = x.reshape(N_TOK, D_MODEL)
    slopes = 2.0 ** (-8.0 * jnp.arange(1, N_HEADS + 1, dtype=_F32) / N_HEADS)

    def gain(g, layer):
        return g[layer].reshape(1, D_MODEL)

    for layer in range(DEPTH):
        j = layer // 2
        if layer % 2 == 0:
            qkv = _qkv_na(x2d, gain(norm_mix_pre, layer), na_w_qkv[j])
            o = _na_attention(qkv, (na_rpb[j].astype(_F32) * LOG2E).reshape(-1))
            w_o = na_w_o[j]
        else:
            qkv = _qkv_dil(x2d.reshape(BATCH, SEQ, D_MODEL), gain(norm_mix_pre, layer), dil_w_qkv[j])
            o = _dil_attention(qkv, slopes)
            w_o = dil_w_o[j]
        x2d = _mix_ffn(layer, o, x2d, w_o, gain(norm_mix_post, layer), gain(norm_ffn_pre, layer),
                       ffn_w_gate, ffn_w_up, ffn_w_down, gain(norm_ffn_post, layer))
    return x2d.reshape(BATCH, SEQ, D_MODEL)
```

```python
import jax
import jax.numpy as jnp
from jax import lax
from jax.experimental import pallas as pl
from jax.experimental.pallas import tpu as pltpu

D_MODEL = 1024
BATCH = 8
SEQ = 2048
DEPTH = 2
N_HEADS = 16
HEAD_DIM = 64
GRID_W = 64
GRID_ROWS = SEQ // GRID_W
NA_ROWS = 8
NA_COLS = 16
DIL_GROUPS = ((128, 1), (512, 4), (2048, 16))
N_GROUPS = len(DIL_GROUPS)
BAND_BLOCK = 128
D_FF = 2816
RMS_EPS = 1e-6
NEG_INF = -1e30
LOG2E = 1.4426950408889634
QK_SCALE = HEAD_DIM ** -0.5 * LOG2E

LANES = 128
N_PAIRS = D_MODEL // LANES
N_TOK = BATCH * SEQ
TM = 512
FF_SPLITS = ((0, 1536), (1536, 1280))
DIL_TN = 1024
CAST_COLS = 512
WEIGHT_STEPS = 4
SUBLANES = 8
MXU_DEPTH = 256
NA_ROWS_PER_ITER = 16
DIL_BLOCKS_PER_ITER = 8
DIL_WINDOW_OFFSETS = (0, BAND_BLOCK // 2, BAND_BLOCK)
VMEM_LIMIT = 56 * 1024 * 1024

_BF16 = jnp.bfloat16
_F32 = jnp.float32


def _rms(x, g):
    ms = jnp.mean(x * x, axis=-1, keepdims=True)
    return (x * lax.rsqrt(ms + RMS_EPS)) * g


def _resident(shape):
    return pl.BlockSpec(shape, lambda *_: (0,) * len(shape), pipeline_mode=pl.Buffered(1))


def _params(*sem):
    return pltpu.CompilerParams(dimension_semantics=sem, vmem_limit_bytes=VMEM_LIMIT)


def _qkv_na_kernel(x_ref, g_ref, w_ref, o_ref, wbf_scr):
    @pl.when(pl.program_id(0) == 0)
    def _():
        for c in range(3 * D_MODEL // CAST_COLS):
            cols = slice(c * CAST_COLS, (c + 1) * CAST_COLS)
            wbf_scr[:, cols] = w_ref[:, cols].astype(_BF16)

    h = _rms(x_ref[...], g_ref[...]).astype(_BF16)
    for j in range(3):
        y = jnp.dot(h, wbf_scr[:, j * D_MODEL:(j + 1) * D_MODEL], preferred_element_type=_F32)
        if j == 0:
            y = y * QK_SCALE
        for c in range(N_PAIRS):
            o_ref[0, j * N_PAIRS + c] = y[:, c * LANES:(c + 1) * LANES].astype(_BF16)


def _qkv_na(x2d, gain, w):
    tiles_per_seq = SEQ // TM
    return pl.pallas_call(
        _qkv_na_kernel,
        grid=(N_TOK // TM,),
        in_specs=[
            pl.BlockSpec((TM, D_MODEL), lambda i: (i, 0)),
            _resident((1, D_MODEL)),
            _resident((D_MODEL, 3 * D_MODEL)),
        ],
        out_specs=pl.BlockSpec((1, 3 * N_PAIRS, TM, LANES),
                               lambda i: (i // tiles_per_seq, 0, i % tiles_per_seq, 0)),
        out_shape=jax.ShapeDtypeStruct((BATCH, 3 * N_PAIRS, SEQ, LANES), _BF16),
        scratch_shapes=[pltpu.VMEM((D_MODEL, 3 * D_MODEL), _BF16)],
        compiler_params=_params("arbitrary"),
        name="qkv_na",
    )(x2d, gain, w)


def _stack_heads(q, first_head):
    zero = jnp.zeros_like(q)
    return jnp.concatenate([jnp.where(first_head, q, zero), jnp.where(first_head, zero, q)], axis=0)


def _masked_probs(s, bias, cap):
    s = jnp.minimum(s + bias, cap)
    m = jnp.max(s, axis=-1, keepdims=True)
    return jnp.exp2(s - m).astype(_BF16), m


def _stacked_pv(p, v, first_head):
    rows = p.shape[0] // 2
    r = jnp.dot(p, jnp.concatenate([v, jnp.ones_like(v)], axis=1), preferred_element_type=_F32)
    num = jnp.where(first_head, r[:rows, :LANES], r[rows:, :LANES])
    den = jnp.where(first_head, r[:rows, LANES:], r[rows:, LANES:])
    return num * (1.0 / den), den


def _na_kernel(q_ref, k_ref, v_ref, rpb_ref, o_ref, bias_scr, cap_scr, p_scr):
    kh = NA_ROWS
    nk = kh * GRID_W
    per_iter = NA_ROWS_PER_ITER
    n_iter = GRID_ROWS // per_iter
    lane = lax.broadcasted_iota(jnp.int32, (GRID_W, LANES), 1)
    first_head = lane < HEAD_DIM

    @pl.when(pl.program_id(1) == 0)
    def _():
        n_row_off, n_col_off = 2 * kh - 1, 2 * NA_COLS - 1
        lane_off = lax.broadcasted_iota(jnp.int32, (1, LANES), 1) - (GRID_W - NA_COLS)
        diag_idx = jnp.clip(lane_off, 0, n_col_off - 1)
        for hh in range(2):
            for row_off in range(n_row_off):
                base = ((2 * pl.program_id(0) + hh) * n_row_off + row_off) * n_col_off
                diag = jnp.zeros((1, LANES), _F32)
                for c in range(n_col_off):
                    diag = jnp.where(diag_idx == c, rpb_ref[base + c], diag)
                diag = jnp.broadcast_to(diag, (GRID_W, LANES))
                tiles = [pltpu.roll(diag, 1 + GRID_W * (1 - half), 1, stride=1, stride_axis=0) for half in range(2)]
                for d in range(kh):
                    r = row_off - (kh - 1 - d)
                    if 0 <= r < kh:
                        half = slice((r % 2) * GRID_W, (r % 2 + 1) * GRID_W)
                        bias_scr[d, hh * GRID_W:(hh + 1) * GRID_W, r * GRID_W:(r + 1) * GRID_W] = tiles[r % 2][:, half]
        qcol = lax.broadcasted_iota(jnp.int32, (2 * GRID_W, LANES), 0) % GRID_W
        kcol = lax.broadcasted_iota(jnp.int32, (2 * GRID_W, LANES), 1) % GRID_W
        col_start = jnp.clip(qcol - NA_COLS // 2, 0, GRID_W - NA_COLS)
        col_ok = (kcol >= col_start) & (kcol < col_start + NA_COLS)
        cap_scr[...] = jnp.where(col_ok, jnp.finfo(_F32).max, NEG_INF).astype(_F32)

    def window(i):
        rs = jnp.clip(i - kh // 2, 0, GRID_ROWS - kh)
        return rs, pl.ds(pl.multiple_of(rs * GRID_W, GRID_W), nk)

    def query_rows(i):
        return pl.ds(pl.multiple_of(i * GRID_W, GRID_W), GRID_W)

    def scores(slot, u, i):
        rs, win = window(i)
        s = lax.dot_general(_stack_heads(q_ref[0, 0, query_rows(i), :], first_head), k_ref[0, 0, win, :],
                            (((1,), (1,)), ((), ())), preferred_element_type=_F32)
        cap = jnp.concatenate([cap_scr[...]] * (nk // LANES), axis=1)
        p_scr[slot, u], _ = _masked_probs(s, bias_scr[i - rs], cap)

    def outputs(slot, u, i):
        _, win = window(i)
        o, _ = _stacked_pv(p_scr[slot, u], v_ref[0, 0, win, :], first_head)
        o_ref[0, 0, query_rows(i), :] = o.astype(_BF16)

    for u in range(per_iter):
        scores(0, u, u)

    def body(it, carry):
        slot = it % 2
        for u in range(per_iter):
            outputs(slot, u, it * per_iter + u)
            scores(1 - slot, u, (it + 1) * per_iter + u)
        return carry

    lax.fori_loop(0, n_iter - 1, body, 0)
    for u in range(per_iter):
        outputs((n_iter - 1) % 2, u, (n_iter - 1) * per_iter + u)


def _na_attention(qkv, rpb_flat):
    slab = (1, 1, SEQ, LANES)
    nk = NA_ROWS * GRID_W
    return pl.pallas_call(
        _na_kernel,
        grid=(N_PAIRS, BATCH),
        in_specs=[
            pl.BlockSpec(slab, lambda p, b: (b, p, 0, 0)),
            pl.BlockSpec(slab, lambda p, b: (b, N_PAIRS + p, 0, 0)),
            pl.BlockSpec(slab, lambda p, b: (b, 2 * N_PAIRS + p, 0, 0)),
            pl.BlockSpec(memory_space=pltpu.SMEM),
        ],
        out_specs=pl.BlockSpec(slab, lambda p, b: (b, p, 0, 0)),
        out_shape=jax.ShapeDtypeStruct((BATCH, N_PAIRS, SEQ, LANES), _BF16),
        scratch_shapes=[pltpu.VMEM((NA_ROWS, 2 * GRID_W, nk), _F32),
                        pltpu.VMEM((2 * GRID_W, LANES), _F32),
                        pltpu.VMEM((2, NA_ROWS_PER_ITER, 2 * GRID_W, nk), _BF16)],
        compiler_params=_params("arbitrary", "arbitrary"),
        name="na_attention",
    )(qkv, qkv, qkv, rpb_flat)


def _mix_ffn_kernel(o_ref, x_ref, wo_ref, gmix_ref, gpre_ref, wg_ref, wu_ref, wd_ref, gpost_ref, out_ref,
                    wo_scr, wg_scr, wu_scr, wd_scr):
    i = pl.program_id(0)

    @pl.when(i < WEIGHT_STEPS)
    def _():
        for src, dst in ((wo_ref, wo_scr), (wg_ref, wg_scr), (wu_ref, wu_scr), (wd_ref, wd_scr)):
            n = src.shape[0]
            dst[pl.ds(pl.multiple_of(i * n, n), n), :] = src[...].astype(_BF16)

    @pl.when(i >= WEIGHT_STEPS)
    def _():
        o = jnp.concatenate([o_ref[0, c] for c in range(N_PAIRS)], axis=-1)
        y = jnp.dot(o, wo_scr[...], preferred_element_type=_F32)
        x = x_ref[...] + _rms(y, gmix_ref[...])
        h = _rms(x, gpre_ref[...]).astype(_BF16)
        y = None
        for start, size in FF_SPLITS:
            g = jnp.dot(h, wg_scr[:, start:start + size], preferred_element_type=_F32)
            u = jnp.dot(h, wu_scr[:, start:start + size], preferred_element_type=_F32)
            a = ((g * jax.nn.sigmoid(g)) * u).astype(_BF16)
            part = jnp.dot(a, wd_scr[start:start + size, :], preferred_element_type=_F32)
            y = part if y is None else y + part
        out_ref[...] = x + _rms(y, gpost_ref[...])


def _mix_ffn(layer, o_cbm, x2d, wo, gmix, gpre, wg, wu, wd, gpost):
    tiles_per_seq = SEQ // TM

    def tile(i):
        return jnp.maximum(i - WEIGHT_STEPS, 0)

    def chunk(i):
        return jnp.minimum(i, WEIGHT_STEPS - 1)

    def weight_chunk(rows, cols):
        return pl.BlockSpec((rows // WEIGHT_STEPS, cols), lambda i: (chunk(i), 0))

    def layer_weight_chunk(rows, cols):
        return pl.BlockSpec((None, rows // WEIGHT_STEPS, cols), lambda i: (layer, chunk(i), 0))

    return pl.pallas_call(
        _mix_ffn_kernel,
        grid=(WEIGHT_STEPS + N_TOK // TM,),
        in_specs=[
            pl.BlockSpec((1, N_PAIRS, TM, LANES), lambda i: (tile(i) // tiles_per_seq, 0, tile(i) % tiles_per_seq, 0)),
            pl.BlockSpec((TM, D_MODEL), lambda i: (tile(i), 0)),
            weight_chunk(D_MODEL, D_MODEL),
            _resident((1, D_MODEL)),
            _resident((1, D_MODEL)),
            layer_weight_chunk(D_MODEL, D_FF),
            layer_weight_chunk(D_MODEL, D_FF),
            layer_weight_chunk(D_FF, D_MODEL),
            _resident((1, D_MODEL)),
        ],
        out_specs=pl.BlockSpec((TM, D_MODEL), lambda i: (tile(i), 0)),
        out_shape=jax.ShapeDtypeStruct((N_TOK, D_MODEL), _F32),
        scratch_shapes=[pltpu.VMEM((D_MODEL, D_MODEL), _BF16), pltpu.VMEM((D_MODEL, D_FF), _BF16),
                        pltpu.VMEM((D_MODEL, D_FF), _BF16), pltpu.VMEM((D_FF, D_MODEL), _BF16)],
        compiler_params=_params("arbitrary"),
        name="mix_ffn",
    )(o_cbm, x2d, wo, gmix, gpre, wg, wu, wd, gpost)


def _qkv_dil_kernel(x_ref, g_ref, w_ref, o_ref, h32_scr, hperm_scr):
    j = pl.program_id(1)
    tiles_per_group = 3 * D_MODEL // DIL_TN

    @pl.when(j == 0)
    def _():
        for t in range(SEQ // TM):
            rows = slice(t * TM, (t + 1) * TM)
            h = _rms(x_ref[0, rows, :], g_ref[...])
            for c in range(N_PAIRS):
                h32_scr[c, rows, :] = h[:, c * LANES:(c + 1) * LANES]

    for grp, (_, dil) in enumerate(DIL_GROUPS):
        @pl.when(j == grp * tiles_per_group)
        def _(dil=dil):
            sub = SEQ // dil
            if dil <= SUBLANES:
                for r in range(dil):
                    for c in range(N_PAIRS):
                        val = h32_scr[c] if dil == 1 else h32_scr[c, pl.ds(r, sub, stride=dil), :]
                        hperm_scr[r * sub:(r + 1) * sub, c * LANES:(c + 1) * LANES] = val.astype(_BF16)
            else:
                per = MXU_DEPTH // dil
                out_row = lax.broadcasted_iota(jnp.int32, (MXU_DEPTH, MXU_DEPTH), 0)
                in_row = lax.broadcasted_iota(jnp.int32, (MXU_DEPTH, MXU_DEPTH), 1)
                pick = (in_row == dil * (out_row % per) + out_row // per).astype(_BF16)
                for a in range(SEQ // MXU_DEPTH):
                    rows = slice(a * MXU_DEPTH, (a + 1) * MXU_DEPTH)
                    chunk = jnp.concatenate([h32_scr[c, rows, :].astype(_BF16) for c in range(N_PAIRS)], axis=1)
                    picked = jnp.dot(pick, chunk, preferred_element_type=_F32).astype(_BF16)
                    for r in range(dil):
                        hperm_scr[r * sub + a * per:r * sub + (a + 1) * per, :] = picked[r * per:(r + 1) * per]

    y = jnp.dot(hperm_scr[...], w_ref[...].astype(_BF16), preferred_element_type=_F32)
    is_q_tile = j % tiles_per_group < D_MODEL // DIL_TN
    y = y * jnp.where(is_q_tile, QK_SCALE, 1.0)
    for c in range(DIL_TN // LANES):
        o_ref[0, c] = y[:, c * LANES:(c + 1) * LANES].astype(_BF16)


def _qkv_dil(x3d, gain, w):
    n_col = N_GROUPS * 3 * D_MODEL
    blocks_per_tile = DIL_TN // LANES
    return pl.pallas_call(
        _qkv_dil_kernel,
        grid=(BATCH, n_col // DIL_TN),
        in_specs=[
            pl.BlockSpec((1, SEQ, D_MODEL), lambda b, j: (b, 0, 0)),
            _resident((1, D_MODEL)),
            pl.BlockSpec((D_MODEL, DIL_TN), lambda b, j: (0, j)),
        ],
        out_specs=pl.BlockSpec((1, blocks_per_tile, SEQ, LANES), lambda b, j: (b, j, 0, 0)),
        out_shape=jax.ShapeDtypeStruct((BATCH, n_col // LANES, SEQ, LANES), _BF16),
        scratch_shapes=[pltpu.VMEM((N_PAIRS, SEQ, LANES), _F32), pltpu.VMEM((SEQ, D_MODEL), _BF16)],
        compiler_params=_params("parallel", "arbitrary"),
        name="qkv_dil",
    )(x3d, gain, w)


def _dil_kernel(slopes_ref, q0, k0, v0, q1, k1, v1, q2, k2, v2, o_ref, og_scr, lg_scr, bias_scr, cap_scr, p_scr,
                m_scr):
    pair = pl.program_id(0)
    qkv_refs = ((q0, k0, v0), (q1, k1, v1), (q2, k2, v2))
    lane = lax.broadcasted_iota(jnp.int32, (BAND_BLOCK, LANES), 1)
    first_head = lane < HEAD_DIM
    max_width = 2 * BAND_BLOCK
    per_iter = DIL_BLOCKS_PER_ITER
    n_iter = SEQ // (BAND_BLOCK * per_iter)

    @pl.when(pl.program_id(1) == 0)
    def _():
        rel = (lax.broadcasted_iota(jnp.int32, (BAND_BLOCK, max_width), 0)
               - lax.broadcasted_iota(jnp.int32, (BAND_BLOCK, max_width), 1))
        for grp, (window, dil) in enumerate(DIL_GROUPS):
            radius = window // (2 * dil)
            for var, off in enumerate(DIL_WINDOW_OFFSETS):
                dist = jnp.abs(rel + off)
                distf = dist.astype(_F32)
                cap_scr[grp, var] = jnp.where(dist <= radius, jnp.finfo(_F32).max, NEG_INF).astype(_F32)
                for hh in range(2):
                    head_rows = slice(hh * BAND_BLOCK, (hh + 1) * BAND_BLOCK)
                    bias_scr[grp, var, head_rows, :] = -((slopes_ref[2 * pair + hh] * dil) * distf) * LOG2E

    def stage_fns(grp):
        window, dil = DIL_GROUPS[grp]
        q_ref, k_ref, v_ref = qkv_refs[grp]
        radius = window // (2 * dil)
        sub = SEQ // dil
        blocks_per_sub = sub // BAND_BLOCK
        width = min(max_width, sub)

        def place(t):
            res = t // blocks_per_sub
            c = t % blocks_per_sub
            kstart = jnp.clip(c * BAND_BLOCK - radius, 0, sub - width)
            return res, c, pl.ds(pl.multiple_of(res * sub + kstart, radius), width)

        def scores(slot, u, t):
            _, c, key_rows = place(t)
            q = q_ref[0, 0, pl.ds(pl.multiple_of(t * BAND_BLOCK, BAND_BLOCK), BAND_BLOCK), :]
            s = lax.dot_general(_stack_heads(q, first_head), k_ref[0, 0, key_rows, :], (((1,), (1,)), ((), ())),
                                preferred_element_type=_F32)
            var = 0 if blocks_per_sub == 1 else jnp.int32(c > 0) + jnp.int32(c == blocks_per_sub - 1)
            cap = cap_scr[grp, var, :, :width]
            p, m = _masked_probs(s, bias_scr[grp, var, :, :width], jnp.concatenate([cap, cap], axis=0))
            p_scr[slot, u, :, :width] = p
            m_scr[slot, u] = jnp.where(first_head, m[:BAND_BLOCK], m[BAND_BLOCK:])

        def outputs(slot, u, t):
            res, c, key_rows = place(t)
            o, den = _stacked_pv(p_scr[slot, u, :, :width], v_ref[0, 0, key_rows, :], first_head)
            if dil == 1:
                rows = pl.ds(pl.multiple_of(t * BAND_BLOCK, BAND_BLOCK), BAND_BLOCK)
            else:
                rows = pl.ds(c * (BAND_BLOCK * dil) + res, BAND_BLOCK, stride=dil)
            og_scr[grp, rows, :] = o
            lg_scr[grp, rows, :] = m_scr[slot, u] + jnp.log2(den)

        return scores, outputs

    fns = [stage_fns(grp) for grp in range(N_GROUPS)]
    for u in range(per_iter):
        fns[0][0](0, u, u)
    for grp in range(N_GROUPS):
        scores, outputs = fns[grp]
        first_slot = (grp * n_iter) % 2

        def body(it, carry, scores=scores, outputs=outputs, first_slot=first_slot):
            slot = (first_slot + it) % 2
            for u in range(per_iter):
                outputs(slot, u, it * per_iter + u)
                scores(1 - slot, u, (it + 1) * per_iter + u)
            return carry

        if n_iter > 1:
            lax.fori_loop(0, n_iter - 1, body, 0)
        last_slot = (first_slot + n_iter - 1) % 2
        for u in range(per_iter):
            outputs(last_slot, u, (n_iter - 1) * per_iter + u)
            if grp + 1 < N_GROUPS:
                fns[grp + 1][0](1 - last_slot, u, u)

    chunk = 256
    for t in range(SEQ // chunk):
        rows = slice(t * chunk, (t + 1) * chunk)
        lse = [lg_scr[grp, rows, :] for grp in range(N_GROUPS)]
        mx = jnp.maximum(jnp.maximum(lse[0], lse[1]), lse[2])
        e = [jnp.exp2(x - mx) for x in lse]
        den = e[0] + e[1] + e[2]
        num = e[0] * og_scr[0, rows, :] + e[1] * og_scr[1, rows, :] + e[2] * og_scr[2, rows, :]
        o_ref[0, 0, rows, :] = (num / den).astype(_BF16)


def _dil_attention(qkv, slopes):
    slab = (1, 1, SEQ, LANES)

    def spec(grp, which):
        base = (grp * 3 + which) * N_PAIRS
        return pl.BlockSpec(slab, lambda p, b: (b, base + p, 0, 0))

    return pl.pallas_call(
        _dil_kernel,
        grid=(N_PAIRS, BATCH),
        in_specs=[pl.BlockSpec(memory_space=pltpu.SMEM)]
        + [spec(grp, which) for grp in range(N_GROUPS) for which in range(3)],
        out_specs=pl.BlockSpec(slab, lambda p, b: (b, p, 0, 0)),
        out_shape=jax.ShapeDtypeStruct((BATCH, N_PAIRS, SEQ, LANES), _BF16),
        scratch_shapes=[
            pltpu.VMEM((N_GROUPS, SEQ, LANES), _F32),
            pltpu.VMEM((N_GROUPS, SEQ, LANES), _F32),
            pltpu.VMEM((N_GROUPS, len(DIL_WINDOW_OFFSETS), 2 * BAND_BLOCK, 2 * BAND_BLOCK), _F32),
            pltpu.VMEM((N_GROUPS, len(DIL_WINDOW_OFFSETS), BAND_BLOCK, 2 * BAND_BLOCK), _F32),
            pltpu.VMEM((2, DIL_BLOCKS_PER_ITER, 2 * BAND_BLOCK, 2 * BAND_BLOCK), _BF16),
            pltpu.VMEM((2, DIL_BLOCKS_PER_ITER, BAND_BLOCK, LANES), _F32),
        ],
        compiler_params=_params("arbitrary", "arbitrary"),
        name="dil_attention",
    )(slopes, *([qkv] * 9))


def kernel(x, norm_mix_pre, norm_mix_post, norm_ffn_pre, norm_ffn_post, na_w_qkv, na_w_o, na_rpb, dil_w_qkv, dil_w_o,
           ffn_w_gate, ffn_w_up, ffn_w_down):
    x2d = x.reshape(N_TOK, D_MODEL)
    slopes = 2.0 ** (-8.0 * jnp.arange(1, N_HEADS + 1, dtype=_F32) / N_HEADS)

    def gain(g, layer):
        return g[layer].reshape(1, D_MODEL)

    for layer in range(DEPTH):
        j = layer // 2
        if layer % 2 == 0:
            qkv = _qkv_na(x2d, gain(norm_mix_pre, layer), na_w_qkv[j])
            o = _na_attention(qkv, (na_rpb[j].astype(_F32) * LOG2E).reshape(-1))
            w_o = na_w_o[j]
        else:
            qkv = _qkv_dil(x2d.reshape(BATCH, SEQ, D_MODEL), gain(norm_mix_pre, layer), dil_w_qkv[j])
            o = _dil_attention(qkv, slopes)
            w_o = dil_w_o[j]
        x2d = _mix_ffn(layer, o, x2d, w_o, gain(norm_mix_post, layer), gain(norm_ffn_pre, layer),
                       ffn_w_gate, ffn_w_up, ffn_w_down, gain(norm_ffn_post, layer))
    return x2d.reshape(BATCH, SEQ, D_MODEL)
```

```python
import jax
import jax.numpy as jnp
from jax import lax
from jax.experimental import pallas as pl
from jax.experimental.pallas import tpu as pltpu

D_MODEL = 1024
BATCH = 8
SEQ = 2048
DEPTH = 2
N_HEADS = 16
HEAD_DIM = 64
GRID_W = 64
GRID_ROWS = SEQ // GRID_W
NA_ROWS = 8
NA_COLS = 16
DIL_GROUPS = ((128, 1), (512, 4), (2048, 16))
N_GROUPS = len(DIL_GROUPS)
BAND_BLOCK = 128
D_FF = 2816
RMS_EPS = 1e-6
NEG_INF = -1e30
LOG2E = 1.4426950408889634
QK_SCALE = HEAD_DIM ** -0.5 * LOG2E

LANES = 128
N_PAIRS = D_MODEL // LANES
N_TOK = BATCH * SEQ
TM = 512
FF_SPLITS = ((0, 1536), (1536, 1280))
DIL_TN = 1024
CAST_COLS = 512
WEIGHT_STEPS = 8
SUBLANES = 8
MXU_DEPTH = 256
NA_ROWS_PER_ITER = 16
DIL_BLOCKS_PER_ITER = 8
DIL_WINDOW_OFFSETS = (0, BAND_BLOCK // 2, BAND_BLOCK)
VMEM_LIMIT = 56 * 1024 * 1024

_BF16 = jnp.bfloat16
_F32 = jnp.float32


def _rms(x, g):
    ms = jnp.mean(x * x, axis=-1, keepdims=True)
    return (x * lax.rsqrt(ms + RMS_EPS)) * g


def _resident(shape):
    return pl.BlockSpec(shape, lambda *_: (0,) * len(shape), pipeline_mode=pl.Buffered(1))


def _params(*sem):
    return pltpu.CompilerParams(dimension_semantics=sem, vmem_limit_bytes=VMEM_LIMIT)


def _qkv_na_kernel(x_ref, g_ref, w_ref, o_ref, wbf_scr):
    @pl.when(pl.program_id(0) == 0)
    def _():
        for c in range(3 * D_MODEL // CAST_COLS):
            cols = slice(c * CAST_COLS, (c + 1) * CAST_COLS)
            wbf_scr[:, cols] = w_ref[:, cols].astype(_BF16)

    h = _rms(x_ref[...], g_ref[...]).astype(_BF16)
    for j in range(3):
        y = jnp.dot(h, wbf_scr[:, j * D_MODEL:(j + 1) * D_MODEL], preferred_element_type=_F32)
        if j == 0:
            y = y * QK_SCALE
        for c in range(N_PAIRS):
            o_ref[0, j * N_PAIRS + c] = y[:, c * LANES:(c + 1) * LANES].astype(_BF16)


def _qkv_na(x2d, gain, w):
    tiles_per_seq = SEQ // TM
    return pl.pallas_call(
        _qkv_na_kernel,
        grid=(N_TOK // TM,),
        in_specs=[
            pl.BlockSpec((TM, D_MODEL), lambda i: (i, 0)),
            _resident((1, D_MODEL)),
            _resident((D_MODEL, 3 * D_MODEL)),
        ],
        out_specs=pl.BlockSpec((1, 3 * N_PAIRS, TM, LANES),
                               lambda i: (i // tiles_per_seq, 0, i % tiles_per_seq, 0)),
        out_shape=jax.ShapeDtypeStruct((BATCH, 3 * N_PAIRS, SEQ, LANES), _BF16),
        scratch_shapes=[pltpu.VMEM((D_MODEL, 3 * D_MODEL), _BF16)],
        compiler_params=_params("arbitrary"),
        name="qkv_na",
    )(x2d, gain, w)


def _stack_heads(q, first_head):
    zero = jnp.zeros_like(q)
    return jnp.concatenate([jnp.where(first_head, q, zero), jnp.where(first_head, zero, q)], axis=0)


def _masked_probs(s, bias, cap):
    s = jnp.minimum(s + bias, cap)
    m = jnp.max(s, axis=-1, keepdims=True)
    return jnp.exp2(s - m).astype(_BF16), m


def _stacked_pv(p, v, first_head):
    rows = p.shape[0] // 2
    r = jnp.dot(p, jnp.concatenate([v, jnp.ones_like(v)], axis=1), preferred_element_type=_F32)
    num = jnp.where(first_head, r[:rows, :LANES], r[rows:, :LANES])
    den = jnp.where(first_head, r[:rows, LANES:], r[rows:, LANES:])
    return num * (1.0 / den), den


def _na_kernel(q_ref, k_ref, v_ref, rpb_ref, o_ref, bias_scr, cap_scr, p_scr):
    kh = NA_ROWS
    nk = kh * GRID_W
    per_iter = NA_ROWS_PER_ITER
    n_iter = GRID_ROWS // per_iter
    lane = lax.broadcasted_iota(jnp.int32, (GRID_W, LANES), 1)
    first_head = lane < HEAD_DIM

    @pl.when(pl.program_id(1) == 0)
    def _():
        n_row_off, n_col_off = 2 * kh - 1, 2 * NA_COLS - 1
        lane_off = lax.broadcasted_iota(jnp.int32, (1, LANES), 1) - (GRID_W - NA_COLS)
        diag_idx = jnp.clip(lane_off, 0, n_col_off - 1)
        for hh in range(2):
            for row_off in range(n_row_off):
                base = ((2 * pl.program_id(0) + hh) * n_row_off + row_off) * n_col_off
                diag = jnp.zeros((1, LANES), _F32)
                for c in range(n_col_off):
                    diag = jnp.where(diag_idx == c, rpb_ref[base + c], diag)
                diag = jnp.broadcast_to(diag, (GRID_W, LANES))
                tiles = [pltpu.roll(diag, 1 + GRID_W * (1 - half), 1, stride=1, stride_axis=0) for half in range(2)]
                for d in range(kh):
                    r = row_off - (kh - 1 - d)
                    if 0 <= r < kh:
                        half = slice((r % 2) * GRID_W, (r % 2 + 1) * GRID_W)
                        bias_scr[d, hh * GRID_W:(hh + 1) * GRID_W, r * GRID_W:(r + 1) * GRID_W] = tiles[r % 2][:, half]
        qcol = lax.broadcasted_iota(jnp.int32, (2 * GRID_W, LANES), 0) % GRID_W
        kcol = lax.broadcasted_iota(jnp.int32, (2 * GRID_W, LANES), 1) % GRID_W
        col_start = jnp.clip(qcol - NA_COLS // 2, 0, GRID_W - NA_COLS)
        col_ok = (kcol >= col_start) & (kcol < col_start + NA_COLS)
        cap_scr[...] = jnp.where(col_ok, jnp.finfo(_F32).max, NEG_INF).astype(_F32)

    def window(i):
        rs = jnp.clip(i - kh // 2, 0, GRID_ROWS - kh)
        return rs, pl.ds(pl.multiple_of(rs * GRID_W, GRID_W), nk)

    def query_rows(i):
        return pl.ds(pl.multiple_of(i * GRID_W, GRID_W), GRID_W)

    def scores(slot, u, i):
        rs, win = window(i)
        s = lax.dot_general(_stack_heads(q_ref[0, 0, query_rows(i), :], first_head), k_ref[0, 0, win, :],
                            (((1,), (1,)), ((), ())), preferred_element_type=_F32)
        cap = jnp.concatenate([cap_scr[...]] * (nk // LANES), axis=1)
        p_scr[slot, u], _ = _masked_probs(s, bias_scr[i - rs], cap)

    def outputs(slot, u, i):
        _, win = window(i)
        o, _ = _stacked_pv(p_scr[slot, u], v_ref[0, 0, win, :], first_head)
        o_ref[0, 0, query_rows(i), :] = o.astype(_BF16)

    for u in range(per_iter):
        scores(0, u, u)

    def body(it, carry):
        slot = it % 2
        for u in range(per_iter):
            outputs(slot, u, it * per_iter + u)
            scores(1 - slot, u, (it + 1) * per_iter + u)
        return carry

    lax.fori_loop(0, n_iter - 1, body, 0)
    for u in range(per_iter):
        outputs((n_iter - 1) % 2, u, (n_iter - 1) * per_iter + u)


def _na_attention(qkv, rpb_flat):
    slab = (1, 1, SEQ, LANES)
    nk = NA_ROWS * GRID_W
    return pl.pallas_call(
        _na_kernel,
        grid=(N_PAIRS, BATCH),
        in_specs=[
            pl.BlockSpec(slab, lambda p, b: (b, p, 0, 0)),
            pl.BlockSpec(slab, lambda p, b: (b, N_PAIRS + p, 0, 0)),
            pl.BlockSpec(slab, lambda p, b: (b, 2 * N_PAIRS + p, 0, 0)),
            pl.BlockSpec(memory_space=pltpu.SMEM),
        ],
        out_specs=pl.BlockSpec(slab, lambda p, b: (b, p, 0, 0)),
        out_shape=jax.ShapeDtypeStruct((BATCH, N_PAIRS, SEQ, LANES), _BF16),
        scratch_shapes=[pltpu.VMEM((NA_ROWS, 2 * GRID_W, nk), _F32),
                        pltpu.VMEM((2 * GRID_W, LANES), _F32),
                        pltpu.VMEM((2, NA_ROWS_PER_ITER, 2 * GRID_W, nk), _BF16)],
        compiler_params=_params("arbitrary", "arbitrary"),
        name="na_attention",
    )(qkv, qkv, qkv, rpb_flat)


def _mix_ffn_kernel(o_ref, x_ref, wo_ref, gmix_ref, gpre_ref, wg_ref, wu_ref, wd_ref, gpost_ref, out_ref,
                    wo_scr, wg_scr, wu_scr, wd_scr):
    i = pl.program_id(0)

    @pl.when(i < WEIGHT_STEPS)
    def _():
        for src, dst in ((wo_ref, wo_scr), (wg_ref, wg_scr), (wu_ref, wu_scr), (wd_ref, wd_scr)):
            n = src.shape[0]
            dst[pl.ds(pl.multiple_of(i * n, n), n), :] = src[...].astype(_BF16)

    @pl.when(i >= WEIGHT_STEPS)
    def _():
        halves = [slice(t * (TM // 2), (t + 1) * (TM // 2)) for t in range(2)]
        ys = [jnp.dot(jnp.concatenate([o_ref[0, c, rows, :] for c in range(N_PAIRS)], axis=-1), wo_scr[...],
                      preferred_element_type=_F32) for rows in halves]
        xs, hs = [], []
        for rows, y in zip(halves, ys):
            x = x_ref[rows, :] + _rms(y, gmix_ref[...])
            xs.append(x)
            hs.append(_rms(x, gpre_ref[...]).astype(_BF16))
        acc = [None, None]
        for start, size in FF_SPLITS:
            gu = [(jnp.dot(h, wg_scr[:, start:start + size], preferred_element_type=_F32),
                   jnp.dot(h, wu_scr[:, start:start + size], preferred_element_type=_F32)) for h in hs]
            for t, (g, u) in enumerate(gu):
                a = ((g * jax.nn.sigmoid(g)) * u).astype(_BF16)
                part = jnp.dot(a, wd_scr[start:start + size, :], preferred_element_type=_F32)
                acc[t] = part if acc[t] is None else acc[t] + part
        for rows, x, y in zip(halves, xs, acc):
            out_ref[rows, :] = x + _rms(y, gpost_ref[...])


def _mix_ffn(layer, o_cbm, x2d, wo, gmix, gpre, wg, wu, wd, gpost):
    tiles_per_seq = SEQ // TM

    def tile(i):
        return jnp.maximum(i - WEIGHT_STEPS, 0)

    def chunk(i):
        return jnp.minimum(i, WEIGHT_STEPS - 1)

    def weight_chunk(rows, cols):
        return pl.BlockSpec((rows // WEIGHT_STEPS, cols), lambda i: (chunk(i), 0))

    def layer_weight_chunk(rows, cols):
        return pl.BlockSpec((None, rows // WEIGHT_STEPS, cols), lambda i: (layer, chunk(i), 0))

    return pl.pallas_call(
        _mix_ffn_kernel,
        grid=(WEIGHT_STEPS + N_TOK // TM,),
        in_specs=[
            pl.BlockSpec((1, N_PAIRS, TM, LANES), lambda i: (tile(i) // tiles_per_seq, 0, tile(i) % tiles_per_seq, 0)),
            pl.BlockSpec((TM, D_MODEL), lambda i: (tile(i), 0)),
            weight_chunk(D_MODEL, D_MODEL),
            _resident((1, D_MODEL)),
            _resident((1, D_MODEL)),
            layer_weight_chunk(D_MODEL, D_FF),
            layer_weight_chunk(D_MODEL, D_FF),
            layer_weight_chunk(D_FF, D_MODEL),
            _resident((1, D_MODEL)),
        ],
        out_specs=pl.BlockSpec((TM, D_MODEL), lambda i: (tile(i), 0)),
        out_shape=jax.ShapeDtypeStruct((N_TOK, D_MODEL), _F32),
        scratch_shapes=[pltpu.VMEM((D_MODEL, D_MODEL), _BF16), pltpu.VMEM((D_MODEL, D_FF), _BF16),
                        pltpu.VMEM((D_MODEL, D_FF), _BF16), pltpu.VMEM((D_FF, D_MODEL), _BF16)],
        compiler_params=_params("arbitrary"),
        name="mix_ffn",
    )(o_cbm, x2d, wo, gmix, gpre, wg, wu, wd, gpost)


def _qkv_dil_kernel(x_ref, g_ref, w_ref, o_ref, h32_scr, hperm_scr):
    j = pl.program_id(1)
    tiles_per_group = 3 * D_MODEL // DIL_TN

    @pl.when(j == 0)
    def _():
        for t in range(SEQ // TM):
            rows = slice(t * TM, (t + 1) * TM)
            h = _rms(x_ref[0, rows, :], g_ref[...])
            for c in range(N_PAIRS):
                h32_scr[c, rows, :] = h[:, c * LANES:(c + 1) * LANES]

    for grp, (_, dil) in enumerate(DIL_GROUPS):
        @pl.when(j == grp * tiles_per_group)
        def _(dil=dil):
            sub = SEQ // dil
            if dil <= SUBLANES:
                for r in range(dil):
                    for c in range(N_PAIRS):
                        val = h32_scr[c] if dil == 1 else h32_scr[c, pl.ds(r, sub, stride=dil), :]
                        hperm_scr[r * sub:(r + 1) * sub, c * LANES:(c + 1) * LANES] = val.astype(_BF16)
            else:
                per = MXU_DEPTH // dil
                out_row = lax.broadcasted_iota(jnp.int32, (MXU_DEPTH, MXU_DEPTH), 0)
                in_row = lax.broadcasted_iota(jnp.int32, (MXU_DEPTH, MXU_DEPTH), 1)
                pick = (in_row == dil * (out_row % per) + out_row // per).astype(_BF16)
                for a in range(SEQ // MXU_DEPTH):
                    rows = slice(a * MXU_DEPTH, (a + 1) * MXU_DEPTH)
                    chunk = jnp.concatenate([h32_scr[c, rows, :].astype(_BF16) for c in range(N_PAIRS)], axis=1)
                    picked = jnp.dot(pick, chunk, preferred_element_type=_F32).astype(_BF16)
                    for r in range(dil):
                        hperm_scr[r * sub + a * per:r * sub + (a + 1) * per, :] = picked[r * per:(r + 1) * per]

    y = jnp.dot(hperm_scr[...], w_ref[...].astype(_BF16), preferred_element_type=_F32)
    is_q_tile = j % tiles_per_group < D_MODEL // DIL_TN
    y = y * jnp.where(is_q_tile, QK_SCALE, 1.0)
    for c in range(DIL_TN // LANES):
        o_ref[0, c] = y[:, c * LANES:(c + 1) * LANES].astype(_BF16)


def _qkv_dil(x3d, gain, w):
    n_col = N_GROUPS * 3 * D_MODEL
    blocks_per_tile = DIL_TN // LANES
    return pl.pallas_call(
        _qkv_dil_kernel,
        grid=(BATCH, n_col // DIL_TN),
        in_specs=[
            pl.BlockSpec((1, SEQ, D_MODEL), lambda b, j: (b, 0, 0)),
            _resident((1, D_MODEL)),
            pl.BlockSpec((D_MODEL, DIL_TN), lambda b, j: (0, j)),
        ],
        out_specs=pl.BlockSpec((1, blocks_per_tile, SEQ, LANES), lambda b, j: (b, j, 0, 0)),
        out_shape=jax.ShapeDtypeStruct((BATCH, n_col // LANES, SEQ, LANES), _BF16),
        scratch_shapes=[pltpu.VMEM((N_PAIRS, SEQ, LANES), _F32), pltpu.VMEM((SEQ, D_MODEL), _BF16)],
        compiler_params=_params("parallel", "arbitrary"),
        name="qkv_dil",
    )(x3d, gain, w)


def _dil_kernel(slopes_ref, q0, k0, v0, q1, k1, v1, q2, k2, v2, o_ref, og_scr, lg_scr, bias_scr, cap_scr, p_scr,
                m_scr):
    pair = pl.program_id(0)
    qkv_refs = ((q0, k0, v0), (q1, k1, v1), (q2, k2, v2))
    lane = lax.broadcasted_iota(jnp.int32, (BAND_BLOCK, LANES), 1)
    first_head = lane < HEAD_DIM
    max_width = 2 * BAND_BLOCK
    per_iter = DIL_BLOCKS_PER_ITER
    n_iter = SEQ // (BAND_BLOCK * per_iter)

    @pl.when(pl.program_id(1) == 0)
    def _():
        rel = (lax.broadcasted_iota(jnp.int32, (BAND_BLOCK, max_width), 0)
               - lax.broadcasted_iota(jnp.int32, (BAND_BLOCK, max_width), 1))
        for grp, (window, dil) in enumerate(DIL_GROUPS):
            radius = window // (2 * dil)
            for var, off in enumerate(DIL_WINDOW_OFFSETS):
                dist = jnp.abs(rel + off)
                distf = dist.astype(_F32)
                cap_scr[grp, var] = jnp.where(dist <= radius, jnp.finfo(_F32).max, NEG_INF).astype(_F32)
                for hh in range(2):
                    head_rows = slice(hh * BAND_BLOCK, (hh + 1) * BAND_BLOCK)
                    bias_scr[grp, var, head_rows, :] = -((slopes_ref[2 * pair + hh] * dil) * distf) * LOG2E

    def stage_fns(grp):
        window, dil = DIL_GROUPS[grp]
        q_ref, k_ref, v_ref = qkv_refs[grp]
        radius = window // (2 * dil)
        sub = SEQ // dil
        blocks_per_sub = sub // BAND_BLOCK
        width = min(max_width, sub)

        def place(t):
            res = t // blocks_per_sub
            c = t % blocks_per_sub
            kstart = jnp.clip(c * BAND_BLOCK - radius, 0, sub - width)
            return res, c, pl.ds(pl.multiple_of(res * sub + kstart, radius), width)

        def scores(slot, u, t):
            _, c, key_rows = place(t)
            q = q_ref[0, 0, pl.ds(pl.multiple_of(t * BAND_BLOCK, BAND_BLOCK), BAND_BLOCK), :]
            s = lax.dot_general(_stack_heads(q, first_head), k_ref[0, 0, key_rows, :], (((1,), (1,)), ((), ())),
                                preferred_element_type=_F32)
            var = 0 if blocks_per_sub == 1 else jnp.int32(c > 0) + jnp.int32(c == blocks_per_sub - 1)
            cap = cap_scr[grp, var, :, :width]
            p, m = _masked_probs(s, bias_scr[grp, var, :, :width], jnp.concatenate([cap, cap], axis=0))
            p_scr[slot, u, :, :width] = p
            m_scr[slot, u] = jnp.where(first_head, m[:BAND_BLOCK], m[BAND_BLOCK:])

        def outputs(slot, u, t):
            res, c, key_rows = place(t)
            o, den = _stacked_pv(p_scr[slot, u, :, :width], v_ref[0, 0, key_rows, :], first_head)
            if dil == 1:
                rows = pl.ds(pl.multiple_of(t * BAND_BLOCK, BAND_BLOCK), BAND_BLOCK)
            else:
                rows = pl.ds(c * (BAND_BLOCK * dil) + res, BAND_BLOCK, stride=dil)
            og_scr[grp, rows, :] = o
            lg_scr[grp, rows, :] = m_scr[slot, u] + jnp.log2(den)

        return scores, outputs

    fns = [stage_fns(grp) for grp in range(N_GROUPS)]
    for u in range(per_iter):
        fns[0][0](0, u, u)
    for grp in range(N_GROUPS):
        scores, outputs = fns[grp]
        first_slot = (grp * n_iter) % 2

        def body(it, carry, scores=scores, outputs=outputs, first_slot=first_slot):
            slot = (first_slot + it) % 2
            for u in range(per_iter):
                outputs(slot, u, it * per_iter + u)
                scores(1 - slot, u, (it + 1) * per_iter + u)
            return carry

        if n_iter > 1:
            lax.fori_loop(0, n_iter - 1, body, 0)
        last_slot = (first_slot + n_iter - 1) % 2
        for u in range(per_iter):
            outputs(last_slot, u, (n_iter - 1) * per_iter + u)
            if grp + 1 < N_GROUPS:
                fns[grp + 1][0](1 - last_slot, u, u)

    chunk = 256
    for t in range(SEQ // chunk):
        rows = slice(t * chunk, (t + 1) * chunk)
        lse = [lg_scr[grp, rows, :] for grp in range(N_GROUPS)]
        mx = jnp.maximum(jnp.maximum(lse[0], lse[1]), lse[2])
        e = [jnp.exp2(x - mx) for x in lse]
        den = e[0] + e[1] + e[2]
        num = e[0] * og_scr[0, rows, :] + e[1] * og_scr[1, rows, :] + e[2] * og_scr[2, rows, :]
        o_ref[0, 0, rows, :] = (num / den).astype(_BF16)


def _dil_attention(qkv, slopes):
    slab = (1, 1, SEQ, LANES)

    def spec(grp, which):
        base = (grp * 3 + which) * N_PAIRS
        return pl.BlockSpec(slab, lambda p, b: (b, base + p, 0, 0))

    return pl.pallas_call(
        _dil_kernel,
        grid=(N_PAIRS, BATCH),
        in_specs=[pl.BlockSpec(memory_space=pltpu.SMEM)]
        + [spec(grp, which) for grp in range(N_GROUPS) for which in range(3)],
        out_specs=pl.BlockSpec(slab, lambda p, b: (b, p, 0, 0)),
        out_shape=jax.ShapeDtypeStruct((BATCH, N_PAIRS, SEQ, LANES), _BF16),
        scratch_shapes=[
            pltpu.VMEM((N_GROUPS, SEQ, LANES), _F32),
            pltpu.VMEM((N_GROUPS, SEQ, LANES), _F32),
            pltpu.VMEM((N_GROUPS, len(DIL_WINDOW_OFFSETS), 2 * BAND_BLOCK, 2 * BAND_BLOCK), _F32),
            pltpu.VMEM((N_GROUPS, len(DIL_WINDOW_OFFSETS), BAND_BLOCK, 2 * BAND_BLOCK), _F32),
            pltpu.VMEM((2, DIL_BLOCKS_PER_ITER, 2 * BAND_BLOCK, 2 * BAND_BLOCK), _BF16),
            pltpu.VMEM((2, DIL_BLOCKS_PER_ITER, BAND_BLOCK, LANES), _F32),
        ],
        compiler_params=_params("arbitrary", "arbitrary"),
        name="dil_attention",
    )(slopes, *([qkv] * 9))


def kernel(x, norm_mix_pre, norm_mix_post, norm_ffn_pre, norm_ffn_post, na_w_qkv, na_w_o, na_rpb, dil_w_qkv, dil_w_o,
           ffn_w_gate, ffn_w_up, ffn_w_down):
    x2d = x.reshape(N_TOK, D_MODEL)
    slopes = 2.0 ** (-8.0 * jnp.arange(1, N_HEADS + 1, dtype=_F32) / N_HEADS)

    def gain(g, layer):
        return g[layer].reshape(1, D_MODEL)

    for layer in range(DEPTH):
        j = layer // 2
        if layer % 2 == 0:
            qkv = _qkv_na(x2d, gain(norm_mix_pre, layer), na_w_qkv[j])
            o = _na_attention(qkv, (na_rpb[j].astype(_F32) * LOG2E).reshape(-1))
            w_o = na_w_o[j]
        else:
            qkv = _qkv_dil(x2d.reshape(BATCH, SEQ, D_MODEL), gain(norm_mix_pre, layer), dil_w_qkv[j])
            o = _dil_attention(qkv, slopes)
            w_o = dil_w_o[j]
        x2d = _mix_ffn(layer, o, x2d, w_o, gain(norm_mix_post, layer), gain(norm_ffn_pre, layer),
                       ffn_w_gate, ffn_w_up, ffn_w_down, gain(norm_ffn_post, layer))
    return x2d.reshape(BATCH, SEQ, D_MODEL)
```

```python
import functools

import jax
import jax.numpy as jnp
from jax import lax
from jax.experimental import pallas as pl
from jax.experimental.pallas import tpu as pltpu

D_MODEL = 1024
BATCH = 8
SEQ = 2048
DEPTH = 2
N_HEADS = 16
HEAD_DIM = 64
GRID_W = 64
GRID_ROWS = SEQ // GRID_W
NA_ROWS = 8
NA_COLS = 16
DIL_GROUPS = ((128, 1), (512, 4), (2048, 16))
N_GROUPS = len(DIL_GROUPS)
BAND_BLOCK = 128
D_FF = 2816
RMS_EPS = 1e-6
NEG_INF = -1e30
LOG2E = 1.4426950408889634
QK_SCALE = HEAD_DIM ** -0.5 * LOG2E

LANES = 128
N_PAIRS = D_MODEL // LANES
N_TOK = BATCH * SEQ
TM = 512
FFN_SLABS = 2
FF_SPLITS = ((0, 1536), (1536, 1280))
DIL_TN = 1024
CAST_COLS = 512
WEIGHT_STEPS = 8
SUBLANES = 8
MXU_DEPTH = 256
NA_ROWS_PER_ITER = 16
DIL_BLOCKS_PER_ITER = 8
DIL_WINDOW_OFFSETS = (0, BAND_BLOCK // 2, BAND_BLOCK)
VMEM_LIMIT = 56 * 1024 * 1024

_BF16 = jnp.bfloat16
_F32 = jnp.float32


def _rms(x, g):
    ms = jnp.mean(x * x, axis=-1, keepdims=True)
    return (x * lax.rsqrt(ms + RMS_EPS)) * g


def _resident(shape):
    return pl.BlockSpec(shape, lambda *_: (0,) * len(shape), pipeline_mode=pl.Buffered(1))


def _params(*sem):
    return pltpu.CompilerParams(dimension_semantics=sem, vmem_limit_bytes=VMEM_LIMIT)


def _qkv_na_kernel(x_ref, g_ref, w_ref, o_ref, wbf_scr):
    @pl.when(pl.program_id(0) == 0)
    def _():
        for c in range(3 * D_MODEL // CAST_COLS):
            cols = slice(c * CAST_COLS, (c + 1) * CAST_COLS)
            wbf_scr[:, cols] = w_ref[:, cols].astype(_BF16)

    h = _rms(x_ref[...], g_ref[...]).astype(_BF16)
    for j in range(3):
        y = jnp.dot(h, wbf_scr[:, j * D_MODEL:(j + 1) * D_MODEL], preferred_element_type=_F32)
        if j == 0:
            y = y * QK_SCALE
        for c in range(N_PAIRS):
            o_ref[0, j * N_PAIRS + c] = y[:, c * LANES:(c + 1) * LANES].astype(_BF16)


def _qkv_na(x2d, gain, w):
    tiles_per_seq = SEQ // TM
    return pl.pallas_call(
        _qkv_na_kernel,
        grid=(N_TOK // TM,),
        in_specs=[
            pl.BlockSpec((TM, D_MODEL), lambda i: (i, 0)),
            _resident((1, D_MODEL)),
            _resident((D_MODEL, 3 * D_MODEL)),
        ],
        out_specs=pl.BlockSpec((1, 3 * N_PAIRS, TM, LANES),
                               lambda i: (i // tiles_per_seq, 0, i % tiles_per_seq, 0)),
        out_shape=jax.ShapeDtypeStruct((BATCH, 3 * N_PAIRS, SEQ, LANES), _BF16),
        scratch_shapes=[pltpu.VMEM((D_MODEL, 3 * D_MODEL), _BF16)],
        compiler_params=_params("arbitrary"),
        name="qkv_na",
    )(x2d, gain, w)


def _stack_heads(q, first_head):
    zero = jnp.zeros_like(q)
    return jnp.concatenate([jnp.where(first_head, q, zero), jnp.where(first_head, zero, q)], axis=0)


def _masked_probs(s, bias, cap):
    s = jnp.minimum(s + bias, cap)
    m = jnp.max(s, axis=-1, keepdims=True)
    return jnp.exp2(s - m).astype(_BF16), m


def _stacked_pv(p, v, first_head):
    rows = p.shape[0] // 2
    r = jnp.dot(p, jnp.concatenate([v, jnp.ones_like(v)], axis=1), preferred_element_type=_F32)
    num = jnp.where(first_head, r[:rows, :LANES], r[rows:, :LANES])
    den = jnp.where(first_head, r[:rows, LANES:], r[rows:, LANES:])
    return num * (1.0 / den), den


def _na_kernel(q_ref, k_ref, v_ref, rpb_ref, o_ref, bias_scr, cap_scr, p_scr):
    kh = NA_ROWS
    nk = kh * GRID_W
    per_iter = NA_ROWS_PER_ITER
    n_iter = GRID_ROWS // per_iter
    lane = lax.broadcasted_iota(jnp.int32, (GRID_W, LANES), 1)
    first_head = lane < HEAD_DIM

    @pl.when(pl.program_id(1) == 0)
    def _():
        n_row_off, n_col_off = 2 * kh - 1, 2 * NA_COLS - 1
        lane_off = lax.broadcasted_iota(jnp.int32, (1, LANES), 1) - (GRID_W - NA_COLS)
        diag_idx = jnp.clip(lane_off, 0, n_col_off - 1)
        for hh in range(2):
            for row_off in range(n_row_off):
                base = ((2 * pl.program_id(0) + hh) * n_row_off + row_off) * n_col_off
                diag = jnp.zeros((1, LANES), _F32)
                for c in range(n_col_off):
                    diag = jnp.where(diag_idx == c, rpb_ref[base + c], diag)
                diag = jnp.broadcast_to(diag, (GRID_W, LANES))
                tiles = [pltpu.roll(diag, 1 + GRID_W * (1 - half), 1, stride=1, stride_axis=0) for half in range(2)]
                for d in range(kh):
                    r = row_off - (kh - 1 - d)
                    if 0 <= r < kh:
                        half = slice((r % 2) * GRID_W, (r % 2 + 1) * GRID_W)
                        bias_scr[d, hh * GRID_W:(hh + 1) * GRID_W, r * GRID_W:(r + 1) * GRID_W] = tiles[r % 2][:, half]
        qcol = lax.broadcasted_iota(jnp.int32, (2 * GRID_W, LANES), 0) % GRID_W
        kcol = lax.broadcasted_iota(jnp.int32, (2 * GRID_W, LANES), 1) % GRID_W
        col_start = jnp.clip(qcol - NA_COLS // 2, 0, GRID_W - NA_COLS)
        col_ok = (kcol >= col_start) & (kcol < col_start + NA_COLS)
        cap_scr[...] = jnp.where(col_ok, jnp.finfo(_F32).max, NEG_INF).astype(_F32)

    def window(i):
        rs = jnp.clip(i - kh // 2, 0, GRID_ROWS - kh)
        return rs, pl.ds(pl.multiple_of(rs * GRID_W, GRID_W), nk)

    def query_rows(i):
        return pl.ds(pl.multiple_of(i * GRID_W, GRID_W), GRID_W)

    def scores(slot, u, i):
        rs, win = window(i)
        s = lax.dot_general(_stack_heads(q_ref[0, 0, query_rows(i), :], first_head), k_ref[0, 0, win, :],
                            (((1,), (1,)), ((), ())), preferred_element_type=_F32)
        cap = jnp.concatenate([cap_scr[...]] * (nk // LANES), axis=1)
        p_scr[slot, u], _ = _masked_probs(s, bias_scr[i - rs], cap)

    def outputs(slot, u, i):
        _, win = window(i)
        o, _ = _stacked_pv(p_scr[slot, u], v_ref[0, 0, win, :], first_head)
        o_ref[0, 0, query_rows(i), :] = o.astype(_BF16)

    for u in range(per_iter):
        scores(0, u, u)

    def body(it, carry):
        slot = it % 2
        for u in range(per_iter):
            outputs(slot, u, it * per_iter + u)
            scores(1 - slot, u, (it + 1) * per_iter + u)
        return carry

    lax.fori_loop(0, n_iter - 1, body, 0)
    for u in range(per_iter):
        outputs((n_iter - 1) % 2, u, (n_iter - 1) * per_iter + u)


def _na_attention(qkv, rpb_flat):
    slab = (1, 1, SEQ, LANES)
    nk = NA_ROWS * GRID_W
    return pl.pallas_call(
        _na_kernel,
        grid=(N_PAIRS, BATCH),
        in_specs=[
            pl.BlockSpec(slab, lambda p, b: (b, p, 0, 0)),
            pl.BlockSpec(slab, lambda p, b: (b, N_PAIRS + p, 0, 0)),
            pl.BlockSpec(slab, lambda p, b: (b, 2 * N_PAIRS + p, 0, 0)),
            pl.BlockSpec(memory_space=pltpu.SMEM),
        ],
        out_specs=pl.BlockSpec(slab, lambda p, b: (b, p, 0, 0)),
        out_shape=jax.ShapeDtypeStruct((BATCH, N_PAIRS, SEQ, LANES), _BF16),
        scratch_shapes=[pltpu.VMEM((NA_ROWS, 2 * GRID_W, nk), _F32),
                        pltpu.VMEM((2 * GRID_W, LANES), _F32),
                        pltpu.VMEM((2, NA_ROWS_PER_ITER, 2 * GRID_W, nk), _BF16)],
        compiler_params=_params("arbitrary", "arbitrary"),
        name="na_attention",
    )(qkv, qkv, qkv, rpb_flat)


def _mix_ffn_kernel(o_ref, x_ref, wo_ref, gmix_ref, gpre_ref, wg_ref, wu_ref, wd_ref, gpost_ref, out_ref,
                    wo_scr, wg_scr, wu_scr, wd_scr):
    i = pl.program_id(0)

    @pl.when(i < WEIGHT_STEPS)
    def _():
        for src, dst in ((wo_ref, wo_scr), (wg_ref, wg_scr), (wu_ref, wu_scr), (wd_ref, wd_scr)):
            n = src.shape[0]
            dst[pl.ds(pl.multiple_of(i * n, n), n), :] = src[...].astype(_BF16)

    @pl.when(i >= WEIGHT_STEPS)
    def _():
        halves = [slice(t * (TM // FFN_SLABS), (t + 1) * (TM // FFN_SLABS)) for t in range(FFN_SLABS)]
        ys = [jnp.dot(jnp.concatenate([o_ref[0, c, rows, :] for c in range(N_PAIRS)], axis=-1), wo_scr[...],
                      preferred_element_type=_F32) for rows in halves]
        xs, hs = [], []
        for rows, y in zip(halves, ys):
            x = x_ref[rows, :] + _rms(y, gmix_ref[...])
            xs.append(x)
            hs.append(_rms(x, gpre_ref[...]).astype(_BF16))
        acc = [None] * FFN_SLABS
        for start, size in FF_SPLITS:
            gu = [(jnp.dot(h, wg_scr[:, start:start + size], preferred_element_type=_F32),
                   jnp.dot(h, wu_scr[:, start:start + size], preferred_element_type=_F32)) for h in hs]
            for t, (g, u) in enumerate(gu):
                a = ((g * jax.nn.sigmoid(g)) * u).astype(_BF16)
                part = jnp.dot(a, wd_scr[start:start + size, :], preferred_element_type=_F32)
                acc[t] = part if acc[t] is None else acc[t] + part
        for rows, x, y in zip(halves, xs, acc):
            out_ref[rows, :] = x + _rms(y, gpost_ref[...])


def _mix_ffn(layer, o_cbm, x2d, wo, gmix, gpre, wg, wu, wd, gpost):
    tiles_per_seq = SEQ // TM

    def tile(i):
        return jnp.maximum(i - WEIGHT_STEPS, 0)

    def chunk(i):
        return jnp.minimum(i, WEIGHT_STEPS - 1)

    def weight_chunk(rows, cols):
        return pl.BlockSpec((rows // WEIGHT_STEPS, cols), lambda i: (chunk(i), 0))

    def layer_weight_chunk(rows, cols):
        return pl.BlockSpec((None, rows // WEIGHT_STEPS, cols), lambda i: (layer, chunk(i), 0))

    return pl.pallas_call(
        _mix_ffn_kernel,
        grid=(WEIGHT_STEPS + N_TOK // TM,),
        in_specs=[
            pl.BlockSpec((1, N_PAIRS, TM, LANES), lambda i: (tile(i) // tiles_per_seq, 0, tile(i) % tiles_per_seq, 0)),
            pl.BlockSpec((TM, D_MODEL), lambda i: (tile(i), 0)),
            weight_chunk(D_MODEL, D_MODEL),
            _resident((1, D_MODEL)),
            _resident((1, D_MODEL)),
            layer_weight_chunk(D_MODEL, D_FF),
            layer_weight_chunk(D_MODEL, D_FF),
            layer_weight_chunk(D_FF, D_MODEL),
            _resident((1, D_MODEL)),
        ],
        out_specs=pl.BlockSpec((TM, D_MODEL), lambda i: (tile(i), 0)),
        out_shape=jax.ShapeDtypeStruct((N_TOK, D_MODEL), _F32),
        scratch_shapes=[pltpu.VMEM((D_MODEL, D_MODEL), _BF16), pltpu.VMEM((D_MODEL, D_FF), _BF16),
                        pltpu.VMEM((D_MODEL, D_FF), _BF16), pltpu.VMEM((D_FF, D_MODEL), _BF16)],
        compiler_params=_params("arbitrary"),
        name="mix_ffn",
    )(o_cbm, x2d, wo, gmix, gpre, wg, wu, wd, gpost)


def _qkv_dil_kernel(x_ref, g_ref, w_ref, o_ref, h32_scr, hperm_scr):
    j = pl.program_id(1)
    tiles_per_group = 3 * D_MODEL // DIL_TN

    @pl.when(j == 0)
    def _():
        for t in range(SEQ // TM):
            rows = slice(t * TM, (t + 1) * TM)
            h = _rms(x_ref[0, rows, :], g_ref[...])
            for c in range(N_PAIRS):
                h32_scr[c, rows, :] = h[:, c * LANES:(c + 1) * LANES]

    for grp, (_, dil) in enumerate(DIL_GROUPS):
        @pl.when(j == grp * tiles_per_group)
        def _(dil=dil):
            sub = SEQ // dil
            if dil <= SUBLANES:
                for r in range(dil):
                    for c in range(N_PAIRS):
                        val = h32_scr[c] if dil == 1 else h32_scr[c, pl.ds(r, sub, stride=dil), :]
                        hperm_scr[r * sub:(r + 1) * sub, c * LANES:(c + 1) * LANES] = val.astype(_BF16)
            else:
                per = MXU_DEPTH // dil
                out_row = lax.broadcasted_iota(jnp.int32, (MXU_DEPTH, MXU_DEPTH), 0)
                in_row = lax.broadcasted_iota(jnp.int32, (MXU_DEPTH, MXU_DEPTH), 1)
                pick = (in_row == dil * (out_row % per) + out_row // per).astype(_BF16)
                for a in range(SEQ // MXU_DEPTH):
                    rows = slice(a * MXU_DEPTH, (a + 1) * MXU_DEPTH)
                    chunk = jnp.concatenate([h32_scr[c, rows, :].astype(_BF16) for c in range(N_PAIRS)], axis=1)
                    picked = jnp.dot(pick, chunk, preferred_element_type=_F32).astype(_BF16)
                    for r in range(dil):
                        hperm_scr[r * sub + a * per:r * sub + (a + 1) * per, :] = picked[r * per:(r + 1) * per]

    y = jnp.dot(hperm_scr[...], w_ref[...].astype(_BF16), preferred_element_type=_F32)
    is_q_tile = j % tiles_per_group < D_MODEL // DIL_TN
    y = y * jnp.where(is_q_tile, QK_SCALE, 1.0)
    for c in range(DIL_TN // LANES):
        o_ref[0, c] = y[:, c * LANES:(c + 1) * LANES].astype(_BF16)


def _qkv_dil(x3d, gain, w):
    n_col = N_GROUPS * 3 * D_MODEL
    blocks_per_tile = DIL_TN // LANES
    return pl.pallas_call(
        _qkv_dil_kernel,
        grid=(BATCH, n_col // DIL_TN),
        in_specs=[
            pl.BlockSpec((1, SEQ, D_MODEL), lambda b, j: (b, 0, 0)),
            _resident((1, D_MODEL)),
            pl.BlockSpec((D_MODEL, DIL_TN), lambda b, j: (0, j)),
        ],
        out_specs=pl.BlockSpec((1, blocks_per_tile, SEQ, LANES), lambda b, j: (b, j, 0, 0)),
        out_shape=jax.ShapeDtypeStruct((BATCH, n_col // LANES, SEQ, LANES), _BF16),
        scratch_shapes=[pltpu.VMEM((N_PAIRS, SEQ, LANES), _F32), pltpu.VMEM((SEQ, D_MODEL), _BF16)],
        compiler_params=_params("parallel", "arbitrary"),
        name="qkv_dil",
    )(x3d, gain, w)


def _dil_kernel(slopes_ref, q0, k0, v0, q1, k1, v1, q2, k2, v2, o_ref, og_scr, lg_scr, bias_scr, cap_scr, p_scr,
                m_scr):
    pair = pl.program_id(0)
    qkv_refs = ((q0, k0, v0), (q1, k1, v1), (q2, k2, v2))
    lane = lax.broadcasted_iota(jnp.int32, (BAND_BLOCK, LANES), 1)
    first_head = lane < HEAD_DIM
    max_width = 2 * BAND_BLOCK
    per_iter = DIL_BLOCKS_PER_ITER
    n_iter = SEQ // (BAND_BLOCK * per_iter)

    @pl.when(pl.program_id(1) == 0)
    def _():
        rel = (lax.broadcasted_iota(jnp.int32, (BAND_BLOCK, max_width), 0)
               - lax.broadcasted_iota(jnp.int32, (BAND_BLOCK, max_width), 1))
        for grp, (window, dil) in enumerate(DIL_GROUPS):
            radius = window // (2 * dil)
            for var, off in enumerate(DIL_WINDOW_OFFSETS):
                dist = jnp.abs(rel + off)
                distf = dist.astype(_F32)
                cap_scr[grp, var] = jnp.where(dist <= radius, jnp.finfo(_F32).max, NEG_INF).astype(_F32)
                for hh in range(2):
                    head_rows = slice(hh * BAND_BLOCK, (hh + 1) * BAND_BLOCK)
                    bias_scr[grp, var, head_rows, :] = -((slopes_ref[2 * pair + hh] * dil) * distf) * LOG2E

    def stage_fns(grp):
        window, dil = DIL_GROUPS[grp]
        q_ref, k_ref, v_ref = qkv_refs[grp]
        radius = window // (2 * dil)
        sub = SEQ // dil
        blocks_per_sub = sub // BAND_BLOCK
        width = min(max_width, sub)

        def place(t):
            res = t // blocks_per_sub
            c = t % blocks_per_sub
            kstart = jnp.clip(c * BAND_BLOCK - radius, 0, sub - width)
            return res, c, pl.ds(pl.multiple_of(res * sub + kstart, radius), width)

        def scores(slot, u, t):
            _, c, key_rows = place(t)
            q = q_ref[0, 0, pl.ds(pl.multiple_of(t * BAND_BLOCK, BAND_BLOCK), BAND_BLOCK), :]
            s = lax.dot_general(_stack_heads(q, first_head), k_ref[0, 0, key_rows, :], (((1,), (1,)), ((), ())),
                                preferred_element_type=_F32)
            var = 0 if blocks_per_sub == 1 else jnp.int32(c > 0) + jnp.int32(c == blocks_per_sub - 1)
            cap = cap_scr[grp, var, :, :width]
            p, m = _masked_probs(s, bias_scr[grp, var, :, :width], jnp.concatenate([cap, cap], axis=0))
            p_scr[slot, u, :, :width] = p
            m_scr[slot, u] = jnp.where(first_head, m[:BAND_BLOCK], m[BAND_BLOCK:])

        def outputs(slot, u, t):
            res, c, key_rows = place(t)
            o, den = _stacked_pv(p_scr[slot, u, :, :width], v_ref[0, 0, key_rows, :], first_head)
            lse = m_scr[slot, u] + jnp.log2(den)
            if dil > 1:
                rows = pl.ds(c * (BAND_BLOCK * dil) + res, BAND_BLOCK, stride=dil)
                og_scr[grp, rows, :] = o
                lg_scr[grp, rows, :] = lse
            else:
                rows = pl.ds(pl.multiple_of(t * BAND_BLOCK, BAND_BLOCK), BAND_BLOCK)
                others = [g for g in range(N_GROUPS) if g != grp]
                lses = [lse] + [lg_scr[g, rows, :] for g in others]
                outs = [o] + [og_scr[g, rows, :] for g in others]
                mx = functools.reduce(jnp.maximum, lses)
                e = [jnp.exp2(x - mx) for x in lses]
                num = functools.reduce(lambda a, b: a + b, [w * x for w, x in zip(e, outs)])
                o_ref[0, 0, rows, :] = (num / functools.reduce(lambda a, b: a + b, e)).astype(_BF16)

        return scores, outputs

    order = sorted(range(N_GROUPS), key=lambda g: -DIL_GROUPS[g][1])
    assert DIL_GROUPS[order[-1]][1] == 1
    fns = [stage_fns(grp) for grp in order]
    for u in range(per_iter):
        fns[0][0](0, u, u)
    for pos in range(N_GROUPS):
        scores, outputs = fns[pos]
        first_slot = (pos * n_iter) % 2

        def body(it, carry, scores=scores, outputs=outputs, first_slot=first_slot):
            slot = (first_slot + it) % 2
            for u in range(per_iter):
                outputs(slot, u, it * per_iter + u)
                scores(1 - slot, u, (it + 1) * per_iter + u)
            return carry

        if n_iter > 1:
            lax.fori_loop(0, n_iter - 1, body, 0)
        last_slot = (first_slot + n_iter - 1) % 2
        for u in range(per_iter):
            outputs(last_slot, u, (n_iter - 1) * per_iter + u)
            if pos + 1 < N_GROUPS:
                fns[pos + 1][0](1 - last_slot, u, u)


def _dil_attention(qkv, slopes):
    slab = (1, 1, SEQ, LANES)

    def spec(grp, which):
        base = (grp * 3 + which) * N_PAIRS
        return pl.BlockSpec(slab, lambda p, b: (b, base + p, 0, 0))

    return pl.pallas_call(
        _dil_kernel,
        grid=(N_PAIRS, BATCH),
        in_specs=[pl.BlockSpec(memory_space=pltpu.SMEM)]
        + [spec(grp, which) for grp in range(N_GROUPS) for which in range(3)],
        out_specs=pl.BlockSpec(slab, lambda p, b: (b, p, 0, 0)),
        out_shape=jax.ShapeDtypeStruct((BATCH, N_PAIRS, SEQ, LANES), _BF16),
        scratch_shapes=[
            pltpu.VMEM((N_GROUPS, SEQ, LANES), _F32),
            pltpu.VMEM((N_GROUPS, SEQ, LANES), _F32),
            pltpu.VMEM((N_GROUPS, len(DIL_WINDOW_OFFSETS), 2 * BAND_BLOCK, 2 * BAND_BLOCK), _F32),
            pltpu.VMEM((N_GROUPS, len(DIL_WINDOW_OFFSETS), BAND_BLOCK, 2 * BAND_BLOCK), _F32),
            pltpu.VMEM((2, DIL_BLOCKS_PER_ITER, 2 * BAND_BLOCK, 2 * BAND_BLOCK), _BF16),
            pltpu.VMEM((2, DIL_BLOCKS_PER_ITER, BAND_BLOCK, LANES), _F32),
        ],
        compiler_params=_params("arbitrary", "arbitrary"),
        name="dil_attention",
    )(slopes, *([qkv] * 9))


def kernel(x, norm_mix_pre, norm_mix_post, norm_ffn_pre, norm_ffn_post, na_w_qkv, na_w_o, na_rpb, dil_w_qkv, dil_w_o,
           ffn_w_gate, ffn_w_up, ffn_w_down):
    x2d = x.reshape(N_TOK, D_MODEL)
    slopes = 2.0 ** (-8.0 * jnp.arange(1, N_HEADS + 1, dtype=_F32) / N_HEADS)

    def gain(g, layer):
        return g[layer].reshape(1, D_MODEL)

    for layer in range(DEPTH):
        j = layer // 2
        if layer % 2 == 0:
            qkv = _qkv_na(x2d, gain(norm_mix_pre, layer), na_w_qkv[j])
            o = _na_attention(qkv, (na_rpb[j].astype(_F32) * LOG2E).reshape(-1))
            w_o = na_w_o[j]
        else:
            qkv = _qkv_dil(x2d.reshape(BATCH, SEQ, D_MODEL), gain(norm_mix_pre, layer), dil_w_qkv[j])
            o = _dil_attention(qkv, slopes)
            w_o = dil_w_o[j]
        x2d = _mix_ffn(layer, o, x2d, w_o, gain(norm_mix_post, layer), gain(norm_ffn_pre, layer),
                       ffn_w_gate, ffn_w_up, ffn_w_down, gain(norm_ffn_post, layer))
    return x2d.reshape(BATCH, SEQ, D_MODEL)
```

```python
import functools

import jax
import jax.numpy as jnp
from jax import lax
from jax.experimental import pallas as pl
from jax.experimental.pallas import tpu as pltpu

D_MODEL = 1024
BATCH = 8
SEQ = 2048
DEPTH = 2
N_HEADS = 16
HEAD_DIM = 64
GRID_W = 64
GRID_ROWS = SEQ // GRID_W
NA_ROWS = 8
NA_COLS = 16
DIL_GROUPS = ((128, 1), (512, 4), (2048, 16))
N_GROUPS = len(DIL_GROUPS)
BAND_BLOCK = 128
D_FF = 2816
RMS_EPS = 1e-6
NEG_INF = -1e30
LOG2E = 1.4426950408889634
QK_SCALE = HEAD_DIM ** -0.5 * LOG2E

LANES = 128
N_PAIRS = D_MODEL // LANES
N_TOK = BATCH * SEQ
TM = 512
FFN_SLABS = 2
FF_SPLITS = ((0, 1536), (1536, 1280))
DIL_TN = 1024
CAST_COLS = 512
WEIGHT_STEPS = 8
SUBLANES = 8
MXU_DEPTH = 256
NA_ROWS_PER_ITER = 16
DIL_BLOCKS_PER_ITER = 8
DIL_WINDOW_OFFSETS = (0, BAND_BLOCK // 2, BAND_BLOCK)
VMEM_LIMIT = 56 * 1024 * 1024

_BF16 = jnp.bfloat16
_F32 = jnp.float32


def _rms(x, g):
    ms = jnp.mean(x * x, axis=-1, keepdims=True)
    return (x * lax.rsqrt(ms + RMS_EPS)) * g


def _resident(shape):
    return pl.BlockSpec(shape, lambda *_: (0,) * len(shape), pipeline_mode=pl.Buffered(1))


def _params(*sem):
    return pltpu.CompilerParams(dimension_semantics=sem, vmem_limit_bytes=VMEM_LIMIT)


def _qkv_na_kernel(x_ref, g_ref, w_ref, o_ref, wbf_scr):
    @pl.when(pl.program_id(0) == 0)
    def _():
        for c in range(3 * D_MODEL // CAST_COLS):
            cols = slice(c * CAST_COLS, (c + 1) * CAST_COLS)
            wbf_scr[:, cols] = w_ref[:, cols].astype(_BF16)

    h = _rms(x_ref[...], g_ref[...]).astype(_BF16)
    for j in range(3):
        y = jnp.dot(h, wbf_scr[:, j * D_MODEL:(j + 1) * D_MODEL], preferred_element_type=_F32)
        if j == 0:
            y = y * QK_SCALE
        for c in range(N_PAIRS):
            o_ref[0, j * N_PAIRS + c] = y[:, c * LANES:(c + 1) * LANES].astype(_BF16)


def _qkv_na(x2d, gain, w):
    tiles_per_seq = SEQ // TM
    return pl.pallas_call(
        _qkv_na_kernel,
        grid=(N_TOK // TM,),
        in_specs=[
            pl.BlockSpec((TM, D_MODEL), lambda i: (i, 0)),
            _resident((1, D_MODEL)),
            _resident((D_MODEL, 3 * D_MODEL)),
        ],
        out_specs=pl.BlockSpec((1, 3 * N_PAIRS, TM, LANES),
                               lambda i: (i // tiles_per_seq, 0, i % tiles_per_seq, 0)),
        out_shape=jax.ShapeDtypeStruct((BATCH, 3 * N_PAIRS, SEQ, LANES), _BF16),
        scratch_shapes=[pltpu.VMEM((D_MODEL, 3 * D_MODEL), _BF16)],
        compiler_params=_params("arbitrary"),
        name="qkv_na",
    )(x2d, gain, w)


def _stack_heads(q, first_head):
    zero = jnp.zeros_like(q)
    return jnp.concatenate([jnp.where(first_head, q, zero), jnp.where(first_head, zero, q)], axis=0)


def _masked_probs(s, bias, cap):
    s = jnp.minimum(s + bias, cap)
    m = jnp.max(s, axis=-1, keepdims=True)
    return jnp.exp2(s - m).astype(_BF16), m


def _stacked_pv(p, v, first_head):
    rows = p.shape[0] // 2
    r = jnp.dot(p, jnp.concatenate([v, jnp.ones_like(v)], axis=1), preferred_element_type=_F32)
    num = jnp.where(first_head, r[:rows, :LANES], r[rows:, :LANES])
    den = jnp.where(first_head, r[:rows, LANES:], r[rows:, LANES:])
    return num * (1.0 / den), den


def _na_kernel(q_ref, k_ref, v_ref, rpb_ref, o_ref, bias_scr, cap_scr, p_scr):
    kh = NA_ROWS
    nk = kh * GRID_W
    per_iter = NA_ROWS_PER_ITER
    n_iter = GRID_ROWS // per_iter
    lane = lax.broadcasted_iota(jnp.int32, (GRID_W, LANES), 1)
    first_head = lane < HEAD_DIM

    @pl.when(pl.program_id(1) == 0)
    def _():
        n_row_off, n_col_off = 2 * kh - 1, 2 * NA_COLS - 1
        lane_off = lax.broadcasted_iota(jnp.int32, (1, LANES), 1) - (GRID_W - NA_COLS)
        diag_idx = jnp.clip(lane_off, 0, n_col_off - 1)
        for hh in range(2):
            for row_off in range(n_row_off):
                base = ((2 * pl.program_id(0) + hh) * n_row_off + row_off) * n_col_off
                diag = jnp.zeros((1, LANES), _F32)
                for c in range(n_col_off):
                    diag = jnp.where(diag_idx == c, rpb_ref[base + c], diag)
                diag = jnp.broadcast_to(diag, (GRID_W, LANES))
                tiles = [pltpu.roll(diag, 1 + GRID_W * (1 - half), 1, stride=1, stride_axis=0) for half in range(2)]
                for d in range(kh):
                    r = row_off - (kh - 1 - d)
                    if 0 <= r < kh:
                        half = slice((r % 2) * GRID_W, (r % 2 + 1) * GRID_W)
                        bias_scr[d, hh * GRID_W:(hh + 1) * GRID_W, r * GRID_W:(r + 1) * GRID_W] = tiles[r % 2][:, half]
        qcol = lax.broadcasted_iota(jnp.int32, (2 * GRID_W, LANES), 0) % GRID_W
        kcol = lax.broadcasted_iota(jnp.int32, (2 * GRID_W, LANES), 1) % GRID_W
        col_start = jnp.clip(qcol - NA_COLS // 2, 0, GRID_W - NA_COLS)
        col_ok = (kcol >= col_start) & (kcol < col_start + NA_COLS)
        cap_scr[...] = jnp.where(col_ok, jnp.finfo(_F32).max, NEG_INF).astype(_F32)

    def window(i):
        rs = jnp.clip(i - kh // 2, 0, GRID_ROWS - kh)
        return rs, pl.ds(pl.multiple_of(rs * GRID_W, GRID_W), nk)

    def query_rows(i):
        return pl.ds(pl.multiple_of(i * GRID_W, GRID_W), GRID_W)

    def scores(slot, u, i):
        rs, win = window(i)
        s = lax.dot_general(_stack_heads(q_ref[0, 0, query_rows(i), :], first_head), k_ref[0, 0, win, :],
                            (((1,), (1,)), ((), ())), preferred_element_type=_F32)
        cap = jnp.concatenate([cap_scr[...]] * (nk // LANES), axis=1)
        p_scr[slot, u], _ = _masked_probs(s, bias_scr[i - rs], cap)

    def outputs(slot, u, i):
        _, win = window(i)
        o, _ = _stacked_pv(p_scr[slot, u], v_ref[0, 0, win, :], first_head)
        o_ref[0, 0, query_rows(i), :] = o.astype(_BF16)

    for u in range(per_iter):
        scores(0, u, u)

    def body(it, carry):
        slot = it % 2
        for u in range(per_iter):
            outputs(slot, u, it * per_iter + u)
            scores(1 - slot, u, (it + 1) * per_iter + u)
        return carry

    lax.fori_loop(0, n_iter - 1, body, 0)
    for u in range(per_iter):
        outputs((n_iter - 1) % 2, u, (n_iter - 1) * per_iter + u)


def _na_attention(qkv, rpb_flat):
    slab = (1, 1, SEQ, LANES)
    nk = NA_ROWS * GRID_W
    return pl.pallas_call(
        _na_kernel,
        grid=(N_PAIRS, BATCH),
        in_specs=[
            pl.BlockSpec(slab, lambda p, b: (b, p, 0, 0)),
            pl.BlockSpec(slab, lambda p, b: (b, N_PAIRS + p, 0, 0)),
            pl.BlockSpec(slab, lambda p, b: (b, 2 * N_PAIRS + p, 0, 0)),
            pl.BlockSpec(memory_space=pltpu.SMEM),
        ],
        out_specs=pl.BlockSpec(slab, lambda p, b: (b, p, 0, 0)),
        out_shape=jax.ShapeDtypeStruct((BATCH, N_PAIRS, SEQ, LANES), _BF16),
        scratch_shapes=[pltpu.VMEM((NA_ROWS, 2 * GRID_W, nk), _F32),
                        pltpu.VMEM((2 * GRID_W, LANES), _F32),
                        pltpu.VMEM((2, NA_ROWS_PER_ITER, 2 * GRID_W, nk), _BF16)],
        compiler_params=_params("arbitrary", "arbitrary"),
        name="na_attention",
    )(qkv, qkv, qkv, rpb_flat)


def _mix_ffn_kernel(o_ref, x_ref, wo_ref, gmix_ref, gpre_ref, wg_ref, wu_ref, wd_ref, gpost_ref, out_ref,
                    wo_scr, wg_scr, wu_scr, wd_scr):
    i = pl.program_id(0)

    @pl.when(i < WEIGHT_STEPS)
    def _():
        for src, dst in ((wo_ref, wo_scr), (wg_ref, wg_scr), (wu_ref, wu_scr), (wd_ref, wd_scr)):
            n = src.shape[0]
            dst[pl.ds(pl.multiple_of(i * n, n), n), :] = src[...].astype(_BF16)

    @pl.when(i >= WEIGHT_STEPS)
    def _():
        halves = [slice(t * (TM // FFN_SLABS), (t + 1) * (TM // FFN_SLABS)) for t in range(FFN_SLABS)]
        ys = [jnp.dot(jnp.concatenate([o_ref[0, c, rows, :] for c in range(N_PAIRS)], axis=-1), wo_scr[...],
                      preferred_element_type=_F32) for rows in halves]
        xs, hs = [], []
        for rows, y in zip(halves, ys):
            x = x_ref[rows, :] + _rms(y, gmix_ref[...])
            xs.append(x)
            hs.append(_rms(x, gpre_ref[...]).astype(_BF16))
        acc = [None] * FFN_SLABS
        for start, size in FF_SPLITS:
            gu = [(jnp.dot(h, wg_scr[:, start:start + size], preferred_element_type=_F32),
                   jnp.dot(h, wu_scr[:, start:start + size], preferred_element_type=_F32)) for h in hs]
            for t, (g, u) in enumerate(gu):
                a = ((g * jax.nn.sigmoid(g)) * u).astype(_BF16)
                part = jnp.dot(a, wd_scr[start:start + size, :], preferred_element_type=_F32)
                acc[t] = part if acc[t] is None else acc[t] + part
        for rows, x, y in zip(halves, xs, acc):
            out_ref[rows, :] = x + _rms(y, gpost_ref[...])


def _mix_ffn(layer, o_cbm, x2d, wo, gmix, gpre, wg, wu, wd, gpost):
    tiles_per_seq = SEQ // TM

    def tile(i):
        return jnp.maximum(i - WEIGHT_STEPS, 0)

    def chunk(i):
        return jnp.minimum(i, WEIGHT_STEPS - 1)

    def weight_chunk(rows, cols):
        return pl.BlockSpec((rows // WEIGHT_STEPS, cols), lambda i: (chunk(i), 0))

    def layer_weight_chunk(rows, cols):
        return pl.BlockSpec((None, rows // WEIGHT_STEPS, cols), lambda i: (layer, chunk(i), 0))

    return pl.pallas_call(
        _mix_ffn_kernel,
        grid=(WEIGHT_STEPS + N_TOK // TM,),
        in_specs=[
            pl.BlockSpec((1, N_PAIRS, TM, LANES), lambda i: (tile(i) // tiles_per_seq, 0, tile(i) % tiles_per_seq, 0)),
            pl.BlockSpec((TM, D_MODEL), lambda i: (tile(i), 0)),
            weight_chunk(D_MODEL, D_MODEL),
            _resident((1, D_MODEL)),
            _resident((1, D_MODEL)),
            layer_weight_chunk(D_MODEL, D_FF),
            layer_weight_chunk(D_MODEL, D_FF),
            layer_weight_chunk(D_FF, D_MODEL),
            _resident((1, D_MODEL)),
        ],
        out_specs=pl.BlockSpec((TM, D_MODEL), lambda i: (tile(i), 0)),
        out_shape=jax.ShapeDtypeStruct((N_TOK, D_MODEL), _F32),
        scratch_shapes=[pltpu.VMEM((D_MODEL, D_MODEL), _BF16), pltpu.VMEM((D_MODEL, D_FF), _BF16),
                        pltpu.VMEM((D_MODEL, D_FF), _BF16), pltpu.VMEM((D_FF, D_MODEL), _BF16)],
        compiler_params=_params("arbitrary"),
        name="mix_ffn",
    )(o_cbm, x2d, wo, gmix, gpre, wg, wu, wd, gpost)


def _qkv_dil_kernel(x_ref, g_ref, w_ref, o_ref, h32_scr, hperm_scr):
    assert DIL_GROUPS[0][1] == 1
    j = pl.program_id(1)
    tiles_per_group = 3 * D_MODEL // DIL_TN

    def project(rows, w):
        y = jnp.dot(hperm_scr[rows, :], w, preferred_element_type=_F32)
        is_q_tile = j % tiles_per_group < D_MODEL // DIL_TN
        y = y * jnp.where(is_q_tile, QK_SCALE, 1.0)
        for c in range(DIL_TN // LANES):
            o_ref[0, c, rows, :] = y[:, c * LANES:(c + 1) * LANES].astype(_BF16)

    @pl.when(j == 0)
    def _():
        w = w_ref[...].astype(_BF16)
        for t in range(SEQ // TM):
            rows = slice(t * TM, (t + 1) * TM)
            h = _rms(x_ref[0, rows, :], g_ref[...])
            for c in range(N_PAIRS):
                h32_scr[c, rows, :] = h[:, c * LANES:(c + 1) * LANES]
            hperm_scr[rows, :] = h.astype(_BF16)
            project(rows, w)

    for grp, (_, dil) in enumerate(DIL_GROUPS):
        if dil == 1:
            continue

        @pl.when(j == grp * tiles_per_group)
        def _(dil=dil):
            w = w_ref[...].astype(_BF16)
            sub = SEQ // dil
            if dil <= SUBLANES:
                for r in range(dil):
                    rows = slice(r * sub, (r + 1) * sub)
                    for c in range(N_PAIRS):
                        val = h32_scr[c, pl.ds(r, sub, stride=dil), :]
                        hperm_scr[rows, c * LANES:(c + 1) * LANES] = val.astype(_BF16)
                    project(rows, w)
            else:
                per = MXU_DEPTH // dil
                out_row = lax.broadcasted_iota(jnp.int32, (MXU_DEPTH, MXU_DEPTH), 0)
                in_row = lax.broadcasted_iota(jnp.int32, (MXU_DEPTH, MXU_DEPTH), 1)
                pick = (in_row == dil * (out_row % per) + out_row // per).astype(_BF16)
                for a in range(SEQ // MXU_DEPTH):
                    rows = slice(a * MXU_DEPTH, (a + 1) * MXU_DEPTH)
                    chunk = jnp.concatenate([h32_scr[c, rows, :].astype(_BF16) for c in range(N_PAIRS)], axis=1)
                    picked = jnp.dot(pick, chunk, preferred_element_type=_F32).astype(_BF16)
                    for r in range(dil):
                        hperm_scr[r * sub + a * per:r * sub + (a + 1) * per, :] = picked[r * per:(r + 1) * per]
                project(slice(0, SEQ), w)

    @pl.when(j % tiles_per_group != 0)
    def _():
        project(slice(0, SEQ), w_ref[...].astype(_BF16))


def _qkv_dil(x3d, gain, w):
    n_col = N_GROUPS * 3 * D_MODEL
    blocks_per_tile = DIL_TN // LANES
    return pl.pallas_call(
        _qkv_dil_kernel,
        grid=(BATCH, n_col // DIL_TN),
        in_specs=[
            pl.BlockSpec((1, SEQ, D_MODEL), lambda b, j: (b, 0, 0)),
            _resident((1, D_MODEL)),
            pl.BlockSpec((D_MODEL, DIL_TN), lambda b, j: (0, j)),
        ],
        out_specs=pl.BlockSpec((1, blocks_per_tile, SEQ, LANES), lambda b, j: (b, j, 0, 0)),
        out_shape=jax.ShapeDtypeStruct((BATCH, n_col // LANES, SEQ, LANES), _BF16),
        scratch_shapes=[pltpu.VMEM((N_PAIRS, SEQ, LANES), _F32), pltpu.VMEM((SEQ, D_MODEL), _BF16)],
        compiler_params=_params("parallel", "arbitrary"),
        name="qkv_dil",
    )(x3d, gain, w)


def _dil_kernel(slopes_ref, q0, k0, v0, q1, k1, v1, q2, k2, v2, o_ref, og_scr, lg_scr, bias_scr, cap_scr, p_scr,
                m_scr):
    pair = pl.program_id(0)
    qkv_refs = ((q0, k0, v0), (q1, k1, v1), (q2, k2, v2))
    lane = lax.broadcasted_iota(jnp.int32, (BAND_BLOCK, LANES), 1)
    first_head = lane < HEAD_DIM
    max_width = 2 * BAND_BLOCK
    per_iter = DIL_BLOCKS_PER_ITER
    n_iter = SEQ // (BAND_BLOCK * per_iter)

    @pl.when(pl.program_id(1) == 0)
    def _():
        rel = (lax.broadcasted_iota(jnp.int32, (BAND_BLOCK, max_width), 0)
               - lax.broadcasted_iota(jnp.int32, (BAND_BLOCK, max_width), 1))
        for grp, (window, dil) in enumerate(DIL_GROUPS):
            radius = window // (2 * dil)
            for var, off in enumerate(DIL_WINDOW_OFFSETS):
                dist = jnp.abs(rel + off)
                distf = dist.astype(_F32)
                cap_scr[grp, var] = jnp.where(dist <= radius, jnp.finfo(_F32).max, NEG_INF).astype(_F32)
                for hh in range(2):
                    head_rows = slice(hh * BAND_BLOCK, (hh + 1) * BAND_BLOCK)
                    bias_scr[grp, var, head_rows, :] = -((slopes_ref[2 * pair + hh] * dil) * distf) * LOG2E

    def stage_fns(grp):
        window, dil = DIL_GROUPS[grp]
        q_ref, k_ref, v_ref = qkv_refs[grp]
        radius = window // (2 * dil)
        sub = SEQ // dil
        blocks_per_sub = sub // BAND_BLOCK
        width = min(max_width, sub)

        def place(t):
            res = t // blocks_per_sub
            c = t % blocks_per_sub
            kstart = jnp.clip(c * BAND_BLOCK - radius, 0, sub - width)
            return res, c, pl.ds(pl.multiple_of(res * sub + kstart, radius), width)

        def scores(slot, u, t):
            _, c, key_rows = place(t)
            q = q_ref[0, 0, pl.ds(pl.multiple_of(t * BAND_BLOCK, BAND_BLOCK), BAND_BLOCK), :]
            s = lax.dot_general(_stack_heads(q, first_head), k_ref[0, 0, key_rows, :], (((1,), (1,)), ((), ())),
                                preferred_element_type=_F32)
            var = 0 if blocks_per_sub == 1 else jnp.int32(c > 0) + jnp.int32(c == blocks_per_sub - 1)
            cap = cap_scr[grp, var, :, :width]
            p, m = _masked_probs(s, bias_scr[grp, var, :, :width], jnp.concatenate([cap, cap], axis=0))
            p_scr[slot, u, :, :width] = p
            m_scr[slot, u] = jnp.where(first_head, m[:BAND_BLOCK], m[BAND_BLOCK:])

        def outputs(slot, u, t):
            res, c, key_rows = place(t)
            o, den = _stacked_pv(p_scr[slot, u, :, :width], v_ref[0, 0, key_rows, :], first_head)
            lse = m_scr[slot, u] + jnp.log2(den)
            if dil > 1:
                rows = pl.ds(c * (BAND_BLOCK * dil) + res, BAND_BLOCK, stride=dil)
                og_scr[grp, rows, :] = o
                lg_scr[grp, rows, :] = lse
            else:
                rows = pl.ds(pl.multiple_of(t * BAND_BLOCK, BAND_BLOCK), BAND_BLOCK)
                others = [g for g in range(N_GROUPS) if g != grp]
                lses = [lse] + [lg_scr[g, rows, :] for g in others]
                outs = [o] + [og_scr[g, rows, :] for g in others]
                mx = functools.reduce(jnp.maximum, lses)
                e = [jnp.exp2(x - mx) for x in lses]
                num = functools.reduce(lambda a, b: a + b, [w * x for w, x in zip(e, outs)])
                o_ref[0, 0, rows, :] = (num / functools.reduce(lambda a, b: a + b, e)).astype(_BF16)

        return scores, outputs

    order = sorted(range(N_GROUPS), key=lambda g: -DIL_GROUPS[g][1])
    assert DIL_GROUPS[order[-1]][1] == 1
    fns = [stage_fns(grp) for grp in order]
    for u in range(per_iter):
        fns[0][0](0, u, u)
    for pos in range(N_GROUPS):
        scores, outputs = fns[pos]
        first_slot = (pos * n_iter) % 2

        def body(it, carry, scores=scores, outputs=outputs, first_slot=first_slot):
            slot = (first_slot + it) % 2
            for u in range(per_iter):
                outputs(slot, u, it * per_iter + u)
                scores(1 - slot, u, (it + 1) * per_iter + u)
            return carry

        if n_iter > 1:
            lax.fori_loop(0, n_iter - 1, body, 0)
        last_slot = (first_slot + n_iter - 1) % 2
        for u in range(per_iter):
            outputs(last_slot, u, (n_iter - 1) * per_iter + u)
            if pos + 1 < N_GROUPS:
                fns[pos + 1][0](1 - last_slot, u, u)


def _dil_attention(qkv, slopes):
    slab = (1, 1, SEQ, LANES)

    def spec(grp, which):
        base = (grp * 3 + which) * N_PAIRS
        return pl.BlockSpec(slab, lambda p, b: (b, base + p, 0, 0))

    return pl.pallas_call(
        _dil_kernel,
        grid=(N_PAIRS, BATCH),
        in_specs=[pl.BlockSpec(memory_space=pltpu.SMEM)]
        + [spec(grp, which) for grp in range(N_GROUPS) for which in range(3)],
        out_specs=pl.BlockSpec(slab, lambda p, b: (b, p, 0, 0)),
        out_shape=jax.ShapeDtypeStruct((BATCH, N_PAIRS, SEQ, LANES), _BF16),
        scratch_shapes=[
            pltpu.VMEM((N_GROUPS, SEQ, LANES), _F32),
            pltpu.VMEM((N_GROUPS, SEQ, LANES), _F32),
            pltpu.VMEM((N_GROUPS, len(DIL_WINDOW_OFFSETS), 2 * BAND_BLOCK, 2 * BAND_BLOCK), _F32),
            pltpu.VMEM((N_GROUPS, len(DIL_WINDOW_OFFSETS), BAND_BLOCK, 2 * BAND_BLOCK), _F32),
            pltpu.VMEM((2, DIL_BLOCKS_PER_ITER, 2 * BAND_BLOCK, 2 * BAND_BLOCK), _BF16),
            pltpu.VMEM((2, DIL_BLOCKS_PER_ITER, BAND_BLOCK, LANES), _F32),
        ],
        compiler_params=_params("arbitrary", "arbitrary"),
        name="dil_attention",
    )(slopes, *([qkv] * 9))


def kernel(x, norm_mix_pre, norm_mix_post, norm_ffn_pre, norm_ffn_post, na_w_qkv, na_w_o, na_rpb, dil_w_qkv, dil_w_o,
           ffn_w_gate, ffn_w_up, ffn_w_down):
    x2d = x.reshape(N_TOK, D_MODEL)
    slopes = 2.0 ** (-8.0 * jnp.arange(1, N_HEADS + 1, dtype=_F32) / N_HEADS)

    def gain(g, layer):
        return g[layer].reshape(1, D_MODEL)

    for layer in range(DEPTH):
        j = layer // 2
        if layer % 2 == 0:
            qkv = _qkv_na(x2d, gain(norm_mix_pre, layer), na_w_qkv[j])
            o = _na_attention(qkv, (na_rpb[j].astype(_F32) * LOG2E).reshape(-1))
            w_o = na_w_o[j]
        else:
            qkv = _qkv_dil(x2d.reshape(BATCH, SEQ, D_MODEL), gain(norm_mix_pre, layer), dil_w_qkv[j])
            o = _dil_attention(qkv, slopes)
            w_o = dil_w_o[j]
        x2d = _mix_ffn(layer, o, x2d, w_o, gain(norm_mix_post, layer), gain(norm_ffn_pre, layer),
                       ffn_w_gate, ffn_w_up, ffn_w_down, gain(norm_ffn_post, layer))
    return x2d.reshape(BATCH, SEQ, D_MODEL)
```

```python
import functools

import jax
import jax.numpy as jnp
from jax import lax
from jax.experimental import pallas as pl
from jax.experimental.pallas import tpu as pltpu

D_MODEL = 1024
BATCH = 8
SEQ = 2048
DEPTH = 2
N_HEADS = 16
HEAD_DIM = 64
GRID_W = 64
GRID_ROWS = SEQ // GRID_W
NA_ROWS = 8
NA_COLS = 16
DIL_GROUPS = ((128, 1), (512, 4), (2048, 16))
N_GROUPS = len(DIL_GROUPS)
BAND_BLOCK = 128
D_FF = 2816
RMS_EPS = 1e-6
NEG_INF = -1e30
LOG2E = 1.4426950408889634
QK_SCALE = HEAD_DIM ** -0.5 * LOG2E

LANES = 128
N_PAIRS = D_MODEL // LANES
N_TOK = BATCH * SEQ
TM = 512
FFN_SLABS = 2
FF_SPLITS = ((0, 1536), (1536, 1280))
DIL_TN = 1024
CAST_COLS = 512
WEIGHT_STEPS = 8
SUBLANES = 8
MXU_DEPTH = 256
NA_ROWS_PER_ITER = 16
NA_BATCHES_PER_STEP = 4
DIL_BLOCKS_PER_ITER = 8
DIL_BATCHES_PER_STEP = 2
DIL_WINDOW_OFFSETS = (0, BAND_BLOCK // 2, BAND_BLOCK)
VMEM_LIMIT = 56 * 1024 * 1024

_BF16 = jnp.bfloat16
_F32 = jnp.float32


def _rms(x, g):
    ms = jnp.mean(x * x, axis=-1, keepdims=True)
    return (x * lax.rsqrt(ms + RMS_EPS)) * g


def _resident(shape):
    return pl.BlockSpec(shape, lambda *_: (0,) * len(shape), pipeline_mode=pl.Buffered(1))


def _params(*sem):
    return pltpu.CompilerParams(dimension_semantics=sem, vmem_limit_bytes=VMEM_LIMIT)


def _qkv_na_kernel(x_ref, g_ref, w_ref, o_ref, wbf_scr):
    @pl.when(pl.program_id(0) == 0)
    def _():
        for c in range(3 * D_MODEL // CAST_COLS):
            cols = slice(c * CAST_COLS, (c + 1) * CAST_COLS)
            wbf_scr[:, cols] = w_ref[:, cols].astype(_BF16)

    h = _rms(x_ref[...], g_ref[...]).astype(_BF16)
    for j in range(3):
        y = jnp.dot(h, wbf_scr[:, j * D_MODEL:(j + 1) * D_MODEL], preferred_element_type=_F32)
        if j == 0:
            y = y * QK_SCALE
        for c in range(N_PAIRS):
            o_ref[0, j * N_PAIRS + c] = y[:, c * LANES:(c + 1) * LANES].astype(_BF16)


def _qkv_na(x2d, gain, w):
    tiles_per_seq = SEQ // TM
    return pl.pallas_call(
        _qkv_na_kernel,
        grid=(N_TOK // TM,),
        in_specs=[
            pl.BlockSpec((TM, D_MODEL), lambda i: (i, 0)),
            _resident((1, D_MODEL)),
            _resident((D_MODEL, 3 * D_MODEL)),
        ],
        out_specs=pl.BlockSpec((1, 3 * N_PAIRS, TM, LANES),
                               lambda i: (i // tiles_per_seq, 0, i % tiles_per_seq, 0)),
        out_shape=jax.ShapeDtypeStruct((BATCH, 3 * N_PAIRS, SEQ, LANES), _BF16),
        scratch_shapes=[pltpu.VMEM((D_MODEL, 3 * D_MODEL), _BF16)],
        compiler_params=_params("arbitrary"),
        name="qkv_na",
    )(x2d, gain, w)


def _stack_heads(q, first_head):
    zero = jnp.zeros_like(q)
    return jnp.concatenate([jnp.where(first_head, q, zero), jnp.where(first_head, zero, q)], axis=0)


def _masked_probs(s, bias, cap):
    s = jnp.minimum(s + bias, cap)
    m = jnp.max(s, axis=-1, keepdims=True)
    return jnp.exp2(s - m).astype(_BF16), m


def _stacked_pv(p, v, first_head):
    rows = p.shape[0] // 2
    r = jnp.dot(p, jnp.concatenate([v, jnp.ones_like(v)], axis=1), preferred_element_type=_F32)
    num = jnp.where(first_head, r[:rows, :LANES], r[rows:, :LANES])
    den = jnp.where(first_head, r[:rows, LANES:], r[rows:, LANES:])
    return num * (1.0 / den), den


def _na_kernel(q_ref, k_ref, v_ref, rpb_ref, o_ref, bias_scr, cap_scr, p_scr):
    kh = NA_ROWS
    nk = kh * GRID_W
    per_iter = NA_ROWS_PER_ITER
    lane = lax.broadcasted_iota(jnp.int32, (GRID_W, LANES), 1)
    first_head = lane < HEAD_DIM

    @pl.when(pl.program_id(1) == 0)
    def _():
        n_row_off, n_col_off = 2 * kh - 1, 2 * NA_COLS - 1
        lane_off = lax.broadcasted_iota(jnp.int32, (1, LANES), 1) - (GRID_W - NA_COLS)
        diag_idx = jnp.clip(lane_off, 0, n_col_off - 1)
        for hh in range(2):
            for row_off in range(n_row_off):
                base = ((2 * pl.program_id(0) + hh) * n_row_off + row_off) * n_col_off
                diag = jnp.zeros((1, LANES), _F32)
                for c in range(n_col_off):
                    diag = jnp.where(diag_idx == c, rpb_ref[base + c], diag)
                diag = jnp.broadcast_to(diag, (GRID_W, LANES))
                tiles = [pltpu.roll(diag, 1 + GRID_W * (1 - half), 1, stride=1, stride_axis=0) for half in range(2)]
                for d in range(kh):
                    r = row_off - (kh - 1 - d)
                    if 0 <= r < kh:
                        half = slice((r % 2) * GRID_W, (r % 2 + 1) * GRID_W)
                        bias_scr[d, hh * GRID_W:(hh + 1) * GRID_W, r * GRID_W:(r + 1) * GRID_W] = tiles[r % 2][:, half]
        qcol = lax.broadcasted_iota(jnp.int32, (2 * GRID_W, LANES), 0) % GRID_W
        kcol = lax.broadcasted_iota(jnp.int32, (2 * GRID_W, LANES), 1) % GRID_W
        col_start = jnp.clip(qcol - NA_COLS // 2, 0, GRID_W - NA_COLS)
        col_ok = (kcol >= col_start) & (kcol < col_start + NA_COLS)
        cap_scr[...] = jnp.where(col_ok, jnp.finfo(_F32).max, NEG_INF).astype(_F32)

    def window(i):
        rs = min(max(i - kh // 2, 0), GRID_ROWS - kh)
        return rs, slice(rs * GRID_W, rs * GRID_W + nk)

    def scores(slot, u, row):
        bb, i = divmod(row, GRID_ROWS)
        rs, win = window(i)
        q = q_ref[bb, 0, i * GRID_W:(i + 1) * GRID_W, :]
        s = lax.dot_general(_stack_heads(q, first_head), k_ref[bb, 0, win, :], (((1,), (1,)), ((), ())),
                            preferred_element_type=_F32)
        cap = jnp.concatenate([cap_scr[...]] * (nk // LANES), axis=1)
        p_scr[slot, u], _ = _masked_probs(s, bias_scr[i - rs], cap)

    def outputs(slot, u, row):
        bb, i = divmod(row, GRID_ROWS)
        _, win = window(i)
        o, _ = _stacked_pv(p_scr[slot, u], v_ref[bb, 0, win, :], first_head)
        o_ref[bb, 0, i * GRID_W:(i + 1) * GRID_W, :] = o.astype(_BF16)

    n_stages = NA_BATCHES_PER_STEP * GRID_ROWS // per_iter
    for u in range(per_iter):
        scores(0, u, u)
    for stage in range(n_stages):
        for u in range(per_iter):
            outputs(stage % 2, u, stage * per_iter + u)
            if stage + 1 < n_stages:
                scores((stage + 1) % 2, u, (stage + 1) * per_iter + u)


def _na_attention(qkv, rpb_flat):
    slab = (NA_BATCHES_PER_STEP, 1, SEQ, LANES)
    nk = NA_ROWS * GRID_W
    return pl.pallas_call(
        _na_kernel,
        grid=(N_PAIRS, BATCH // NA_BATCHES_PER_STEP),
        in_specs=[
            pl.BlockSpec(slab, lambda p, b: (b, p, 0, 0)),
            pl.BlockSpec(slab, lambda p, b: (b, N_PAIRS + p, 0, 0)),
            pl.BlockSpec(slab, lambda p, b: (b, 2 * N_PAIRS + p, 0, 0)),
            pl.BlockSpec(memory_space=pltpu.SMEM),
        ],
        out_specs=pl.BlockSpec(slab, lambda p, b: (b, p, 0, 0)),
        out_shape=jax.ShapeDtypeStruct((BATCH, N_PAIRS, SEQ, LANES), _BF16),
        scratch_shapes=[pltpu.VMEM((NA_ROWS, 2 * GRID_W, nk), _F32),
                        pltpu.VMEM((2 * GRID_W, LANES), _F32),
                        pltpu.VMEM((2, NA_ROWS_PER_ITER, 2 * GRID_W, nk), _BF16)],
        compiler_params=_params("arbitrary", "arbitrary"),
        name="na_attention",
    )(qkv, qkv, qkv, rpb_flat)


def _mix_ffn_kernel(o_ref, x_ref, wo_ref, gmix_ref, gpre_ref, wg_ref, wu_ref, wd_ref, gpost_ref, out_ref,
                    wo_scr, wg_scr, wu_scr, wd_scr):
    i = pl.program_id(0)

    @pl.when(i < WEIGHT_STEPS)
    def _():
        for src, dst in ((wo_ref, wo_scr), (wg_ref, wg_scr), (wu_ref, wu_scr), (wd_ref, wd_scr)):
            n = src.shape[0]
            dst[pl.ds(pl.multiple_of(i * n, n), n), :] = src[...].astype(_BF16)

    @pl.when(i >= WEIGHT_STEPS)
    def _():
        halves = [slice(t * (TM // FFN_SLABS), (t + 1) * (TM // FFN_SLABS)) for t in range(FFN_SLABS)]
        ys = [jnp.dot(jnp.concatenate([o_ref[0, c, rows, :] for c in range(N_PAIRS)], axis=-1), wo_scr[...],
                      preferred_element_type=_F32) for rows in halves]
        xs, hs = [], []
        for rows, y in zip(halves, ys):
            x = x_ref[rows, :] + _rms(y, gmix_ref[...])
            xs.append(x)
            hs.append(_rms(x, gpre_ref[...]).astype(_BF16))
        acc = [None] * FFN_SLABS
        for start, size in FF_SPLITS:
            gu = [(jnp.dot(h, wg_scr[:, start:start + size], preferred_element_type=_F32),
                   jnp.dot(h, wu_scr[:, start:start + size], preferred_element_type=_F32)) for h in hs]
            for t, (g, u) in enumerate(gu):
                a = ((g * jax.nn.sigmoid(g)) * u).astype(_BF16)
                part = jnp.dot(a, wd_scr[start:start + size, :], preferred_element_type=_F32)
                acc[t] = part if acc[t] is None else acc[t] + part
        for rows, x, y in zip(halves, xs, acc):
            out_ref[rows, :] = x + _rms(y, gpost_ref[...])


def _mix_ffn(layer, o_cbm, x2d, wo, gmix, gpre, wg, wu, wd, gpost):
    tiles_per_seq = SEQ // TM

    def tile(i):
        return jnp.maximum(i - WEIGHT_STEPS, 0)

    def chunk(i):
        return jnp.minimum(i, WEIGHT_STEPS - 1)

    def weight_chunk(rows, cols):
        return pl.BlockSpec((rows // WEIGHT_STEPS, cols), lambda i: (chunk(i), 0))

    def layer_weight_chunk(rows, cols):
        return pl.BlockSpec((None, rows // WEIGHT_STEPS, cols), lambda i: (layer, chunk(i), 0))

    return pl.pallas_call(
        _mix_ffn_kernel,
        grid=(WEIGHT_STEPS + N_TOK // TM,),
        in_specs=[
            pl.BlockSpec((1, N_PAIRS, TM, LANES), lambda i: (tile(i) // tiles_per_seq, 0, tile(i) % tiles_per_seq, 0)),
            pl.BlockSpec((TM, D_MODEL), lambda i: (tile(i), 0)),
            weight_chunk(D_MODEL, D_MODEL),
            _resident((1, D_MODEL)),
            _resident((1, D_MODEL)),
            layer_weight_chunk(D_MODEL, D_FF),
            layer_weight_chunk(D_MODEL, D_FF),
            layer_weight_chunk(D_FF, D_MODEL),
            _resident((1, D_MODEL)),
        ],
        out_specs=pl.BlockSpec((TM, D_MODEL), lambda i: (tile(i), 0)),
        out_shape=jax.ShapeDtypeStruct((N_TOK, D_MODEL), _F32),
        scratch_shapes=[pltpu.VMEM((D_MODEL, D_MODEL), _BF16), pltpu.VMEM((D_MODEL, D_FF), _BF16),
                        pltpu.VMEM((D_MODEL, D_FF), _BF16), pltpu.VMEM((D_FF, D_MODEL), _BF16)],
        compiler_params=_params("arbitrary"),
        name="mix_ffn",
    )(o_cbm, x2d, wo, gmix, gpre, wg, wu, wd, gpost)


def _qkv_dil_kernel(x_ref, g_ref, w_ref, o_ref, h32_scr, hperm_scr):
    assert DIL_GROUPS[0][1] == 1
    j = pl.program_id(1)
    tiles_per_group = 3 * D_MODEL // DIL_TN

    def project(rows, w):
        y = jnp.dot(hperm_scr[rows, :], w, preferred_element_type=_F32)
        is_q_tile = j % tiles_per_group < D_MODEL // DIL_TN
        y = y * jnp.where(is_q_tile, QK_SCALE, 1.0)
        for c in range(DIL_TN // LANES):
            o_ref[0, c, rows, :] = y[:, c * LANES:(c + 1) * LANES].astype(_BF16)

    @pl.when(j == 0)
    def _():
        w = w_ref[...].astype(_BF16)
        for t in range(SEQ // TM):
            rows = slice(t * TM, (t + 1) * TM)
            h = _rms(x_ref[0, rows, :], g_ref[...])
            for c in range(N_PAIRS):
                h32_scr[c, rows, :] = h[:, c * LANES:(c + 1) * LANES]
            hperm_scr[rows, :] = h.astype(_BF16)
            project(rows, w)

    for grp, (_, dil) in enumerate(DIL_GROUPS):
        if dil == 1:
            continue

        @pl.when(j == grp * tiles_per_group)
        def _(dil=dil):
            w = w_ref[...].astype(_BF16)
            sub = SEQ // dil
            if dil <= SUBLANES:
                for r in range(dil):
                    rows = slice(r * sub, (r + 1) * sub)
                    for c in range(N_PAIRS):
                        val = h32_scr[c, pl.ds(r, sub, stride=dil), :]
                        hperm_scr[rows, c * LANES:(c + 1) * LANES] = val.astype(_BF16)
                    project(rows, w)
            else:
                per = MXU_DEPTH // dil
                out_row = lax.broadcasted_iota(jnp.int32, (MXU_DEPTH, MXU_DEPTH), 0)
                in_row = lax.broadcasted_iota(jnp.int32, (MXU_DEPTH, MXU_DEPTH), 1)
                pick = (in_row == dil * (out_row % per) + out_row // per).astype(_BF16)
                for a in range(SEQ // MXU_DEPTH):
                    rows = slice(a * MXU_DEPTH, (a + 1) * MXU_DEPTH)
                    chunk = jnp.concatenate([h32_scr[c, rows, :].astype(_BF16) for c in range(N_PAIRS)], axis=1)
                    picked = jnp.dot(pick, chunk, preferred_element_type=_F32).astype(_BF16)
                    for r in range(dil):
                        hperm_scr[r * sub + a * per:r * sub + (a + 1) * per, :] = picked[r * per:(r + 1) * per]
                project(slice(0, SEQ), w)

    @pl.when(j % tiles_per_group != 0)
    def _():
        project(slice(0, SEQ), w_ref[...].astype(_BF16))


def _qkv_dil(x3d, gain, w):
    n_col = N_GROUPS * 3 * D_MODEL
    blocks_per_tile = DIL_TN // LANES
    return pl.pallas_call(
        _qkv_dil_kernel,
        grid=(BATCH, n_col // DIL_TN),
        in_specs=[
            pl.BlockSpec((1, SEQ, D_MODEL), lambda b, j: (b, 0, 0)),
            _resident((1, D_MODEL)),
            pl.BlockSpec((D_MODEL, DIL_TN), lambda b, j: (0, j)),
        ],
        out_specs=pl.BlockSpec((1, blocks_per_tile, SEQ, LANES), lambda b, j: (b, j, 0, 0)),
        out_shape=jax.ShapeDtypeStruct((BATCH, n_col // LANES, SEQ, LANES), _BF16),
        scratch_shapes=[pltpu.VMEM((N_PAIRS, SEQ, LANES), _F32), pltpu.VMEM((SEQ, D_MODEL), _BF16)],
        compiler_params=_params("parallel", "arbitrary"),
        name="qkv_dil",
    )(x3d, gain, w)


def _dil_kernel(slopes_ref, q0, k0, v0, q1, k1, v1, q2, k2, v2, o_ref, og_scr, lg_scr, bias_scr, cap_scr, p_scr,
                m_scr):
    pair = pl.program_id(0)
    qkv_refs = ((q0, k0, v0), (q1, k1, v1), (q2, k2, v2))
    lane = lax.broadcasted_iota(jnp.int32, (BAND_BLOCK, LANES), 1)
    first_head = lane < HEAD_DIM
    max_width = 2 * BAND_BLOCK
    per_iter = DIL_BLOCKS_PER_ITER

    @pl.when(pl.program_id(1) == 0)
    def _():
        rel = (lax.broadcasted_iota(jnp.int32, (BAND_BLOCK, max_width), 0)
               - lax.broadcasted_iota(jnp.int32, (BAND_BLOCK, max_width), 1))
        for grp, (window, dil) in enumerate(DIL_GROUPS):
            radius = window // (2 * dil)
            for var, off in enumerate(DIL_WINDOW_OFFSETS):
                dist = jnp.abs(rel + off)
                distf = dist.astype(_F32)
                cap_scr[grp, var] = jnp.where(dist <= radius, jnp.finfo(_F32).max, NEG_INF).astype(_F32)
                for hh in range(2):
                    head_rows = slice(hh * BAND_BLOCK, (hh + 1) * BAND_BLOCK)
                    bias_scr[grp, var, head_rows, :] = -((slopes_ref[2 * pair + hh] * dil) * distf) * LOG2E

    def place(grp, t):
        window, dil = DIL_GROUPS[grp]
        radius = window // (2 * dil)
        sub = SEQ // dil
        blocks_per_sub = sub // BAND_BLOCK
        width = min(max_width, sub)
        res, c = divmod(t, blocks_per_sub)
        kstart = min(max(c * BAND_BLOCK - radius, 0), sub - width)
        var = DIL_WINDOW_OFFSETS.index(c * BAND_BLOCK - kstart)
        key_rows = slice(res * sub + kstart, res * sub + kstart + width)
        if dil == 1:
            rows = slice(t * BAND_BLOCK, (t + 1) * BAND_BLOCK)
        else:
            rows = pl.ds(c * (BAND_BLOCK * dil) + res, BAND_BLOCK, stride=dil)
        return var, key_rows, width, rows

    def scores(slot, u, unit):
        bb, grp, t = unit
        q_ref, k_ref, _ = qkv_refs[grp]
        var, key_rows, width, _ = place(grp, t)
        q = q_ref[bb, 0, t * BAND_BLOCK:(t + 1) * BAND_BLOCK, :]
        s = lax.dot_general(_stack_heads(q, first_head), k_ref[bb, 0, key_rows, :], (((1,), (1,)), ((), ())),
                            preferred_element_type=_F32)
        cap = cap_scr[grp, var, :, :width]
        p, m = _masked_probs(s, bias_scr[grp, var, :, :width], jnp.concatenate([cap, cap], axis=0))
        p_scr[slot, u, :, :width] = p
        m_scr[slot, u] = jnp.where(first_head, m[:BAND_BLOCK], m[BAND_BLOCK:])

    def outputs(slot, u, unit):
        bb, grp, t = unit
        v_ref = qkv_refs[grp][2]
        _, key_rows, width, rows = place(grp, t)
        o, den = _stacked_pv(p_scr[slot, u, :, :width], v_ref[bb, 0, key_rows, :], first_head)
        lse = m_scr[slot, u] + jnp.log2(den)
        if DIL_GROUPS[grp][1] > 1:
            og_scr[grp, rows, :] = o
            lg_scr[grp, rows, :] = lse
        else:
            others = [g for g in range(N_GROUPS) if g != grp]
            lses = [lse] + [lg_scr[g, rows, :] for g in others]
            outs = [o] + [og_scr[g, rows, :] for g in others]
            mx = functools.reduce(jnp.maximum, lses)
            e = [jnp.exp2(x - mx) for x in lses]
            num = functools.reduce(lambda a, b: a + b, [w * x for w, x in zip(e, outs)])
            o_ref[bb, 0, rows, :] = (num / functools.reduce(lambda a, b: a + b, e)).astype(_BF16)

    order = sorted(range(N_GROUPS), key=lambda g: -DIL_GROUPS[g][1])
    assert DIL_GROUPS[order[-1]][1] == 1
    units = [(bb, grp, t) for bb in range(DIL_BATCHES_PER_STEP) for grp in order for t in range(SEQ // BAND_BLOCK)]
    stages = [units[n:n + per_iter] for n in range(0, len(units), per_iter)]
    for u, unit in enumerate(stages[0]):
        scores(0, u, unit)
    for n, stage in enumerate(stages):
        for u, unit in enumerate(stage):
            outputs(n % 2, u, unit)
            if n + 1 < len(stages):
                scores((n + 1) % 2, u, stages[n + 1][u])


def _dil_attention(qkv, slopes):
    slab = (DIL_BATCHES_PER_STEP, 1, SEQ, LANES)

    def spec(grp, which):
        base = (grp * 3 + which) * N_PAIRS
        return pl.BlockSpec(slab, lambda p, b: (b, base + p, 0, 0))

    return pl.pallas_call(
        _dil_kernel,
        grid=(N_PAIRS, BATCH // DIL_BATCHES_PER_STEP),
        in_specs=[pl.BlockSpec(memory_space=pltpu.SMEM)]
        + [spec(grp, which) for grp in range(N_GROUPS) for which in range(3)],
        out_specs=pl.BlockSpec(slab, lambda p, b: (b, p, 0, 0)),
        out_shape=jax.ShapeDtypeStruct((BATCH, N_PAIRS, SEQ, LANES), _BF16),
        scratch_shapes=[
            pltpu.VMEM((N_GROUPS, SEQ, LANES), _F32),
            pltpu.VMEM((N_GROUPS, SEQ, LANES), _F32),
            pltpu.VMEM((N_GROUPS, len(DIL_WINDOW_OFFSETS), 2 * BAND_BLOCK, 2 * BAND_BLOCK), _F32),
            pltpu.VMEM((N_GROUPS, len(DIL_WINDOW_OFFSETS), BAND_BLOCK, 2 * BAND_BLOCK), _F32),
            pltpu.VMEM((2, DIL_BLOCKS_PER_ITER, 2 * BAND_BLOCK, 2 * BAND_BLOCK), _BF16),
            pltpu.VMEM((2, DIL_BLOCKS_PER_ITER, BAND_BLOCK, LANES), _F32),
        ],
        compiler_params=_params("arbitrary", "arbitrary"),
        name="dil_attention",
    )(slopes, *([qkv] * 9))


def kernel(x, norm_mix_pre, norm_mix_post, norm_ffn_pre, norm_ffn_post, na_w_qkv, na_w_o, na_rpb, dil_w_qkv, dil_w_o,
           ffn_w_gate, ffn_w_up, ffn_w_down):
    x2d = x.reshape(N_TOK, D_MODEL)
    slopes = 2.0 ** (-8.0 * jnp.arange(1, N_HEADS + 1, dtype=_F32) / N_HEADS)

    def gain(g, layer):
        return g[layer].reshape(1, D_MODEL)

    for layer in range(DEPTH):
        j = layer // 2
        if layer % 2 == 0:
            qkv = _qkv_na(x2d, gain(norm_mix_pre, layer), na_w_qkv[j])
            o = _na_attention(qkv, (na_rpb[j].astype(_F32) * LOG2E).reshape(-1))
            w_o = na_w_o[j]
        else:
            qkv = _qkv_dil(x2d.reshape(BATCH, SEQ, D_MODEL), gain(norm_mix_pre, layer), dil_w_qkv[j])
            o = _dil_attention(qkv, slopes)
            w_o = dil_w_o[j]
        x2d = _mix_ffn(layer, o, x2d, w_o, gain(norm_mix_post, layer), gain(norm_ffn_pre, layer),
                       ffn_w_gate, ffn_w_up, ffn_w_down, gain(norm_ffn_post, layer))
    return x2d.reshape(BATCH, SEQ, D_MODEL)
```

```python
import functools

import jax
import jax.numpy as jnp
from jax import lax
from jax.experimental import pallas as pl
from jax.experimental.pallas import tpu as pltpu

D_MODEL = 1024
BATCH = 8
SEQ = 2048
DEPTH = 2
N_HEADS = 16
HEAD_DIM = 64
GRID_W = 64
GRID_ROWS = SEQ // GRID_W
NA_ROWS = 8
NA_COLS = 16
DIL_GROUPS = ((128, 1), (512, 4), (2048, 16))
N_GROUPS = len(DIL_GROUPS)
BAND_BLOCK = 128
D_FF = 2816
RMS_EPS = 1e-6
NEG_INF = -1e30
LOG2E = 1.4426950408889634
QK_SCALE = HEAD_DIM ** -0.5 * LOG2E

LANES = 128
N_PAIRS = D_MODEL // LANES
N_TOK = BATCH * SEQ
TM = 512
FFN_SLABS = 2
FF_SPLITS = ((0, 1536), (1536, 1280))
DIL_TN = 1024
CAST_COLS = 512
WEIGHT_STEPS = 8
SUBLANES = 8
MXU_DEPTH = 256
NA_ROWS_PER_ITER = 4
NA_BATCHES_PER_STEP = 4
DIL_BLOCKS_PER_ITER = 4
DIL_BATCHES_PER_STEP = 2
DIL_WINDOW_OFFSETS = (0, BAND_BLOCK // 2, BAND_BLOCK)
VMEM_LIMIT = 56 * 1024 * 1024

_BF16 = jnp.bfloat16
_F32 = jnp.float32


def _rms(x, g):
    ms = jnp.mean(x * x, axis=-1, keepdims=True)
    return (x * lax.rsqrt(ms + RMS_EPS)) * g


def _resident(shape):
    return pl.BlockSpec(shape, lambda *_: (0,) * len(shape), pipeline_mode=pl.Buffered(1))


def _params(*sem):
    return pltpu.CompilerParams(dimension_semantics=sem, vmem_limit_bytes=VMEM_LIMIT)


def _qkv_na_kernel(x_ref, g_ref, w_ref, o_ref, wbf_scr):
    @pl.when(pl.program_id(0) == 0)
    def _():
        for c in range(3 * D_MODEL // CAST_COLS):
            cols = slice(c * CAST_COLS, (c + 1) * CAST_COLS)
            wbf_scr[:, cols] = w_ref[:, cols].astype(_BF16)

    h = _rms(x_ref[...], g_ref[...]).astype(_BF16)
    for j in range(3):
        y = jnp.dot(h, wbf_scr[:, j * D_MODEL:(j + 1) * D_MODEL], preferred_element_type=_F32)
        if j == 0:
            y = y * QK_SCALE
        for c in range(N_PAIRS):
            o_ref[0, j * N_PAIRS + c] = y[:, c * LANES:(c + 1) * LANES].astype(_BF16)


def _qkv_na(x2d, gain, w):
    tiles_per_seq = SEQ // TM
    return pl.pallas_call(
        _qkv_na_kernel,
        grid=(N_TOK // TM,),
        in_specs=[
            pl.BlockSpec((TM, D_MODEL), lambda i: (i, 0)),
            _resident((1, D_MODEL)),
            _resident((D_MODEL, 3 * D_MODEL)),
        ],
        out_specs=pl.BlockSpec((1, 3 * N_PAIRS, TM, LANES),
                               lambda i: (i // tiles_per_seq, 0, i % tiles_per_seq, 0)),
        out_shape=jax.ShapeDtypeStruct((BATCH, 3 * N_PAIRS, SEQ, LANES), _BF16),
        scratch_shapes=[pltpu.VMEM((D_MODEL, 3 * D_MODEL), _BF16)],
        compiler_params=_params("arbitrary"),
        name="qkv_na",
    )(x2d, gain, w)


def _stack_heads(q, first_head):
    zero = jnp.zeros_like(q)
    return jnp.concatenate([jnp.where(first_head, q, zero), jnp.where(first_head, zero, q)], axis=0)


def _masked_probs(s, bias, cap):
    s = jnp.minimum(s + bias, cap)
    m = jnp.max(s, axis=-1, keepdims=True)
    return jnp.exp2(s - m).astype(_BF16), m


def _stacked_pv(p, v, first_head):
    rows = p.shape[0] // 2
    r = jnp.dot(p, jnp.concatenate([v, jnp.ones_like(v)], axis=1), preferred_element_type=_F32)
    num = jnp.where(first_head, r[:rows, :LANES], r[rows:, :LANES])
    den = jnp.where(first_head, r[:rows, LANES:], r[rows:, LANES:])
    return num * (1.0 / den), den


def _na_kernel(q_ref, k_ref, v_ref, rpb_ref, o_ref, bias_scr, cap_scr, p_scr):
    kh = NA_ROWS
    nk = kh * GRID_W
    per_iter = NA_ROWS_PER_ITER
    lane = lax.broadcasted_iota(jnp.int32, (GRID_W, LANES), 1)
    first_head = lane < HEAD_DIM

    @pl.when(pl.program_id(1) == 0)
    def _():
        n_row_off, n_col_off = 2 * kh - 1, 2 * NA_COLS - 1
        lane_off = lax.broadcasted_iota(jnp.int32, (1, LANES), 1) - (GRID_W - NA_COLS)
        diag_idx = jnp.clip(lane_off, 0, n_col_off - 1)
        for hh in range(2):
            for row_off in range(n_row_off):
                base = ((2 * pl.program_id(0) + hh) * n_row_off + row_off) * n_col_off
                diag = jnp.zeros((1, LANES), _F32)
                for c in range(n_col_off):
                    diag = jnp.where(diag_idx == c, rpb_ref[base + c], diag)
                diag = jnp.broadcast_to(diag, (GRID_W, LANES))
                tiles = [pltpu.roll(diag, 1 + GRID_W * (1 - half), 1, stride=1, stride_axis=0) for half in range(2)]
                for d in range(kh):
                    r = row_off - (kh - 1 - d)
                    if 0 <= r < kh:
                        half = slice((r % 2) * GRID_W, (r % 2 + 1) * GRID_W)
                        bias_scr[d, hh * GRID_W:(hh + 1) * GRID_W, r * GRID_W:(r + 1) * GRID_W] = tiles[r % 2][:, half]
        qcol = lax.broadcasted_iota(jnp.int32, (2 * GRID_W, LANES), 0) % GRID_W
        kcol = lax.broadcasted_iota(jnp.int32, (2 * GRID_W, LANES), 1) % GRID_W
        col_start = jnp.clip(qcol - NA_COLS // 2, 0, GRID_W - NA_COLS)
        col_ok = (kcol >= col_start) & (kcol < col_start + NA_COLS)
        cap_scr[...] = jnp.where(col_ok, jnp.finfo(_F32).max, NEG_INF).astype(_F32)

    def window(i):
        rs = min(max(i - kh // 2, 0), GRID_ROWS - kh)
        return rs, slice(rs * GRID_W, rs * GRID_W + nk)

    def scores(slot, u, row):
        bb, i = divmod(row, GRID_ROWS)
        rs, win = window(i)
        q = q_ref[bb, 0, i * GRID_W:(i + 1) * GRID_W, :]
        s = lax.dot_general(_stack_heads(q, first_head), k_ref[bb, 0, win, :], (((1,), (1,)), ((), ())),
                            preferred_element_type=_F32)
        cap = jnp.concatenate([cap_scr[...]] * (nk // LANES), axis=1)
        p_scr[slot, u], _ = _masked_probs(s, bias_scr[i - rs], cap)

    def outputs(slot, u, row):
        bb, i = divmod(row, GRID_ROWS)
        _, win = window(i)
        o, _ = _stacked_pv(p_scr[slot, u], v_ref[bb, 0, win, :], first_head)
        o_ref[bb, 0, i * GRID_W:(i + 1) * GRID_W, :] = o.astype(_BF16)

    n_stages = NA_BATCHES_PER_STEP * GRID_ROWS // per_iter
    for u in range(per_iter):
        scores(0, u, u)
    for stage in range(n_stages):
        for u in range(per_iter):
            outputs(stage % 2, u, stage * per_iter + u)
            if stage + 1 < n_stages:
                scores((stage + 1) % 2, u, (stage + 1) * per_iter + u)


def _na_attention(qkv, rpb_flat):
    slab = (NA_BATCHES_PER_STEP, 1, SEQ, LANES)
    nk = NA_ROWS * GRID_W
    return pl.pallas_call(
        _na_kernel,
        grid=(N_PAIRS, BATCH // NA_BATCHES_PER_STEP),
        in_specs=[
            pl.BlockSpec(slab, lambda p, b: (b, p, 0, 0)),
            pl.BlockSpec(slab, lambda p, b: (b, N_PAIRS + p, 0, 0)),
            pl.BlockSpec(slab, lambda p, b: (b, 2 * N_PAIRS + p, 0, 0)),
            pl.BlockSpec(memory_space=pltpu.SMEM),
        ],
        out_specs=pl.BlockSpec(slab, lambda p, b: (b, p, 0, 0)),
        out_shape=jax.ShapeDtypeStruct((BATCH, N_PAIRS, SEQ, LANES), _BF16),
        scratch_shapes=[pltpu.VMEM((NA_ROWS, 2 * GRID_W, nk), _F32),
                        pltpu.VMEM((2 * GRID_W, LANES), _F32),
                        pltpu.VMEM((2, NA_ROWS_PER_ITER, 2 * GRID_W, nk), _BF16)],
        compiler_params=_params("arbitrary", "arbitrary"),
        name="na_attention",
    )(qkv, qkv, qkv, rpb_flat)


def _mix_ffn_kernel(o_ref, x_ref, wo_ref, gmix_ref, gpre_ref, wg_ref, wu_ref, wd_ref, gpost_ref, out_ref,
                    wo_scr, wg_scr, wu_scr, wd_scr):
    i = pl.program_id(0)

    @pl.when(i < WEIGHT_STEPS)
    def _():
        for src, dst in ((wo_ref, wo_scr), (wg_ref, wg_scr), (wu_ref, wu_scr), (wd_ref, wd_scr)):
            n = src.shape[0]
            dst[pl.ds(pl.multiple_of(i * n, n), n), :] = src[...].astype(_BF16)

    @pl.when(i >= WEIGHT_STEPS)
    def _():
        halves = [slice(t * (TM // FFN_SLABS), (t + 1) * (TM // FFN_SLABS)) for t in range(FFN_SLABS)]
        ys = [jnp.dot(jnp.concatenate([o_ref[0, c, rows, :] for c in range(N_PAIRS)], axis=-1), wo_scr[...],
                      preferred_element_type=_F32) for rows in halves]
        xs, hs = [], []
        for rows, y in zip(halves, ys):
            x = x_ref[rows, :] + _rms(y, gmix_ref[...])
            xs.append(x)
            hs.append(_rms(x, gpre_ref[...]).astype(_BF16))
        acc = [None] * FFN_SLABS
        for start, size in FF_SPLITS:
            gu = [(jnp.dot(h, wg_scr[:, start:start + size], preferred_element_type=_F32),
                   jnp.dot(h, wu_scr[:, start:start + size], preferred_element_type=_F32)) for h in hs]
            for t, (g, u) in enumerate(gu):
                a = ((g * jax.nn.sigmoid(g)) * u).astype(_BF16)
                part = jnp.dot(a, wd_scr[start:start + size, :], preferred_element_type=_F32)
                acc[t] = part if acc[t] is None else acc[t] + part
        for rows, x, y in zip(halves, xs, acc):
            out_ref[rows, :] = x + _rms(y, gpost_ref[...])


def _mix_ffn(layer, o_cbm, x2d, wo, gmix, gpre, wg, wu, wd, gpost):
    tiles_per_seq = SEQ // TM

    def tile(i):
        return jnp.maximum(i - WEIGHT_STEPS, 0)

    def chunk(i):
        return jnp.minimum(i, WEIGHT_STEPS - 1)

    def weight_chunk(rows, cols):
        return pl.BlockSpec((rows // WEIGHT_STEPS, cols), lambda i: (chunk(i), 0))

    def layer_weight_chunk(rows, cols):
        return pl.BlockSpec((None, rows // WEIGHT_STEPS, cols), lambda i: (layer, chunk(i), 0))

    return pl.pallas_call(
        _mix_ffn_kernel,
        grid=(WEIGHT_STEPS + N_TOK // TM,),
        in_specs=[
            pl.BlockSpec((1, N_PAIRS, TM, LANES), lambda i: (tile(i) // tiles_per_seq, 0, tile(i) % tiles_per_seq, 0)),
            pl.BlockSpec((TM, D_MODEL), lambda i: (tile(i), 0)),
            weight_chunk(D_MODEL, D_MODEL),
            _resident((1, D_MODEL)),
            _resident((1, D_MODEL)),
            layer_weight_chunk(D_MODEL, D_FF),
            layer_weight_chunk(D_MODEL, D_FF),
            layer_weight_chunk(D_FF, D_MODEL),
            _resident((1, D_MODEL)),
        ],
        out_specs=pl.BlockSpec((TM, D_MODEL), lambda i: (tile(i), 0)),
        out_shape=jax.ShapeDtypeStruct((N_TOK, D_MODEL), _F32),
        scratch_shapes=[pltpu.VMEM((D_MODEL, D_MODEL), _BF16), pltpu.VMEM((D_MODEL, D_FF), _BF16),
                        pltpu.VMEM((D_MODEL, D_FF), _BF16), pltpu.VMEM((D_FF, D_MODEL), _BF16)],
        compiler_params=_params("arbitrary"),
        name="mix_ffn",
    )(o_cbm, x2d, wo, gmix, gpre, wg, wu, wd, gpost)


def _qkv_dil_kernel(x_ref, g_ref, w_ref, o_ref, h32_scr, hperm_scr):
    assert DIL_GROUPS[0][1] == 1
    j = pl.program_id(1)
    tiles_per_group = 3 * D_MODEL // DIL_TN

    def project(rows, w):
        y = jnp.dot(hperm_scr[rows, :], w, preferred_element_type=_F32)
        is_q_tile = j % tiles_per_group < D_MODEL // DIL_TN
        y = y * jnp.where(is_q_tile, QK_SCALE, 1.0)
        for c in range(DIL_TN // LANES):
            o_ref[0, c, rows, :] = y[:, c * LANES:(c + 1) * LANES].astype(_BF16)

    @pl.when(j == 0)
    def _():
        w = w_ref[...].astype(_BF16)
        for t in range(SEQ // TM):
            rows = slice(t * TM, (t + 1) * TM)
            h = _rms(x_ref[0, rows, :], g_ref[...])
            for c in range(N_PAIRS):
                h32_scr[c, rows, :] = h[:, c * LANES:(c + 1) * LANES]
            hperm_scr[rows, :] = h.astype(_BF16)
            project(rows, w)

    for grp, (_, dil) in enumerate(DIL_GROUPS):
        if dil == 1:
            continue

        @pl.when(j == grp * tiles_per_group)
        def _(dil=dil):
            w = w_ref[...].astype(_BF16)
            sub = SEQ // dil
            if dil <= SUBLANES:
                for r in range(dil):
                    rows = slice(r * sub, (r + 1) * sub)
                    for c in range(N_PAIRS):
                        val = h32_scr[c, pl.ds(r, sub, stride=dil), :]
                        hperm_scr[rows, c * LANES:(c + 1) * LANES] = val.astype(_BF16)
                    project(rows, w)
            else:
                per = MXU_DEPTH // dil
                out_row = lax.broadcasted_iota(jnp.int32, (MXU_DEPTH, MXU_DEPTH), 0)
                in_row = lax.broadcasted_iota(jnp.int32, (MXU_DEPTH, MXU_DEPTH), 1)
                pick = (in_row == dil * (out_row % per) + out_row // per).astype(_BF16)
                for a in range(SEQ // MXU_DEPTH):
                    rows = slice(a * MXU_DEPTH, (a + 1) * MXU_DEPTH)
                    chunk = jnp.concatenate([h32_scr[c, rows, :].astype(_BF16) for c in range(N_PAIRS)], axis=1)
                    picked = jnp.dot(pick, chunk, preferred_element_type=_F32).astype(_BF16)
                    for r in range(dil):
                        hperm_scr[r * sub + a * per:r * sub + (a + 1) * per, :] = picked[r * per:(r + 1) * per]
                project(slice(0, SEQ), w)

    @pl.when(j % tiles_per_group != 0)
    def _():
        project(slice(0, SEQ), w_ref[...].astype(_BF16))


def _qkv_dil(x3d, gain, w):
    n_col = N_GROUPS * 3 * D_MODEL
    blocks_per_tile = DIL_TN // LANES
    return pl.pallas_call(
        _qkv_dil_kernel,
        grid=(BATCH, n_col // DIL_TN),
        in_specs=[
            pl.BlockSpec((1, SEQ, D_MODEL), lambda b, j: (b, 0, 0)),
            _resident((1, D_MODEL)),
            pl.BlockSpec((D_MODEL, DIL_TN), lambda b, j: (0, j)),
        ],
        out_specs=pl.BlockSpec((1, blocks_per_tile, SEQ, LANES), lambda b, j: (b, j, 0, 0)),
        out_shape=jax.ShapeDtypeStruct((BATCH, n_col // LANES, SEQ, LANES), _BF16),
        scratch_shapes=[pltpu.VMEM((N_PAIRS, SEQ, LANES), _F32), pltpu.VMEM((SEQ, D_MODEL), _BF16)],
        compiler_params=_params("parallel", "arbitrary"),
        name="qkv_dil",
    )(x3d, gain, w)


def _dil_kernel(slopes_ref, q0, k0, v0, q1, k1, v1, q2, k2, v2, o_ref, og_scr, lg_scr, bias_scr, cap_scr, p_scr,
                m_scr):
    pair = pl.program_id(0)
    qkv_refs = ((q0, k0, v0), (q1, k1, v1), (q2, k2, v2))
    lane = lax.broadcasted_iota(jnp.int32, (BAND_BLOCK, LANES), 1)
    first_head = lane < HEAD_DIM
    max_width = 2 * BAND_BLOCK
    per_iter = DIL_BLOCKS_PER_ITER

    @pl.when(pl.program_id(1) == 0)
    def _():
        rel = (lax.broadcasted_iota(jnp.int32, (BAND_BLOCK, max_width), 0)
               - lax.broadcasted_iota(jnp.int32, (BAND_BLOCK, max_width), 1))
        for grp, (window, dil) in enumerate(DIL_GROUPS):
            radius = window // (2 * dil)
            for var, off in enumerate(DIL_WINDOW_OFFSETS):
                dist = jnp.abs(rel + off)
                distf = dist.astype(_F32)
                cap_scr[grp, var] = jnp.where(dist <= radius, jnp.finfo(_F32).max, NEG_INF).astype(_F32)
                for hh in range(2):
                    head_rows = slice(hh * BAND_BLOCK, (hh + 1) * BAND_BLOCK)
                    bias_scr[grp, var, head_rows, :] = -((slopes_ref[2 * pair + hh] * dil) * distf) * LOG2E

    def place(grp, t):
        window, dil = DIL_GROUPS[grp]
        radius = window // (2 * dil)
        sub = SEQ // dil
        blocks_per_sub = sub // BAND_BLOCK
        width = min(max_width, sub)
        res, c = divmod(t, blocks_per_sub)
        kstart = min(max(c * BAND_BLOCK - radius, 0), sub - width)
        var = DIL_WINDOW_OFFSETS.index(c * BAND_BLOCK - kstart)
        key_rows = slice(res * sub + kstart, res * sub + kstart + width)
        if dil == 1:
            rows = slice(t * BAND_BLOCK, (t + 1) * BAND_BLOCK)
        else:
            rows = pl.ds(c * (BAND_BLOCK * dil) + res, BAND_BLOCK, stride=dil)
        return var, key_rows, width, rows

    def scores(slot, u, unit):
        bb, grp, t = unit
        q_ref, k_ref, _ = qkv_refs[grp]
        var, key_rows, width, _ = place(grp, t)
        q = q_ref[bb, 0, t * BAND_BLOCK:(t + 1) * BAND_BLOCK, :]
        s = lax.dot_general(_stack_heads(q, first_head), k_ref[bb, 0, key_rows, :], (((1,), (1,)), ((), ())),
                            preferred_element_type=_F32)
        cap = cap_scr[grp, var, :, :width]
        p, m = _masked_probs(s, bias_scr[grp, var, :, :width], jnp.concatenate([cap, cap], axis=0))
        p_scr[slot, u, :, :width] = p
        m_scr[slot, u] = jnp.where(first_head, m[:BAND_BLOCK], m[BAND_BLOCK:])

    def outputs(slot, u, unit):
        bb, grp, t = unit
        v_ref = qkv_refs[grp][2]
        _, key_rows, width, rows = place(grp, t)
        o, den = _stacked_pv(p_scr[slot, u, :, :width], v_ref[bb, 0, key_rows, :], first_head)
        lse = m_scr[slot, u] + jnp.log2(den)
        if DIL_GROUPS[grp][1] > 1:
            og_scr[grp, rows, :] = o
            lg_scr[grp, rows, :] = lse
        else:
            others = [g for g in range(N_GROUPS) if g != grp]
            lses = [lse] + [lg_scr[g, rows, :] for g in others]
            outs = [o] + [og_scr[g, rows, :] for g in others]
            mx = functools.reduce(jnp.maximum, lses)
            e = [jnp.exp2(x - mx) for x in lses]
            num = functools.reduce(lambda a, b: a + b, [w * x for w, x in zip(e, outs)])
            o_ref[bb, 0, rows, :] = (num / functools.reduce(lambda a, b: a + b, e)).astype(_BF16)

    order = sorted(range(N_GROUPS), key=lambda g: -DIL_GROUPS[g][1])
    assert DIL_GROUPS[order[-1]][1] == 1
    units = [(bb, grp, t) for bb in range(DIL_BATCHES_PER_STEP) for grp in order for t in range(SEQ // BAND_BLOCK)]
    stages = [units[n:n + per_iter] for n in range(0, len(units), per_iter)]
    for u, unit in enumerate(stages[0]):
        scores(0, u, unit)
    for n, stage in enumerate(stages):
        for u, unit in enumerate(stage):
            outputs(n % 2, u, unit)
            if n + 1 < len(stages):
                scores((n + 1) % 2, u, stages[n + 1][u])


def _dil_attention(qkv, slopes):
    slab = (DIL_BATCHES_PER_STEP, 1, SEQ, LANES)

    def spec(grp, which):
        base = (grp * 3 + which) * N_PAIRS
        return pl.BlockSpec(slab, lambda p, b: (b, base + p, 0, 0))

    return pl.pallas_call(
        _dil_kernel,
        grid=(N_PAIRS, BATCH // DIL_BATCHES_PER_STEP),
        in_specs=[pl.BlockSpec(memory_space=pltpu.SMEM)]
        + [spec(grp, which) for grp in range(N_GROUPS) for which in range(3)],
        out_specs=pl.BlockSpec(slab, lambda p, b: (b, p, 0, 0)),
        out_shape=jax.ShapeDtypeStruct((BATCH, N_PAIRS, SEQ, LANES), _BF16),
        scratch_shapes=[
            pltpu.VMEM((N_GROUPS, SEQ, LANES), _F32),
            pltpu.VMEM((N_GROUPS, SEQ, LANES), _F32),
            pltpu.VMEM((N_GROUPS, len(DIL_WINDOW_OFFSETS), 2 * BAND_BLOCK, 2 * BAND_BLOCK), _F32),
            pltpu.VMEM((N_GROUPS, len(DIL_WINDOW_OFFSETS), BAND_BLOCK, 2 * BAND_BLOCK), _F32),
            pltpu.VMEM((2, DIL_BLOCKS_PER_ITER, 2 * BAND_BLOCK, 2 * BAND_BLOCK), _BF16),
            pltpu.VMEM((2, DIL_BLOCKS_PER_ITER, BAND_BLOCK, LANES), _F32),
        ],
        compiler_params=_params("arbitrary", "arbitrary"),
        name="dil_attention",
    )(slopes, *([qkv] * 9))


def kernel(x, norm_mix_pre, norm_mix_post, norm_ffn_pre, norm_ffn_post, na_w_qkv, na_w_o, na_rpb, dil_w_qkv, dil_w_o,
           ffn_w_gate, ffn_w_up, ffn_w_down):
    x2d = x.reshape(N_TOK, D_MODEL)
    slopes = 2.0 ** (-8.0 * jnp.arange(1, N_HEADS + 1, dtype=_F32) / N_HEADS)

    def gain(g, layer):
        return g[layer].reshape(1, D_MODEL)

    for layer in range(DEPTH):
        j = layer // 2
        if layer % 2 == 0:
            qkv = _qkv_na(x2d, gain(norm_mix_pre, layer), na_w_qkv[j])
            o = _na_attention(qkv, (na_rpb[j].astype(_F32) * LOG2E).reshape(-1))
            w_o = na_w_o[j]
        else:
            qkv = _qkv_dil(x2d.reshape(BATCH, SEQ, D_MODEL), gain(norm_mix_pre, layer), dil_w_qkv[j])
            o = _dil_attention(qkv, slopes)
            w_o = dil_w_o[j]
        x2d = _mix_ffn(layer, o, x2d, w_o, gain(norm_mix_post, layer), gain(norm_ffn_pre, layer),
                       ffn_w_gate, ffn_w_up, ffn_w_down, gain(norm_ffn_post, layer))
    return x2d.reshape(BATCH, SEQ, D_MODEL)
```

```python
import functools

import jax
import jax.numpy as jnp
from jax import lax
from jax.experimental import pallas as pl
from jax.experimental.pallas import tpu as pltpu

D_MODEL = 1024
BATCH = 8
SEQ = 2048
DEPTH = 2
N_HEADS = 16
HEAD_DIM = 64
GRID_W = 64
GRID_ROWS = SEQ // GRID_W
NA_ROWS = 8
NA_COLS = 16
DIL_GROUPS = ((128, 1), (512, 4), (2048, 16))
N_GROUPS = len(DIL_GROUPS)
BAND_BLOCK = 128
D_FF = 2816
RMS_EPS = 1e-6
NEG_INF = -1e30
LOG2E = 1.4426950408889634
QK_SCALE = HEAD_DIM ** -0.5 * LOG2E

LANES = 128
N_PAIRS = D_MODEL // LANES
N_TOK = BATCH * SEQ
TM = 512
FFN_SLABS = 2
FF_SPLITS = ((0, 1536), (1536, 1280))
DIL_TN = 1024
CAST_COLS = 512
WEIGHT_STEPS = 8
SUBLANES = 8
MXU_DEPTH = 256
NA_ROWS_PER_ITER = 4
NA_BATCHES_PER_STEP = 4
DIL_BLOCKS_PER_ITER = 8
DIL_BATCHES_PER_STEP = 2
DIL_WINDOW_OFFSETS = (0, BAND_BLOCK // 2, BAND_BLOCK)
VMEM_LIMIT = 56 * 1024 * 1024

_BF16 = jnp.bfloat16
_F32 = jnp.float32


def _rms(x, g):
    ms = jnp.mean(x * x, axis=-1, keepdims=True)
    return (x * lax.rsqrt(ms + RMS_EPS)) * g


def _resident(shape):
    return pl.BlockSpec(shape, lambda *_: (0,) * len(shape), pipeline_mode=pl.Buffered(1))


def _params(*sem):
    return pltpu.CompilerParams(dimension_semantics=sem, vmem_limit_bytes=VMEM_LIMIT)


def _qkv_na_kernel(x_ref, g_ref, w_ref, o_ref, wbf_scr):
    @pl.when(pl.program_id(0) == 0)
    def _():
        for c in range(3 * D_MODEL // CAST_COLS):
            cols = slice(c * CAST_COLS, (c + 1) * CAST_COLS)
            wbf_scr[:, cols] = w_ref[:, cols].astype(_BF16)

    h = _rms(x_ref[...], g_ref[...]).astype(_BF16)
    for j in range(3):
        y = jnp.dot(h, wbf_scr[:, j * D_MODEL:(j + 1) * D_MODEL], preferred_element_type=_F32)
        if j == 0:
            y = y * QK_SCALE
        for c in range(N_PAIRS):
            o_ref[0, j * N_PAIRS + c] = y[:, c * LANES:(c + 1) * LANES].astype(_BF16)


def _qkv_na(x2d, gain, w):
    tiles_per_seq = SEQ // TM
    return pl.pallas_call(
        _qkv_na_kernel,
        grid=(N_TOK // TM,),
        in_specs=[
            pl.BlockSpec((TM, D_MODEL), lambda i: (i, 0)),
            _resident((1, D_MODEL)),
            _resident((D_MODEL, 3 * D_MODEL)),
        ],
        out_specs=pl.BlockSpec((1, 3 * N_PAIRS, TM, LANES),
                               lambda i: (i // tiles_per_seq, 0, i % tiles_per_seq, 0)),
        out_shape=jax.ShapeDtypeStruct((BATCH, 3 * N_PAIRS, SEQ, LANES), _BF16),
        scratch_shapes=[pltpu.VMEM((D_MODEL, 3 * D_MODEL), _BF16)],
        compiler_params=_params("arbitrary"),
        name="qkv_na",
    )(x2d, gain, w)


def _stack_heads(q, first_head):
    zero = jnp.zeros_like(q)
    return jnp.concatenate([jnp.where(first_head, q, zero), jnp.where(first_head, zero, q)], axis=0)


def _masked_probs(s, bias, cap):
    s = jnp.minimum(s + bias, cap)
    m = jnp.max(s, axis=-1, keepdims=True)
    return jnp.exp2(s - m).astype(_BF16), m


def _stacked_pv(p, v, first_head):
    rows = p.shape[0] // 2
    r = jnp.dot(p, jnp.concatenate([v, jnp.ones_like(v)], axis=1), preferred_element_type=_F32)
    num = jnp.where(first_head, r[:rows, :LANES], r[rows:, :LANES])
    den = jnp.where(first_head, r[:rows, LANES:], r[rows:, LANES:])
    return num * (1.0 / den), den


def _na_kernel(q_ref, k_ref, v_ref, rpb_ref, o_ref, bias_scr, cap_scr, p_scr):
    kh = NA_ROWS
    nk = kh * GRID_W
    per_iter = NA_ROWS_PER_ITER
    lane = lax.broadcasted_iota(jnp.int32, (GRID_W, LANES), 1)
    first_head = lane < HEAD_DIM

    @pl.when(pl.program_id(1) == 0)
    def _():
        n_row_off, n_col_off = 2 * kh - 1, 2 * NA_COLS - 1
        lane_off = lax.broadcasted_iota(jnp.int32, (1, LANES), 1) - (GRID_W - NA_COLS)
        diag_idx = jnp.clip(lane_off, 0, n_col_off - 1)
        for hh in range(2):
            for row_off in range(n_row_off):
                base = ((2 * pl.program_id(0) + hh) * n_row_off + row_off) * n_col_off
                diag = jnp.zeros((1, LANES), _F32)
                for c in range(n_col_off):
                    diag = jnp.where(diag_idx == c, rpb_ref[base + c], diag)
                diag = jnp.broadcast_to(diag, (GRID_W, LANES))
                tiles = [pltpu.roll(diag, 1 + GRID_W * (1 - half), 1, stride=1, stride_axis=0) for half in range(2)]
                for d in range(kh):
                    r = row_off - (kh - 1 - d)
                    if 0 <= r < kh:
                        half = slice((r % 2) * GRID_W, (r % 2 + 1) * GRID_W)
                        bias_scr[d, hh * GRID_W:(hh + 1) * GRID_W, r * GRID_W:(r + 1) * GRID_W] = tiles[r % 2][:, half]
        qcol = lax.broadcasted_iota(jnp.int32, (2 * GRID_W, LANES), 0) % GRID_W
        kcol = lax.broadcasted_iota(jnp.int32, (2 * GRID_W, LANES), 1) % GRID_W
        col_start = jnp.clip(qcol - NA_COLS // 2, 0, GRID_W - NA_COLS)
        col_ok = (kcol >= col_start) & (kcol < col_start + NA_COLS)
        cap_scr[...] = jnp.where(col_ok, jnp.finfo(_F32).max, NEG_INF).astype(_F32)

    def window(i):
        rs = min(max(i - kh // 2, 0), GRID_ROWS - kh)
        return rs, slice(rs * GRID_W, rs * GRID_W + nk)

    def scores(slot, u, row):
        bb, i = divmod(row, GRID_ROWS)
        rs, win = window(i)
        q = q_ref[bb, 0, i * GRID_W:(i + 1) * GRID_W, :]
        s = lax.dot_general(_stack_heads(q, first_head), k_ref[bb, 0, win, :], (((1,), (1,)), ((), ())),
                            preferred_element_type=_F32)
        cap = jnp.concatenate([cap_scr[...]] * (nk // LANES), axis=1)
        p_scr[slot, u], _ = _masked_probs(s, bias_scr[i - rs], cap)

    def outputs(slot, u, row):
        bb, i = divmod(row, GRID_ROWS)
        _, win = window(i)
        o, _ = _stacked_pv(p_scr[slot, u], v_ref[bb, 0, win, :], first_head)
        o_ref[bb, 0, i * GRID_W:(i + 1) * GRID_W, :] = o.astype(_BF16)

    n_stages = NA_BATCHES_PER_STEP * GRID_ROWS // per_iter
    for u in range(per_iter):
        scores(0, u, u)
    for stage in range(n_stages):
        for u in range(per_iter):
            outputs(stage % 2, u, stage * per_iter + u)
            if stage + 1 < n_stages:
                scores((stage + 1) % 2, u, (stage + 1) * per_iter + u)


def _na_attention(qkv, rpb_flat):
    slab = (NA_BATCHES_PER_STEP, 1, SEQ, LANES)
    nk = NA_ROWS * GRID_W
    return pl.pallas_call(
        _na_kernel,
        grid=(N_PAIRS, BATCH // NA_BATCHES_PER_STEP),
        in_specs=[
            pl.BlockSpec(slab, lambda p, b: (b, p, 0, 0)),
            pl.BlockSpec(slab, lambda p, b: (b, N_PAIRS + p, 0, 0)),
            pl.BlockSpec(slab, lambda p, b: (b, 2 * N_PAIRS + p, 0, 0)),
            pl.BlockSpec(memory_space=pltpu.SMEM),
        ],
        out_specs=pl.BlockSpec(slab, lambda p, b: (b, p, 0, 0)),
        out_shape=jax.ShapeDtypeStruct((BATCH, N_PAIRS, SEQ, LANES), _BF16),
        scratch_shapes=[pltpu.VMEM((NA_ROWS, 2 * GRID_W, nk), _F32),
                        pltpu.VMEM((2 * GRID_W, LANES), _F32),
                        pltpu.VMEM((2, NA_ROWS_PER_ITER, 2 * GRID_W, nk), _BF16)],
        compiler_params=_params("arbitrary", "arbitrary"),
        name="na_attention",
    )(qkv, qkv, qkv, rpb_flat)


def _mix_ffn_kernel(o_ref, x_ref, wo_ref, gmix_ref, gpre_ref, wg_ref, wu_ref, wd_ref, gpost_ref, out_ref,
                    wo_scr, wg_scr, wu_scr, wd_scr):
    i = pl.program_id(0)

    @pl.when(i < WEIGHT_STEPS)
    def _():
        for src, dst in ((wo_ref, wo_scr), (wg_ref, wg_scr), (wu_ref, wu_scr), (wd_ref, wd_scr)):
            n = src.shape[0]
            dst[pl.ds(pl.multiple_of(i * n, n), n), :] = src[...].astype(_BF16)

    @pl.when(i >= WEIGHT_STEPS)
    def _():
        halves = [slice(t * (TM // FFN_SLABS), (t + 1) * (TM // FFN_SLABS)) for t in range(FFN_SLABS)]
        ys = [jnp.dot(jnp.concatenate([o_ref[0, c, rows, :] for c in range(N_PAIRS)], axis=-1), wo_scr[...],
                      preferred_element_type=_F32) for rows in halves]
        xs, hs = [], []
        for rows, y in zip(halves, ys):
            x = x_ref[rows, :] + _rms(y, gmix_ref[...])
            xs.append(x)
            hs.append(_rms(x, gpre_ref[...]).astype(_BF16))
        acc = [None] * FFN_SLABS
        for start, size in FF_SPLITS:
            gu = [(jnp.dot(h, wg_scr[:, start:start + size], preferred_element_type=_F32),
                   jnp.dot(h, wu_scr[:, start:start + size], preferred_element_type=_F32)) for h in hs]
            for t, (g, u) in enumerate(gu):
                a = ((g * jax.nn.sigmoid(g)) * u).astype(_BF16)
                part = jnp.dot(a, wd_scr[start:start + size, :], preferred_element_type=_F32)
                acc[t] = part if acc[t] is None else acc[t] + part
        for rows, x, y in zip(halves, xs, acc):
            out_ref[rows, :] = x + _rms(y, gpost_ref[...])


def _mix_ffn(layer, o_cbm, x2d, wo, gmix, gpre, wg, wu, wd, gpost):
    tiles_per_seq = SEQ // TM

    def tile(i):
        return jnp.maximum(i - WEIGHT_STEPS, 0)

    def chunk(i):
        return jnp.minimum(i, WEIGHT_STEPS - 1)

    def weight_chunk(rows, cols):
        return pl.BlockSpec((rows // WEIGHT_STEPS, cols), lambda i: (chunk(i), 0))

    def layer_weight_chunk(rows, cols):
        return pl.BlockSpec((None, rows // WEIGHT_STEPS, cols), lambda i: (layer, chunk(i), 0))

    return pl.pallas_call(
        _mix_ffn_kernel,
        grid=(WEIGHT_STEPS + N_TOK // TM,),
        in_specs=[
            pl.BlockSpec((1, N_PAIRS, TM, LANES), lambda i: (tile(i) // tiles_per_seq, 0, tile(i) % tiles_per_seq, 0)),
            pl.BlockSpec((TM, D_MODEL), lambda i: (tile(i), 0)),
            weight_chunk(D_MODEL, D_MODEL),
            _resident((1, D_MODEL)),
            _resident((1, D_MODEL)),
            layer_weight_chunk(D_MODEL, D_FF),
            layer_weight_chunk(D_MODEL, D_FF),
            layer_weight_chunk(D_FF, D_MODEL),
            _resident((1, D_MODEL)),
        ],
        out_specs=pl.BlockSpec((TM, D_MODEL), lambda i: (tile(i), 0)),
        out_shape=jax.ShapeDtypeStruct((N_TOK, D_MODEL), _F32),
        scratch_shapes=[pltpu.VMEM((D_MODEL, D_MODEL), _BF16), pltpu.VMEM((D_MODEL, D_FF), _BF16),
                        pltpu.VMEM((D_MODEL, D_FF), _BF16), pltpu.VMEM((D_FF, D_MODEL), _BF16)],
        compiler_params=_params("arbitrary"),
        name="mix_ffn",
    )(o_cbm, x2d, wo, gmix, gpre, wg, wu, wd, gpost)


def _qkv_dil_kernel(x_ref, g_ref, w_ref, o_ref, h32_scr, hperm_scr):
    assert DIL_GROUPS[0][1] == 1
    j = pl.program_id(1)
    tiles_per_group = 3 * D_MODEL // DIL_TN

    def project(rows, w):
        y = jnp.dot(hperm_scr[rows, :], w, preferred_element_type=_F32)
        is_q_tile = j % tiles_per_group < D_MODEL // DIL_TN
        y = y * jnp.where(is_q_tile, QK_SCALE, 1.0)
        for c in range(DIL_TN // LANES):
            o_ref[0, c, rows, :] = y[:, c * LANES:(c + 1) * LANES].astype(_BF16)

    @pl.when(j == 0)
    def _():
        w = w_ref[...].astype(_BF16)
        for t in range(SEQ // TM):
            rows = slice(t * TM, (t + 1) * TM)
            h = _rms(x_ref[0, rows, :], g_ref[...])
            for c in range(N_PAIRS):
                h32_scr[c, rows, :] = h[:, c * LANES:(c + 1) * LANES]
            hperm_scr[rows, :] = h.astype(_BF16)
            project(rows, w)

    for grp, (_, dil) in enumerate(DIL_GROUPS):
        if dil == 1:
            continue

        @pl.when(j == grp * tiles_per_group)
        def _(dil=dil):
            w = w_ref[...].astype(_BF16)
            sub = SEQ // dil
            if dil <= SUBLANES:
                for r in range(dil):
                    rows = slice(r * sub, (r + 1) * sub)
                    for c in range(N_PAIRS):
                        val = h32_scr[c, pl.ds(r, sub, stride=dil), :]
                        hperm_scr[rows, c * LANES:(c + 1) * LANES] = val.astype(_BF16)
                    project(rows, w)
            else:
                per = MXU_DEPTH // dil
                out_row = lax.broadcasted_iota(jnp.int32, (MXU_DEPTH, MXU_DEPTH), 0)
                in_row = lax.broadcasted_iota(jnp.int32, (MXU_DEPTH, MXU_DEPTH), 1)
                pick = (in_row == dil * (out_row % per) + out_row // per).astype(_BF16)
                for a in range(SEQ // MXU_DEPTH):
                    rows = slice(a * MXU_DEPTH, (a + 1) * MXU_DEPTH)
                    chunk = jnp.concatenate([h32_scr[c, rows, :].astype(_BF16) for c in range(N_PAIRS)], axis=1)
                    picked = jnp.dot(pick, chunk, preferred_element_type=_F32).astype(_BF16)
                    for r in range(dil):
                        hperm_scr[r * sub + a * per:r * sub + (a + 1) * per, :] = picked[r * per:(r + 1) * per]
                project(slice(0, SEQ), w)

    @pl.when(j % tiles_per_group != 0)
    def _():
        project(slice(0, SEQ), w_ref[...].astype(_BF16))


def _qkv_dil(x3d, gain, w):
    n_col = N_GROUPS * 3 * D_MODEL
    blocks_per_tile = DIL_TN // LANES
    return pl.pallas_call(
        _qkv_dil_kernel,
        grid=(BATCH, n_col // DIL_TN),
        in_specs=[
            pl.BlockSpec((1, SEQ, D_MODEL), lambda b, j: (b, 0, 0)),
            _resident((1, D_MODEL)),
            pl.BlockSpec((D_MODEL, DIL_TN), lambda b, j: (0, j)),
        ],
        out_specs=pl.BlockSpec((1, blocks_per_tile, SEQ, LANES), lambda b, j: (b, j, 0, 0)),
        out_shape=jax.ShapeDtypeStruct((BATCH, n_col // LANES, SEQ, LANES), _BF16),
        scratch_shapes=[pltpu.VMEM((N_PAIRS, SEQ, LANES), _F32), pltpu.VMEM((SEQ, D_MODEL), _BF16)],
        compiler_params=_params("parallel", "arbitrary"),
        name="qkv_dil",
    )(x3d, gain, w)


def _dil_kernel(slopes_ref, q0, k0, v0, q1, k1, v1, q2, k2, v2, o_ref, og_scr, lg_scr, bias_scr, cap_scr, p_scr,
                m_scr):
    pair = pl.program_id(0)
    qkv_refs = ((q0, k0, v0), (q1, k1, v1), (q2, k2, v2))
    lane = lax.broadcasted_iota(jnp.int32, (BAND_BLOCK, LANES), 1)
    first_head = lane < HEAD_DIM
    max_width = 2 * BAND_BLOCK
    per_iter = DIL_BLOCKS_PER_ITER

    @pl.when(pl.program_id(1) == 0)
    def _():
        rel = (lax.broadcasted_iota(jnp.int32, (BAND_BLOCK, max_width), 0)
               - lax.broadcasted_iota(jnp.int32, (BAND_BLOCK, max_width), 1))
        for grp, (window, dil) in enumerate(DIL_GROUPS):
            radius = window // (2 * dil)
            for var, off in enumerate(DIL_WINDOW_OFFSETS):
                dist = jnp.abs(rel + off)
                distf = dist.astype(_F32)
                cap_scr[grp, var] = jnp.where(dist <= radius, jnp.finfo(_F32).max, NEG_INF).astype(_F32)
                for hh in range(2):
                    head_rows = slice(hh * BAND_BLOCK, (hh + 1) * BAND_BLOCK)
                    bias_scr[grp, var, head_rows, :] = -((slopes_ref[2 * pair + hh] * dil) * distf) * LOG2E

    def place(grp, t):
        window, dil = DIL_GROUPS[grp]
        radius = window // (2 * dil)
        sub = SEQ // dil
        blocks_per_sub = sub // BAND_BLOCK
        width = min(max_width, sub)
        res, c = divmod(t, blocks_per_sub)
        kstart = min(max(c * BAND_BLOCK - radius, 0), sub - width)
        var = DIL_WINDOW_OFFSETS.index(c * BAND_BLOCK - kstart)
        key_rows = slice(res * sub + kstart, res * sub + kstart + width)
        if dil == 1:
            rows = slice(t * BAND_BLOCK, (t + 1) * BAND_BLOCK)
        else:
            rows = pl.ds(c * (BAND_BLOCK * dil) + res, BAND_BLOCK, stride=dil)
        return var, key_rows, width, rows

    def scores(slot, u, unit):
        bb, grp, t = unit
        q_ref, k_ref, _ = qkv_refs[grp]
        var, key_rows, width, _ = place(grp, t)
        q = q_ref[bb, 0, t * BAND_BLOCK:(t + 1) * BAND_BLOCK, :]
        s = lax.dot_general(_stack_heads(q, first_head), k_ref[bb, 0, key_rows, :], (((1,), (1,)), ((), ())),
                            preferred_element_type=_F32)
        cap = cap_scr[grp, var, :, :width]
        p, m = _masked_probs(s, bias_scr[grp, var, :, :width], jnp.concatenate([cap, cap], axis=0))
        p_scr[slot, u, :, :width] = p
        m_scr[slot, u] = jnp.where(first_head, m[:BAND_BLOCK], m[BAND_BLOCK:])

    def outputs(slot, u, unit):
        bb, grp, t = unit
        v_ref = qkv_refs[grp][2]
        _, key_rows, width, rows = place(grp, t)
        o, den = _stacked_pv(p_scr[slot, u, :, :width], v_ref[bb, 0, key_rows, :], first_head)
        lse = m_scr[slot, u] + jnp.log2(den)
        if DIL_GROUPS[grp][1] > 1:
            og_scr[grp, rows, :] = o
            lg_scr[grp, rows, :] = lse
        else:
            others = [g for g in range(N_GROUPS) if g != grp]
            lses = [lse] + [lg_scr[g, rows, :] for g in others]
            outs = [o] + [og_scr[g, rows, :] for g in others]
            mx = functools.reduce(jnp.maximum, lses)
            e = [jnp.exp2(x - mx) for x in lses]
            num = functools.reduce(lambda a, b: a + b, [w * x for w, x in zip(e, outs)])
            o_ref[bb, 0, rows, :] = (num / functools.reduce(lambda a, b: a + b, e)).astype(_BF16)

    order = sorted(range(N_GROUPS), key=lambda g: -DIL_GROUPS[g][1])
    assert DIL_GROUPS[order[-1]][1] == 1
    units = [(bb, grp, t) for bb in range(DIL_BATCHES_PER_STEP) for grp in order for t in range(SEQ // BAND_BLOCK)]
    stages = [units[n:n + per_iter] for n in range(0, len(units), per_iter)]
    for u, unit in enumerate(stages[0]):
        scores(0, u, unit)
    for n, stage in enumerate(stages):
        for u, unit in enumerate(stage):
            outputs(n % 2, u, unit)
            if n + 1 < len(stages):
                scores((n + 1) % 2, u, stages[n + 1][u])


def _dil_attention(qkv, slopes):
    slab = (DIL_BATCHES_PER_STEP, 1, SEQ, LANES)

    def spec(grp, which):
        base = (grp * 3 + which) * N_PAIRS
        return pl.BlockSpec(slab, lambda p, b: (b, base + p, 0, 0))

    return pl.pallas_call(
        _dil_kernel,
        grid=(N_PAIRS, BATCH // DIL_BATCHES_PER_STEP),
        in_specs=[pl.BlockSpec(memory_space=pltpu.SMEM)]
        + [spec(grp, which) for grp in range(N_GROUPS) for which in range(3)],
        out_specs=pl.BlockSpec(slab, lambda p, b: (b, p, 0, 0)),
        out_shape=jax.ShapeDtypeStruct((BATCH, N_PAIRS, SEQ, LANES), _BF16),
        scratch_shapes=[
            pltpu.VMEM((N_GROUPS, SEQ, LANES), _F32),
            pltpu.VMEM((N_GROUPS, SEQ, LANES), _F32),
            pltpu.VMEM((N_GROUPS, len(DIL_WINDOW_OFFSETS), 2 * BAND_BLOCK, 2 * BAND_BLOCK), _F32),
            pltpu.VMEM((N_GROUPS, len(DIL_WINDOW_OFFSETS), BAND_BLOCK, 2 * BAND_BLOCK), _F32),
            pltpu.VMEM((2, DIL_BLOCKS_PER_ITER, 2 * BAND_BLOCK, 2 * BAND_BLOCK), _BF16),
            pltpu.VMEM((2, DIL_BLOCKS_PER_ITER, BAND_BLOCK, LANES), _F32),
        ],
        compiler_params=_params("arbitrary", "arbitrary"),
        name="dil_attention",
    )(slopes, *([qkv] * 9))


def kernel(x, norm_mix_pre, norm_mix_post, norm_ffn_pre, norm_ffn_post, na_w_qkv, na_w_o, na_rpb, dil_w_qkv, dil_w_o,
           ffn_w_gate, ffn_w_up, ffn_w_down):
    x2d = x.reshape(N_TOK, D_MODEL)
    slopes = 2.0 ** (-8.0 * jnp.arange(1, N_HEADS + 1, dtype=_F32) / N_HEADS)

    def gain(g, layer):
        return g[layer].reshape(1, D_MODEL)

    for layer in range(DEPTH):
        j = layer // 2
        if layer % 2 == 0:
            qkv = _qkv_na(x2d, gain(norm_mix_pre, layer), na_w_qkv[j])
            o = _na_attention(qkv, (na_rpb[j].astype(_F32) * LOG2E).reshape(-1))
            w_o = na_w_o[j]
        else:
            qkv = _qkv_dil(x2d.reshape(BATCH, SEQ, D_MODEL), gain(norm_mix_pre, layer), dil_w_qkv[j])
            o = _dil_attention(qkv, slopes)
            w_o = dil_w_o[j]
        x2d = _mix_ffn(layer, o, x2d, w_o, gain(norm_mix_post, layer), gain(norm_ffn_pre, layer),
                       ffn_w_gate, ffn_w_up, ffn_w_down, gain(norm_ffn_post, layer))
    return x2d.reshape(BATCH, SEQ, D_MODEL)
```

```python
import functools

import jax
import jax.numpy as jnp
from jax import lax
from jax.experimental import pallas as pl
from jax.experimental.pallas import tpu as pltpu

D_MODEL = 1024
BATCH = 8
SEQ = 2048
DEPTH = 2
N_HEADS = 16
HEAD_DIM = 64
GRID_W = 64
GRID_ROWS = SEQ // GRID_W
NA_ROWS = 8
NA_COLS = 16
DIL_GROUPS = ((128, 1), (512, 4), (2048, 16))
N_GROUPS = len(DIL_GROUPS)
BAND_BLOCK = 128
D_FF = 2816
RMS_EPS = 1e-6
NEG_INF = -1e30
LOG2E = 1.4426950408889634
QK_SCALE = HEAD_DIM ** -0.5 * LOG2E

LANES = 128
N_PAIRS = D_MODEL // LANES
N_TOK = BATCH * SEQ
TM = 512
FFN_SLABS = 2
FF_SPLITS = ((0, 1536), (1536, 1280))
DIL_TN = 1024
CAST_COLS = 512
WEIGHT_STEPS = 8
SUBLANES = 8
MXU_DEPTH = 256
NA_ROWS_PER_ITER = 4
NA_BATCHES_PER_STEP = 4
DIL_BLOCKS_PER_ITER = 8
DIL_BATCHES_PER_STEP = 2
DIL_WINDOW_OFFSETS = (0, BAND_BLOCK // 2, BAND_BLOCK)
VMEM_LIMIT = 56 * 1024 * 1024

_BF16 = jnp.bfloat16
_F32 = jnp.float32


def _rms(x, g):
    ms = jnp.mean(x * x, axis=-1, keepdims=True)
    return (x * lax.rsqrt(ms + RMS_EPS)) * g


def _resident(shape):
    return pl.BlockSpec(shape, lambda *_: (0,) * len(shape), pipeline_mode=pl.Buffered(1))


def _params(*sem):
    return pltpu.CompilerParams(dimension_semantics=sem, vmem_limit_bytes=VMEM_LIMIT)


def _qkv_na_kernel(x_ref, g_ref, w_ref, o_ref, wbf_scr):
    @pl.when(pl.program_id(0) == 0)
    def _():
        for c in range(3 * D_MODEL // CAST_COLS):
            cols = slice(c * CAST_COLS, (c + 1) * CAST_COLS)
            wbf_scr[:, cols] = w_ref[:, cols].astype(_BF16)

    h = _rms(x_ref[...], g_ref[...]).astype(_BF16)
    for j in range(3):
        y = jnp.dot(h, wbf_scr[:, j * D_MODEL:(j + 1) * D_MODEL], preferred_element_type=_F32)
        if j == 0:
            y = y * QK_SCALE
        for c in range(N_PAIRS):
            o_ref[0, j * N_PAIRS + c] = y[:, c * LANES:(c + 1) * LANES].astype(_BF16)


def _qkv_na(x2d, gain, w):
    tiles_per_seq = SEQ // TM
    return pl.pallas_call(
        _qkv_na_kernel,
        grid=(N_TOK // TM,),
        in_specs=[
            pl.BlockSpec((TM, D_MODEL), lambda i: (i, 0)),
            _resident((1, D_MODEL)),
            _resident((D_MODEL, 3 * D_MODEL)),
        ],
        out_specs=pl.BlockSpec((1, 3 * N_PAIRS, TM, LANES),
                               lambda i: (i // tiles_per_seq, 0, i % tiles_per_seq, 0)),
        out_shape=jax.ShapeDtypeStruct((BATCH, 3 * N_PAIRS, SEQ, LANES), _BF16),
        scratch_shapes=[pltpu.VMEM((D_MODEL, 3 * D_MODEL), _BF16)],
        compiler_params=_params("arbitrary"),
        name="qkv_na",
    )(x2d, gain, w)


def _stack_heads(q, first_head):
    zero = jnp.zeros_like(q)
    return jnp.concatenate([jnp.where(first_head, q, zero), jnp.where(first_head, zero, q)], axis=0)


def _masked_probs(s, bias, cap):
    s = jnp.minimum(s + bias, cap)
    m = jnp.max(s, axis=-1, keepdims=True)
    return jnp.exp2(s - m).astype(_BF16), m


def _stacked_pv(p, v, first_head):
    rows = p.shape[0] // 2
    r = jnp.dot(p, jnp.concatenate([v, jnp.ones_like(v)], axis=1), preferred_element_type=_F32)
    num = jnp.where(first_head, r[:rows, :LANES], r[rows:, :LANES])
    den = jnp.where(first_head, r[:rows, LANES:], r[rows:, LANES:])
    return num, den


def _na_kernel(q_ref, k_ref, v_ref, rpb_ref, o_ref, bias_scr, cap_scr, p_scr):
    kh = NA_ROWS
    nk = kh * GRID_W
    per_iter = NA_ROWS_PER_ITER
    lane = lax.broadcasted_iota(jnp.int32, (GRID_W, LANES), 1)
    first_head = lane < HEAD_DIM

    @pl.when(pl.program_id(1) == 0)
    def _():
        n_row_off, n_col_off = 2 * kh - 1, 2 * NA_COLS - 1
        lane_off = lax.broadcasted_iota(jnp.int32, (1, LANES), 1) - (GRID_W - NA_COLS)
        diag_idx = jnp.clip(lane_off, 0, n_col_off - 1)
        for hh in range(2):
            for row_off in range(n_row_off):
                base = ((2 * pl.program_id(0) + hh) * n_row_off + row_off) * n_col_off
                diag = jnp.zeros((1, LANES), _F32)
                for c in range(n_col_off):
                    diag = jnp.where(diag_idx == c, rpb_ref[base + c], diag)
                diag = jnp.broadcast_to(diag, (GRID_W, LANES))
                tiles = [pltpu.roll(diag, 1 + GRID_W * (1 - half), 1, stride=1, stride_axis=0) for half in range(2)]
                for d in range(kh):
                    r = row_off - (kh - 1 - d)
                    if 0 <= r < kh:
                        half = slice((r % 2) * GRID_W, (r % 2 + 1) * GRID_W)
                        bias_scr[d, hh * GRID_W:(hh + 1) * GRID_W, r * GRID_W:(r + 1) * GRID_W] = tiles[r % 2][:, half]
        qcol = lax.broadcasted_iota(jnp.int32, (2 * GRID_W, LANES), 0) % GRID_W
        kcol = lax.broadcasted_iota(jnp.int32, (2 * GRID_W, LANES), 1) % GRID_W
        col_start = jnp.clip(qcol - NA_COLS // 2, 0, GRID_W - NA_COLS)
        col_ok = (kcol >= col_start) & (kcol < col_start + NA_COLS)
        cap_scr[...] = jnp.where(col_ok, jnp.finfo(_F32).max, NEG_INF).astype(_F32)

    def window(i):
        rs = min(max(i - kh // 2, 0), GRID_ROWS - kh)
        return rs, slice(rs * GRID_W, rs * GRID_W + nk)

    def scores(slot, u, row):
        bb, i = divmod(row, GRID_ROWS)
        rs, win = window(i)
        q = q_ref[bb, 0, i * GRID_W:(i + 1) * GRID_W, :]
        s = lax.dot_general(_stack_heads(q, first_head), k_ref[bb, 0, win, :], (((1,), (1,)), ((), ())),
                            preferred_element_type=_F32)
        cap = jnp.concatenate([cap_scr[...]] * (nk // LANES), axis=1)
        p_scr[slot, u], _ = _masked_probs(s, bias_scr[i - rs], cap)

    def outputs(slot, u, row):
        bb, i = divmod(row, GRID_ROWS)
        _, win = window(i)
        num, den = _stacked_pv(p_scr[slot, u], v_ref[bb, 0, win, :], first_head)
        o_ref[bb, 0, i * GRID_W:(i + 1) * GRID_W, :] = (num * (1.0 / den)).astype(_BF16)

    n_stages = NA_BATCHES_PER_STEP * GRID_ROWS // per_iter
    for u in range(per_iter):
        scores(0, u, u)
    for stage in range(n_stages):
        for u in range(per_iter):
            outputs(stage % 2, u, stage * per_iter + u)
            if stage + 1 < n_stages:
                scores((stage + 1) % 2, u, (stage + 1) * per_iter + u)


def _na_attention(qkv, rpb_flat):
    slab = (NA_BATCHES_PER_STEP, 1, SEQ, LANES)
    nk = NA_ROWS * GRID_W
    return pl.pallas_call(
        _na_kernel,
        grid=(N_PAIRS, BATCH // NA_BATCHES_PER_STEP),
        in_specs=[
            pl.BlockSpec(slab, lambda p, b: (b, p, 0, 0)),
            pl.BlockSpec(slab, lambda p, b: (b, N_PAIRS + p, 0, 0)),
            pl.BlockSpec(slab, lambda p, b: (b, 2 * N_PAIRS + p, 0, 0)),
            pl.BlockSpec(memory_space=pltpu.SMEM),
        ],
        out_specs=pl.BlockSpec(slab, lambda p, b: (b, p, 0, 0)),
        out_shape=jax.ShapeDtypeStruct((BATCH, N_PAIRS, SEQ, LANES), _BF16),
        scratch_shapes=[pltpu.VMEM((NA_ROWS, 2 * GRID_W, nk), _F32),
                        pltpu.VMEM((2 * GRID_W, LANES), _F32),
                        pltpu.VMEM((2, NA_ROWS_PER_ITER, 2 * GRID_W, nk), _BF16)],
        compiler_params=_params("arbitrary", "arbitrary"),
        name="na_attention",
    )(qkv, qkv, qkv, rpb_flat)


def _mix_ffn_kernel(o_ref, x_ref, wo_ref, gmix_ref, gpre_ref, wg_ref, wu_ref, wd_ref, gpost_ref, out_ref,
                    wo_scr, wg_scr, wu_scr, wd_scr):
    i = pl.program_id(0)

    @pl.when(i < WEIGHT_STEPS)
    def _():
        for src, dst in ((wo_ref, wo_scr), (wg_ref, wg_scr), (wu_ref, wu_scr), (wd_ref, wd_scr)):
            n = src.shape[0]
            dst[pl.ds(pl.multiple_of(i * n, n), n), :] = src[...].astype(_BF16)

    @pl.when(i >= WEIGHT_STEPS)
    def _():
        halves = [slice(t * (TM // FFN_SLABS), (t + 1) * (TM // FFN_SLABS)) for t in range(FFN_SLABS)]
        ys = [jnp.dot(jnp.concatenate([o_ref[0, c, rows, :] for c in range(N_PAIRS)], axis=-1), wo_scr[...],
                      preferred_element_type=_F32) for rows in halves]
        xs, hs = [], []
        for rows, y in zip(halves, ys):
            x = x_ref[rows, :] + _rms(y, gmix_ref[...])
            xs.append(x)
            hs.append(_rms(x, gpre_ref[...]).astype(_BF16))
        acc = [None] * FFN_SLABS
        for start, size in FF_SPLITS:
            gu = [(jnp.dot(h, wg_scr[:, start:start + size], preferred_element_type=_F32),
                   jnp.dot(h, wu_scr[:, start:start + size], preferred_element_type=_F32)) for h in hs]
            for t, (g, u) in enumerate(gu):
                a = ((g * jax.nn.sigmoid(g)) * u).astype(_BF16)
                part = jnp.dot(a, wd_scr[start:start + size, :], preferred_element_type=_F32)
                acc[t] = part if acc[t] is None else acc[t] + part
        for rows, x, y in zip(halves, xs, acc):
            out_ref[rows, :] = x + _rms(y, gpost_ref[...])


def _mix_ffn(layer, o_cbm, x2d, wo, gmix, gpre, wg, wu, wd, gpost):
    tiles_per_seq = SEQ // TM

    def tile(i):
        return jnp.maximum(i - WEIGHT_STEPS, 0)

    def chunk(i):
        return jnp.minimum(i, WEIGHT_STEPS - 1)

    def weight_chunk(rows, cols):
        return pl.BlockSpec((rows // WEIGHT_STEPS, cols), lambda i: (chunk(i), 0))

    def layer_weight_chunk(rows, cols):
        return pl.BlockSpec((None, rows // WEIGHT_STEPS, cols), lambda i: (layer, chunk(i), 0))

    return pl.pallas_call(
        _mix_ffn_kernel,
        grid=(WEIGHT_STEPS + N_TOK // TM,),
        in_specs=[
            pl.BlockSpec((1, N_PAIRS, TM, LANES), lambda i: (tile(i) // tiles_per_seq, 0, tile(i) % tiles_per_seq, 0)),
            pl.BlockSpec((TM, D_MODEL), lambda i: (tile(i), 0)),
            weight_chunk(D_MODEL, D_MODEL),
            _resident((1, D_MODEL)),
            _resident((1, D_MODEL)),
            layer_weight_chunk(D_MODEL, D_FF),
            layer_weight_chunk(D_MODEL, D_FF),
            layer_weight_chunk(D_FF, D_MODEL),
            _resident((1, D_MODEL)),
        ],
        out_specs=pl.BlockSpec((TM, D_MODEL), lambda i: (tile(i), 0)),
        out_shape=jax.ShapeDtypeStruct((N_TOK, D_MODEL), _F32),
        scratch_shapes=[pltpu.VMEM((D_MODEL, D_MODEL), _BF16), pltpu.VMEM((D_MODEL, D_FF), _BF16),
                        pltpu.VMEM((D_MODEL, D_FF), _BF16), pltpu.VMEM((D_FF, D_MODEL), _BF16)],
        compiler_params=_params("arbitrary"),
        name="mix_ffn",
    )(o_cbm, x2d, wo, gmix, gpre, wg, wu, wd, gpost)


def _qkv_dil_kernel(x_ref, g_ref, w_ref, o_ref, h32_scr, hperm_scr):
    assert DIL_GROUPS[0][1] == 1
    j = pl.program_id(1)
    tiles_per_group = 3 * D_MODEL // DIL_TN

    def project(rows, w):
        y = jnp.dot(hperm_scr[rows, :], w, preferred_element_type=_F32)
        is_q_tile = j % tiles_per_group < D_MODEL // DIL_TN
        y = y * jnp.where(is_q_tile, QK_SCALE, 1.0)
        for c in range(DIL_TN // LANES):
            o_ref[0, c, rows, :] = y[:, c * LANES:(c + 1) * LANES].astype(_BF16)

    @pl.when(j == 0)
    def _():
        w = w_ref[...].astype(_BF16)
        for t in range(SEQ // TM):
            rows = slice(t * TM, (t + 1) * TM)
            h = _rms(x_ref[0, rows, :], g_ref[...])
            for c in range(N_PAIRS):
                h32_scr[c, rows, :] = h[:, c * LANES:(c + 1) * LANES]
            hperm_scr[rows, :] = h.astype(_BF16)
            project(rows, w)

    for grp, (_, dil) in enumerate(DIL_GROUPS):
        if dil == 1:
            continue

        @pl.when(j == grp * tiles_per_group)
        def _(dil=dil):
            w = w_ref[...].astype(_BF16)
            sub = SEQ // dil
            if dil <= SUBLANES:
                for r in range(dil):
                    rows = slice(r * sub, (r + 1) * sub)
                    for c in range(N_PAIRS):
                        val = h32_scr[c, pl.ds(r, sub, stride=dil), :]
                        hperm_scr[rows, c * LANES:(c + 1) * LANES] = val.astype(_BF16)
                    project(rows, w)
            else:
                per = MXU_DEPTH // dil
                out_row = lax.broadcasted_iota(jnp.int32, (MXU_DEPTH, MXU_DEPTH), 0)
                in_row = lax.broadcasted_iota(jnp.int32, (MXU_DEPTH, MXU_DEPTH), 1)
                pick = (in_row == dil * (out_row % per) + out_row // per).astype(_BF16)
                for a in range(SEQ // MXU_DEPTH):
                    rows = slice(a * MXU_DEPTH, (a + 1) * MXU_DEPTH)
                    chunk = jnp.concatenate([h32_scr[c, rows, :].astype(_BF16) for c in range(N_PAIRS)], axis=1)
                    picked = jnp.dot(pick, chunk, preferred_element_type=_F32).astype(_BF16)
                    for r in range(dil):
                        hperm_scr[r * sub + a * per:r * sub + (a + 1) * per, :] = picked[r * per:(r + 1) * per]
                project(slice(0, SEQ), w)

    @pl.when(j % tiles_per_group != 0)
    def _():
        project(slice(0, SEQ), w_ref[...].astype(_BF16))


def _qkv_dil(x3d, gain, w):
    n_col = N_GROUPS * 3 * D_MODEL
    blocks_per_tile = DIL_TN // LANES
    return pl.pallas_call(
        _qkv_dil_kernel,
        grid=(BATCH, n_col // DIL_TN),
        in_specs=[
            pl.BlockSpec((1, SEQ, D_MODEL), lambda b, j: (b, 0, 0)),
            _resident((1, D_MODEL)),
            pl.BlockSpec((D_MODEL, DIL_TN), lambda b, j: (0, j)),
        ],
        out_specs=pl.BlockSpec((1, blocks_per_tile, SEQ, LANES), lambda b, j: (b, j, 0, 0)),
        out_shape=jax.ShapeDtypeStruct((BATCH, n_col // LANES, SEQ, LANES), _BF16),
        scratch_shapes=[pltpu.VMEM((N_PAIRS, SEQ, LANES), _F32), pltpu.VMEM((SEQ, D_MODEL), _BF16)],
        compiler_params=_params("parallel", "arbitrary"),
        name="qkv_dil",
    )(x3d, gain, w)


def _dil_kernel(slopes_ref, q0, k0, v0, q1, k1, v1, q2, k2, v2, o_ref, og_scr, lg_scr, dg_scr, bias_scr, cap_scr, p_scr,
                m_scr):
    pair = pl.program_id(0)
    qkv_refs = ((q0, k0, v0), (q1, k1, v1), (q2, k2, v2))
    lane = lax.broadcasted_iota(jnp.int32, (BAND_BLOCK, LANES), 1)
    first_head = lane < HEAD_DIM
    max_width = 2 * BAND_BLOCK
    per_iter = DIL_BLOCKS_PER_ITER

    @pl.when(pl.program_id(1) == 0)
    def _():
        rel = (lax.broadcasted_iota(jnp.int32, (BAND_BLOCK, max_width), 0)
               - lax.broadcasted_iota(jnp.int32, (BAND_BLOCK, max_width), 1))
        for grp, (window, dil) in enumerate(DIL_GROUPS):
            radius = window // (2 * dil)
            for var, off in enumerate(DIL_WINDOW_OFFSETS):
                dist = jnp.abs(rel + off)
                distf = dist.astype(_F32)
                cap_scr[grp, var] = jnp.where(dist <= radius, jnp.finfo(_F32).max, NEG_INF).astype(_F32)
                for hh in range(2):
                    head_rows = slice(hh * BAND_BLOCK, (hh + 1) * BAND_BLOCK)
                    bias_scr[grp, var, head_rows, :] = -((slopes_ref[2 * pair + hh] * dil) * distf) * LOG2E

    def place(grp, t):
        window, dil = DIL_GROUPS[grp]
        radius = window // (2 * dil)
        sub = SEQ // dil
        blocks_per_sub = sub // BAND_BLOCK
        width = min(max_width, sub)
        res, c = divmod(t, blocks_per_sub)
        kstart = min(max(c * BAND_BLOCK - radius, 0), sub - width)
        var = DIL_WINDOW_OFFSETS.index(c * BAND_BLOCK - kstart)
        key_rows = slice(res * sub + kstart, res * sub + kstart + width)
        if dil == 1:
            rows = slice(t * BAND_BLOCK, (t + 1) * BAND_BLOCK)
        else:
            rows = pl.ds(c * (BAND_BLOCK * dil) + res, BAND_BLOCK, stride=dil)
        return var, key_rows, width, rows

    def scores(slot, u, unit):
        bb, grp, t = unit
        q_ref, k_ref, _ = qkv_refs[grp]
        var, key_rows, width, _ = place(grp, t)
        q = q_ref[bb, 0, t * BAND_BLOCK:(t + 1) * BAND_BLOCK, :]
        s = lax.dot_general(_stack_heads(q, first_head), k_ref[bb, 0, key_rows, :], (((1,), (1,)), ((), ())),
                            preferred_element_type=_F32)
        cap = cap_scr[grp, var, :, :width]
        p, m = _masked_probs(s, bias_scr[grp, var, :, :width], jnp.concatenate([cap, cap], axis=0))
        p_scr[slot, u, :, :width] = p
        m_scr[slot, u] = jnp.where(first_head, m[:BAND_BLOCK], m[BAND_BLOCK:])

    def outputs(slot, u, unit):
        bb, grp, t = unit
        v_ref = qkv_refs[grp][2]
        _, key_rows, width, rows = place(grp, t)
        num, den = _stacked_pv(p_scr[slot, u, :, :width], v_ref[bb, 0, key_rows, :], first_head)
        m = m_scr[slot, u]
        if DIL_GROUPS[grp][1] > 1:
            og_scr[grp, rows, :] = num
            dg_scr[grp, rows, :] = den
            lg_scr[grp, rows, :] = m
        else:
            others = [g for g in range(N_GROUPS) if g != grp]
            maxima = [m] + [lg_scr[g, rows, :] for g in others]
            nums = [num] + [og_scr[g, rows, :] for g in others]
            dens = [den] + [dg_scr[g, rows, :] for g in others]
            mx = functools.reduce(jnp.maximum, maxima)
            w = [jnp.exp2(x - mx) for x in maxima]
            total = functools.reduce(lambda a, b: a + b, [a * b for a, b in zip(w, dens)])
            merged = functools.reduce(lambda a, b: a + b, [a * b for a, b in zip(w, nums)])
            o_ref[bb, 0, rows, :] = (merged / total).astype(_BF16)

    order = sorted(range(N_GROUPS), key=lambda g: -DIL_GROUPS[g][1])
    assert DIL_GROUPS[order[-1]][1] == 1
    units = [(bb, grp, t) for bb in range(DIL_BATCHES_PER_STEP) for grp in order for t in range(SEQ // BAND_BLOCK)]
    stages = [units[n:n + per_iter] for n in range(0, len(units), per_iter)]
    for u, unit in enumerate(stages[0]):
        scores(0, u, unit)
    for n, stage in enumerate(stages):
        for u, unit in enumerate(stage):
            outputs(n % 2, u, unit)
            if n + 1 < len(stages):
                scores((n + 1) % 2, u, stages[n + 1][u])


def _dil_attention(qkv, slopes):
    slab = (DIL_BATCHES_PER_STEP, 1, SEQ, LANES)

    def spec(grp, which):
        base = (grp * 3 + which) * N_PAIRS
        return pl.BlockSpec(slab, lambda p, b: (b, base + p, 0, 0))

    return pl.pallas_call(
        _dil_kernel,
        grid=(N_PAIRS, BATCH // DIL_BATCHES_PER_STEP),
        in_specs=[pl.BlockSpec(memory_space=pltpu.SMEM)]
        + [spec(grp, which) for grp in range(N_GROUPS) for which in range(3)],
        out_specs=pl.BlockSpec(slab, lambda p, b: (b, p, 0, 0)),
        out_shape=jax.ShapeDtypeStruct((BATCH, N_PAIRS, SEQ, LANES), _BF16),
        scratch_shapes=[
            pltpu.VMEM((N_GROUPS, SEQ, LANES), _F32),
            pltpu.VMEM((N_GROUPS, SEQ, LANES), _F32),
            pltpu.VMEM((N_GROUPS, SEQ, LANES), _F32),
            pltpu.VMEM((N_GROUPS, len(DIL_WINDOW_OFFSETS), 2 * BAND_BLOCK, 2 * BAND_BLOCK), _F32),
            pltpu.VMEM((N_GROUPS, len(DIL_WINDOW_OFFSETS), BAND_BLOCK, 2 * BAND_BLOCK), _F32),
            pltpu.VMEM((2, DIL_BLOCKS_PER_ITER, 2 * BAND_BLOCK, 2 * BAND_BLOCK), _BF16),
            pltpu.VMEM((2, DIL_BLOCKS_PER_ITER, BAND_BLOCK, LANES), _F32),
        ],
        compiler_params=_params("arbitrary", "arbitrary"),
        name="dil_attention",
    )(slopes, *([qkv] * 9))


def kernel(x, norm_mix_pre, norm_mix_post, norm_ffn_pre, norm_ffn_post, na_w_qkv, na_w_o, na_rpb, dil_w_qkv, dil_w_o,
           ffn_w_gate, ffn_w_up, ffn_w_down):
    x2d = x.reshape(N_TOK, D_MODEL)
    slopes = 2.0 ** (-8.0 * jnp.arange(1, N_HEADS + 1, dtype=_F32) / N_HEADS)

    def gain(g, layer):
        return g[layer].reshape(1, D_MODEL)

    for layer in range(DEPTH):
        j = layer // 2
        if layer % 2 == 0:
            qkv = _qkv_na(x2d, gain(norm_mix_pre, layer), na_w_qkv[j])
            o = _na_attention(qkv, (na_rpb[j].astype(_F32) * LOG2E).reshape(-1))
            w_o = na_w_o[j]
        else:
            qkv = _qkv_dil(x2d.reshape(BATCH, SEQ, D_MODEL), gain(norm_mix_pre, layer), dil_w_qkv[j])
            o = _dil_attention(qkv, slopes)
            w_o = dil_w_o[j]
        x2d = _mix_ffn(layer, o, x2d, w_o, gain(norm_mix_post, layer), gain(norm_ffn_pre, layer),
                       ffn_w_gate, ffn_w_up, ffn_w_down, gain(norm_ffn_post, layer))
    return x2d.reshape(BATCH, SEQ, D_MODEL)
```
